```python
import jax, jax.numpy as jnp
from jax import lax
import numpy as np

D_MODEL = 1024
BATCH = 8
SEQ = 16384
DEPTH = 4

CHUNK = 64
Q_BLOCK = 128
SGU_CHUNK = 128
HEAD_DIM = 64
D_ATTN = D_MODEL // 2
N_ATTN_HEADS = D_ATTN // HEAD_DIM
D_SGU = D_MODEL // 2
SGU_GROUP_DIM = 64
N_SGU_GROUPS = D_SGU // SGU_GROUP_DIM
D_MIX = D_ATTN + D_SGU
D_IN = 3 * D_ATTN + N_ATTN_HEADS + 2 * D_SGU
D_FF = -(-8 * D_MODEL // (3 * 256)) * 256
EPS = 1e-6

kernel_name = 'fox_gmlp_hybrid_trunk'


def rms_norm(x, g):
    xf = x.astype(jnp.float32)
    y = xf * lax.rsqrt(jnp.mean(xf * xf, axis=-1, keepdims=True) + EPS)
    return (y * g.astype(jnp.float32)).astype(x.dtype)


def layer_norm(x, g, b):
    xf = x.astype(jnp.float32)
    mu = jnp.mean(xf, axis=-1, keepdims=True)
    xc = xf - mu
    y = xc * lax.rsqrt(jnp.mean(xc * xc, axis=-1, keepdims=True) + EPS)
    return (y * g.astype(jnp.float32) + b.astype(jnp.float32)).astype(x.dtype)


def forgetting_attention(q, k, v, log_f):
    B, S, H, Dh = q.shape
    nb = S // Q_BLOCK
    c_bhs = jnp.cumsum(log_f, axis=1).transpose(0, 2, 1)
    q_blocks = q.reshape(B, nb, Q_BLOCK, H, Dh).transpose(1, 0, 2, 3, 4)
    cq_blocks = c_bhs.reshape(B, H, nb, Q_BLOCK).transpose(2, 0, 1, 3)
    starts = jnp.arange(nb, dtype=jnp.int32) * Q_BLOCK
    key_pos = jnp.arange(S, dtype=jnp.int32)
    scale = Dh ** -0.5

    def one_block(args):
        qb, cqb, start = args
        s = jnp.einsum('bqhd,bkhd->bhqk', qb, k).astype(jnp.float32) * scale
        s = s + cqb[..., :, None] - c_bhs[..., None, :]
        q_pos = start + jnp.arange(Q_BLOCK, dtype=jnp.int32)
        causal = key_pos[None, :] <= q_pos[:, None]
        s = jnp.where(causal, s, -jnp.inf)
        p = jax.nn.softmax(s, axis=-1).astype(v.dtype)
        return jnp.einsum('bhqk,bkhd->bqhd', p, v)

    out = lax.map(one_block, (q_blocks, cq_blocks, starts))
    return out.transpose(1, 0, 2, 3, 4).reshape(B, S, H * Dh)


def spatial_gating(z, ln_g, ln_b, w_s, b_s):
    B, S, _ = z.shape
    zu, zv = jnp.split(z, 2, axis=-1)
    zv = layer_norm(zv, ln_g, ln_b)
    nc = S // SGU_CHUNK
    zv = zv.reshape(B, nc, SGU_CHUNK, N_SGU_GROUPS, SGU_GROUP_DIM)
    pos = jnp.arange(SGU_CHUNK, dtype=jnp.int32) // CHUNK
    mask = (pos[None, :] <= pos[:, None]).astype(w_s.dtype)
    w = w_s * mask[None]
    mixed = jnp.einsum('gij,bcjgd->bcigd', w, zv) + b_s.T[None, None, :, :, None]
    return zu * mixed.reshape(B, S, D_SGU)


def _fwd_setup_inputs(seed: int = 0) -> dict:
    key = jax.random.key(seed)
    ks = jax.random.split(key, 14)
    f32 = jnp.float32
    nrm = lambda k, shape, s: jax.random.normal(k, shape, f32) * s
    head_bias = jnp.linspace(1.0, 5.0, N_ATTN_HEADS, dtype=f32)
    return {
        'x': jax.random.normal(ks[0], (BATCH, SEQ, D_MODEL), f32),
        'mix_norm_g': 1.0 + nrm(ks[1], (DEPTH, D_MODEL), 0.05),
        'w_in': nrm(ks[2], (DEPTH, D_MODEL, D_IN), D_MODEL ** -0.5),
        'b_f': head_bias[None, :] + nrm(ks[3], (DEPTH, N_ATTN_HEADS), 0.1),
        'sgu_ln_g': 1.0 + nrm(ks[4], (DEPTH, D_SGU), 0.05),
        'sgu_ln_b': nrm(ks[5], (DEPTH, D_SGU), 0.02),
        'w_s': nrm(ks[6], (DEPTH, N_SGU_GROUPS, SGU_CHUNK, SGU_CHUNK), 0.5 * SGU_CHUNK ** -0.5),
        'b_s': 1.0 + nrm(ks[7], (DEPTH, N_SGU_GROUPS, SGU_CHUNK), 0.1),
        'out_norm_g': 1.0 + nrm(ks[8], (DEPTH, D_MIX), 0.05),
        'w_out': nrm(ks[9], (DEPTH, D_MIX, D_MODEL), D_MIX ** -0.5),
        'ffn_norm_g': 1.0 + nrm(ks[10], (DEPTH, D_MODEL), 0.05),
        'w_gate_up': nrm(ks[11], (DEPTH, D_MODEL, 2 * D_FF), D_MODEL ** -0.5),
        'w_down': nrm(ks[12], (DEPTH, D_FF, D_MODEL), D_FF ** -0.5),
        'final_norm_g': 1.0 + nrm(ks[13], (D_MODEL,), 0.05),
    }


def _fwd_reference(x, mix_norm_g, w_in, b_f, sgu_ln_g, sgu_ln_b, w_s, b_s, out_norm_g, w_out,
              ffn_norm_g, w_gate_up, w_down, final_norm_g):
    B, S, _ = x.shape
    for l in range(DEPTH):
        xn = rms_norm(x, mix_norm_g[l])
        h = xn @ w_in[l]
        q, k, v, f_logit, z = jnp.split(
            h, [D_ATTN, 2 * D_ATTN, 3 * D_ATTN, 3 * D_ATTN + N_ATTN_HEADS], axis=-1)
        q = q.reshape(B, S, N_ATTN_HEADS, HEAD_DIM)
        k = k.reshape(B, S, N_ATTN_HEADS, HEAD_DIM)
        v = v.reshape(B, S, N_ATTN_HEADS, HEAD_DIM)
        log_f = jax.nn.log_sigmoid(f_logit.astype(jnp.float32) + b_f[l].astype(jnp.float32))
        attn = forgetting_attention(q, k, v, log_f)
        sgu = spatial_gating(jax.nn.gelu(z, approximate=False),
                             sgu_ln_g[l], sgu_ln_b[l], w_s[l], b_s[l])
        merged = jnp.concatenate(
            [rms_norm(attn, out_norm_g[l, :D_ATTN]), rms_norm(sgu, out_norm_g[l, D_ATTN:])], axis=-1)
        x = x + merged @ w_out[l]
        xn = rms_norm(x, ffn_norm_g[l])
        gate, up = jnp.split(xn @ w_gate_up[l], 2, axis=-1)
        x = x + (jax.nn.silu(gate) * up) @ w_down[l]
    return rms_norm(x, final_norm_g)


import jax as _jax
import jax.numpy as _jnp

TWIN_FORMAT = 'train_step'
FWD_PARAMS = ['x', 'mix_norm_g', 'w_in', 'b_f', 'sgu_ln_g', 'sgu_ln_b', 'w_s', 'b_s', 'out_norm_g', 'w_out', 'ffn_norm_g', 'w_gate_up', 'w_down', 'final_norm_g']
TWIN_WEIGHTS = ['mix_norm_g', 'w_in', 'b_f', 'sgu_ln_g', 'sgu_ln_b', 'w_s', 'b_s', 'out_norm_g', 'w_out', 'ffn_norm_g', 'w_gate_up', 'w_down', 'final_norm_g']
TWIN_DIFF_INPUT = 'x'
TWIN_INPUTS = ['x', 'mix_norm_g', 'w_in', 'b_f', 'sgu_ln_g', 'sgu_ln_b', 'w_s', 'b_s', 'out_norm_g', 'w_out', 'ffn_norm_g', 'w_gate_up', 'w_down', 'final_norm_g', 'loss_target', 'm_mix_norm_g', 'm_w_in', 'm_b_f', 'm_sgu_ln_g', 'm_sgu_ln_b', 'm_w_s', 'm_b_s', 'm_out_norm_g', 'm_w_out', 'm_ffn_norm_g', 'm_w_gate_up', 'm_w_down', 'm_final_norm_g', 'v_mix_norm_g', 'v_w_in', 'v_b_f', 'v_sgu_ln_g', 'v_sgu_ln_b', 'v_w_s', 'v_b_s', 'v_out_norm_g', 'v_w_out', 'v_ffn_norm_g', 'v_w_gate_up', 'v_w_down', 'v_final_norm_g']
TWIN_OUTPUTS = ['loss', 'grad_x', 'grad_mix_norm_g', 'grad_w_in', 'grad_b_f', 'grad_sgu_ln_g', 'grad_sgu_ln_b', 'grad_w_s', 'grad_b_s', 'grad_out_norm_g', 'grad_w_out', 'grad_ffn_norm_g', 'grad_w_gate_up', 'grad_w_down', 'grad_final_norm_g', 'delta_mix_norm_g', 'delta_w_in', 'delta_b_f', 'delta_sgu_ln_g', 'delta_sgu_ln_b', 'delta_w_s', 'delta_b_s', 'delta_out_norm_g', 'delta_w_out', 'delta_ffn_norm_g', 'delta_w_gate_up', 'delta_w_down', 'delta_final_norm_g', 'new_m_mix_norm_g', 'new_m_w_in', 'new_m_b_f', 'new_m_sgu_ln_g', 'new_m_sgu_ln_b', 'new_m_w_s', 'new_m_b_s', 'new_m_out_norm_g', 'new_m_w_out', 'new_m_ffn_norm_g', 'new_m_w_gate_up', 'new_m_w_down', 'new_m_final_norm_g', 'new_v_mix_norm_g', 'new_v_w_in', 'new_v_b_f', 'new_v_sgu_ln_g', 'new_v_sgu_ln_b', 'new_v_w_s', 'new_v_b_s', 'new_v_out_norm_g', 'new_v_w_out', 'new_v_ffn_norm_g', 'new_v_w_gate_up', 'new_v_w_down', 'new_v_final_norm_g']
TWIN_LEAF_KINDS = {'loss': 'loss', 'grad_x': 'grad_x', 'grad_mix_norm_g': 'grad_w', 'grad_w_in': 'grad_w', 'grad_b_f': 'grad_w', 'grad_sgu_ln_g': 'grad_w', 'grad_sgu_ln_b': 'grad_w', 'grad_w_s': 'grad_w', 'grad_b_s': 'grad_w', 'grad_out_norm_g': 'grad_w', 'grad_w_out': 'grad_w', 'grad_ffn_norm_g': 'grad_w', 'grad_w_gate_up': 'grad_w', 'grad_w_down': 'grad_w', 'grad_final_norm_g': 'grad_w', 'delta_mix_norm_g': 'delta_w', 'delta_w_in': 'delta_w', 'delta_b_f': 'delta_w', 'delta_sgu_ln_g': 'delta_w', 'delta_sgu_ln_b': 'delta_w', 'delta_w_s': 'delta_w', 'delta_b_s': 'delta_w', 'delta_out_norm_g': 'delta_w', 'delta_w_out': 'delta_w', 'delta_ffn_norm_g': 'delta_w', 'delta_w_gate_up': 'delta_w', 'delta_w_down': 'delta_w', 'delta_final_norm_g': 'delta_w', 'new_m_mix_norm_g': 'new_m', 'new_m_w_in': 'new_m', 'new_m_b_f': 'new_m', 'new_m_sgu_ln_g': 'new_m', 'new_m_sgu_ln_b': 'new_m', 'new_m_w_s': 'new_m', 'new_m_b_s': 'new_m', 'new_m_out_norm_g': 'new_m', 'new_m_w_out': 'new_m', 'new_m_ffn_norm_g': 'new_m', 'new_m_w_gate_up': 'new_m', 'new_m_w_down': 'new_m', 'new_m_final_norm_g': 'new_m', 'new_v_mix_norm_g': 'new_v', 'new_v_w_in': 'new_v', 'new_v_b_f': 'new_v', 'new_v_sgu_ln_g': 'new_v', 'new_v_sgu_ln_b': 'new_v', 'new_v_w_s': 'new_v', 'new_v_b_s': 'new_v', 'new_v_out_norm_g': 'new_v', 'new_v_w_out': 'new_v', 'new_v_ffn_norm_g': 'new_v', 'new_v_w_gate_up': 'new_v', 'new_v_w_down': 'new_v', 'new_v_final_norm_g': 'new_v'}


def _forward(args):
    return _fwd_reference(*[args[k] for k in FWD_PARAMS])


def _output_shape():
    def fwd():
        inp = _fwd_setup_inputs(0)
        return _fwd_reference(*[inp[k] for k in FWD_PARAMS])
    out = _jax.eval_shape(fwd)
    return out.shape, out.dtype

N_MICROBATCH = 1
ADAM_LR = 0.001
ADAM_B1 = 0.9
ADAM_B2 = 0.999
ADAM_EPS = 1e-08
ADAM_WD = 0.01
ADAM_STEP = 10
PER_EXAMPLE_BATCH_AXIS = {'x': 0, 'loss_target': 0}
SHARED_INPUTS = []
_WEIGHT_DTYPES = {'mix_norm_g': _jnp.float32, 'w_in': _jnp.float32, 'b_f': _jnp.float32, 'sgu_ln_g': _jnp.float32, 'sgu_ln_b': _jnp.float32, 'w_s': _jnp.float32, 'b_s': _jnp.float32, 'out_norm_g': _jnp.float32, 'w_out': _jnp.float32, 'ffn_norm_g': _jnp.float32, 'w_gate_up': _jnp.float32, 'w_down': _jnp.float32, 'final_norm_g': _jnp.float32}
MOMENT_SCALE = {'mix_norm_g': 4.906461e-01, 'w_in': 3.018517e-01, 'b_f': 1.089540e+00, 'sgu_ln_g': 8.846093e-02, 'sgu_ln_b': 8.696743e-02, 'w_s': 1.279229e-01, 'b_s': 1.415919e-01, 'out_norm_g': 7.840218e-01, 'w_out': 7.495235e-01, 'ffn_norm_g': 2.197440e-01, 'w_gate_up': 8.783931e-02, 'w_down': 1.541277e-01, 'final_norm_g': 1.310203e+02}


def _to_microbatches(a, axis):
    t = _jnp.moveaxis(a, axis, 0)
    t = t.reshape((N_MICROBATCH, t.shape[0] // N_MICROBATCH) + t.shape[1:])
    return _jnp.moveaxis(t, 1, axis + 1)


def setup_inputs(seed: int = 0) -> dict:
    inp = _fwd_setup_inputs(seed)
    key = _jax.random.fold_in(_jax.random.key(seed), 7919)
    shape, _ = _output_shape()
    out = dict(inp)
    out["loss_target"] = _jax.random.normal(_jax.random.fold_in(key, 0), shape, _jnp.float32)
    for i, name in enumerate(TWIN_WEIGHTS):
        w = inp[name].astype(_jnp.float32)
        if MOMENT_SCALE is None:
            s = _jnp.sqrt(_jnp.mean(_jnp.square(w)) + 1e-30)
        else:
            s = MOMENT_SCALE[name]
        km, kv = _jax.random.split(_jax.random.fold_in(key, i + 1))
        out[name] = w
        out["m_" + name] = s * _jax.random.normal(km, w.shape, _jnp.float32)
        out["v_" + name] = (s * s) * _jax.random.uniform(kv, w.shape, _jnp.float32, 0.5, 1.5)
    if N_MICROBATCH > 1:
        for name, axis in PER_EXAMPLE_BATCH_AXIS.items():
            out[name] = _to_microbatches(out[name], axis)
    return {'x': out['x'], 'mix_norm_g': out['mix_norm_g'], 'w_in': out['w_in'], 'b_f': out['b_f'], 'sgu_ln_g': out['sgu_ln_g'], 'sgu_ln_b': out['sgu_ln_b'], 'w_s': out['w_s'], 'b_s': out['b_s'], 'out_norm_g': out['out_norm_g'], 'w_out': out['w_out'], 'ffn_norm_g': out['ffn_norm_g'], 'w_gate_up': out['w_gate_up'], 'w_down': out['w_down'], 'final_norm_g': out['final_norm_g'], 'loss_target': out['loss_target'], 'm_mix_norm_g': out['m_mix_norm_g'], 'm_w_in': out['m_w_in'], 'm_b_f': out['m_b_f'], 'm_sgu_ln_g': out['m_sgu_ln_g'], 'm_sgu_ln_b': out['m_sgu_ln_b'], 'm_w_s': out['m_w_s'], 'm_b_s': out['m_b_s'], 'm_out_norm_g': out['m_out_norm_g'], 'm_w_out': out['m_w_out'], 'm_ffn_norm_g': out['m_ffn_norm_g'], 'm_w_gate_up': out['m_w_gate_up'], 'm_w_down': out['m_w_down'], 'm_final_norm_g': out['m_final_norm_g'], 'v_mix_norm_g': out['v_mix_norm_g'], 'v_w_in': out['v_w_in'], 'v_b_f': out['v_b_f'], 'v_sgu_ln_g': out['v_sgu_ln_g'], 'v_sgu_ln_b': out['v_sgu_ln_b'], 'v_w_s': out['v_w_s'], 'v_b_s': out['v_b_s'], 'v_out_norm_g': out['v_out_norm_g'], 'v_w_out': out['v_w_out'], 'v_ffn_norm_g': out['v_ffn_norm_g'], 'v_w_gate_up': out['v_w_gate_up'], 'v_w_down': out['v_w_down'], 'v_final_norm_g': out['v_final_norm_g']}


def _loss(weights, diff, rest, loss_target):
    with _jax.named_scope("forward"):
        args = {**rest, TWIN_DIFF_INPUT: diff, **{k: w.astype(_WEIGHT_DTYPES[k]) for k, w in weights.items()}}
        y = _forward(args)
    with _jax.named_scope("loss_head"):
        err = _jnp.square(y.astype(_jnp.float32) - loss_target)
        return 0.5 * _jnp.sum(_jnp.mean(err, axis=-1)) if err.ndim else 0.5 * err


def _adamw(w, g, m, v):
    m = ADAM_B1 * m + (1.0 - ADAM_B1) * g
    v = ADAM_B2 * v + (1.0 - ADAM_B2) * _jnp.square(g)
    m_hat = m / (1.0 - ADAM_B1 ** ADAM_STEP)
    v_hat = v / (1.0 - ADAM_B2 ** ADAM_STEP)
    delta = -ADAM_LR * (m_hat / (_jnp.sqrt(v_hat) + ADAM_EPS) + ADAM_WD * w)
    return delta, m, v


def reference(x, mix_norm_g, w_in, b_f, sgu_ln_g, sgu_ln_b, w_s, b_s, out_norm_g, w_out, ffn_norm_g, w_gate_up, w_down, final_norm_g, loss_target, m_mix_norm_g, m_w_in, m_b_f, m_sgu_ln_g, m_sgu_ln_b, m_w_s, m_b_s, m_out_norm_g, m_w_out, m_ffn_norm_g, m_w_gate_up, m_w_down, m_final_norm_g, v_mix_norm_g, v_w_in, v_b_f, v_sgu_ln_g, v_sgu_ln_b, v_w_s, v_b_s, v_out_norm_g, v_w_out, v_ffn_norm_g, v_w_gate_up, v_w_down, v_final_norm_g):
    given = dict(x=x, mix_norm_g=mix_norm_g, w_in=w_in, b_f=b_f, sgu_ln_g=sgu_ln_g, sgu_ln_b=sgu_ln_b, w_s=w_s, b_s=b_s, out_norm_g=out_norm_g, w_out=w_out, ffn_norm_g=ffn_norm_g, w_gate_up=w_gate_up, w_down=w_down, final_norm_g=final_norm_g, loss_target=loss_target, m_mix_norm_g=m_mix_norm_g, m_w_in=m_w_in, m_b_f=m_b_f, m_sgu_ln_g=m_sgu_ln_g, m_sgu_ln_b=m_sgu_ln_b, m_w_s=m_w_s, m_b_s=m_b_s, m_out_norm_g=m_out_norm_g, m_w_out=m_w_out, m_ffn_norm_g=m_ffn_norm_g, m_w_gate_up=m_w_gate_up, m_w_down=m_w_down, m_final_norm_g=m_final_norm_g, v_mix_norm_g=v_mix_norm_g, v_w_in=v_w_in, v_b_f=v_b_f, v_sgu_ln_g=v_sgu_ln_g, v_sgu_ln_b=v_sgu_ln_b, v_w_s=v_w_s, v_b_s=v_b_s, v_out_norm_g=v_out_norm_g, v_w_out=v_w_out, v_ffn_norm_g=v_ffn_norm_g, v_w_gate_up=v_w_gate_up, v_w_down=v_w_down, v_final_norm_g=v_final_norm_g)
    weights = {n: given[n] for n in TWIN_WEIGHTS}
    shared = {n: given[n] for n in SHARED_INPUTS}
    per_example = {n: given[n] for n in ['x']}
    grad_fn = _jax.value_and_grad(_loss, argnums=(0, 1))

    def one_microbatch(ex, loss_target):
        ex = dict(ex)
        diff = ex.pop(TWIN_DIFF_INPUT)
        return grad_fn(weights, diff, {**shared, **ex}, loss_target)

    if N_MICROBATCH == 1:
        loss, (grad_w, grad_x) = one_microbatch(per_example, given["loss_target"])
    else:
        def body(carry, xs):
            loss_sum, grad_sum = carry
            l_k, (gw_k, gx_k) = one_microbatch(xs[0], xs[1])
            with _jax.named_scope("update"):
                return (loss_sum + l_k, _jax.tree.map(_jnp.add, grad_sum, gw_k)), gx_k

        init = (_jnp.zeros((), _jnp.float32), _jax.tree.map(_jnp.zeros_like, weights))
        (loss, grad_w), grad_x = _jax.lax.scan(body, init, (per_example, given["loss_target"]))
    with _jax.named_scope("update"):
        delta_w, new_m, new_v = {}, {}, {}
        for n in TWIN_WEIGHTS:
            delta_w[n], new_m[n], new_v[n] = _adamw(weights[n], grad_w[n], given["m_" + n], given["v_" + n])
    return (loss, grad_x, *[grad_w[n] for n in TWIN_WEIGHTS], *[delta_w[n] for n in TWIN_WEIGHTS],
            *[new_m[n] for n in TWIN_WEIGHTS], *[new_v[n] for n in TWIN_WEIGHTS])
```

```python
import functools
import math

import jax
import jax.numpy as jnp
from jax import lax
from jax.experimental import pallas as pl
from jax.experimental.pallas import tpu as pltpu

F32, BF16 = jnp.float32, jnp.bfloat16
HIGHEST = lax.Precision.HIGHEST
MESH = pl.DeviceIdType.MESH
ANY = pl.BlockSpec(memory_space=pl.ANY)
SDS = jax.ShapeDtypeStruct

N_DEV = 8
DEPTH = 4
D_MODEL = 1024
D_ATTN = 512
D_SGU = 512
N_HEADS = 8
HEAD_DIM = 64
HEAD_PAIRS = N_HEADS // 2
SGU_CHUNK = 128
SGU_BLOCK = 64
N_GROUPS = 8
GROUP_DIM = 64
D_FF = 2816
FF_CHUNK = 1408
N_FF_CHUNKS = D_FF // FF_CHUNK
D_IN = 3 * D_ATTN + N_HEADS + 2 * D_SGU
LANES = 128
EPS = 1e-6
QK_SCALE = HEAD_DIM ** -0.5
INV_SQRT2 = 1.0 / math.sqrt(2.0)
INV_SQRT_2PI = 1.0 / math.sqrt(2.0 * math.pi)

ADAM_LR = 0.001
ADAM_B1 = 0.9
ADAM_B2 = 0.999
ADAM_EPS = 1e-08
ADAM_WD = 0.01
ADAM_STEP = 10

VMEM_LIMIT_BYTES = 56 * 1024 * 1024


def _params(*sem):
    return pltpu.CompilerParams(dimension_semantics=sem or None, vmem_limit_bytes=VMEM_LIMIT_BYTES)


def _dot(a, b):
    return jnp.dot(a, b, preferred_element_type=F32)


def _dot_nt(a, b):
    return lax.dot_general(a, b, (((1,), (1,)), ((), ())), preferred_element_type=F32)


def _dot_tn(a, b):
    return lax.dot_general(a, b, (((0,), (0,)), ((), ())), preferred_element_type=F32)


def _dot_exact(a, b):
    return jnp.dot(a, b, precision=HIGHEST, preferred_element_type=F32)


def _mean(v):
    return jnp.mean(v, axis=-1, keepdims=True)


def _sigmoid(v):
    return 1.0 / (1.0 + jnp.exp(-v))


def _row_spec(tm, n):
    return pl.BlockSpec((tm, n), lambda i: (i, 0))


def _rev_spec(tm, n, nt):
    return pl.BlockSpec((tm, n), lambda i: (nt - 1 - i, 0))


def _const_spec(shape):
    return pl.BlockSpec(shape, lambda i: (0,) * len(shape))


def _tiles(t):
    return min(256, t), min(512, t)


def _group_indicator():
    r = lax.broadcasted_iota(jnp.int32, (D_ATTN, LANES), 0)
    c = lax.broadcasted_iota(jnp.int32, (D_ATTN, LANES), 1)
    return ((r >> 6) == c).astype(F32)


def _sgu_mix(w_ref, zc, lane_grp):
    out = jnp.zeros((SGU_CHUNK, D_SGU), F32)
    for g in range(N_GROUPS):
        out = out + jnp.where(lane_grp == g, _dot(w_ref[g], zc), 0.0)
    return out


def _fwd_a(x, gmix, wqkv, wf, bf, wz, lng, lnb, wsm, bsf, *, tm):
    t = x.shape[0]
    nt = t // tm
    nch = tm // SGU_CHUNK

    def body(x_ref, gmix_ref, wqkv_ref, wf_ref, bf_ref, wz_ref, lng_ref, lnb_ref, wsm_ref, bsf_ref,
             q_ref, k_ref, v_ref, fl_ref, c_ref, z_ref, sgu_ref, carry_ref):
        @pl.when(pl.program_id(0) == 0)
        def _():
            carry_ref[...] = jnp.zeros_like(carry_ref)

        xt = x_ref[...]
        r = lax.rsqrt(_mean(xt * xt) + EPS)
        xn = ((xt * r) * gmix_ref[...]).astype(BF16)
        qkv = _dot(xn, wqkv_ref[...])
        q_ref[...] = (qkv[:, :D_ATTN] * QK_SCALE).astype(BF16)
        k_ref[...] = qkv[:, D_ATTN:2 * D_ATTN].astype(BF16)
        v_ref[...] = qkv[:, 2 * D_ATTN:].astype(BF16)

        fl = _dot(xn, wf_ref[...]) + bf_ref[...]
        fl_ref[...] = fl
        logf = jnp.minimum(fl, 0.0) - jnp.log1p(jnp.exp(-jnp.abs(fl)))
        row = lax.broadcasted_iota(jnp.int32, (tm, tm), 0)
        col = lax.broadcasted_iota(jnp.int32, (tm, tm), 1)
        c = _dot_exact((col <= row).astype(F32), logf) + carry_ref[...]
        c_ref[...] = c
        carry_ref[...] = c[tm - 1:tm, :]

        z = _dot(xn, wz_ref[...])
        z_ref[...] = z
        zg = 0.5 * z * (1.0 + lax.erf(z * INV_SQRT2))
        zu = zg[:, :D_SGU]
        zv = zg[:, D_SGU:]
        xc = zv - _mean(zv)
        zvn = ((xc * lax.rsqrt(_mean(xc * xc) + EPS)) * lng_ref[...] + lnb_ref[...]).astype(BF16)
        lane_grp = lax.broadcasted_iota(jnp.int32, (SGU_CHUNK, D_SGU), 1) >> 6
        for ch in range(nch):
            rows = slice(ch * SGU_CHUNK, (ch + 1) * SGU_CHUNK)
            mixed = _sgu_mix(wsm_ref, zvn[rows, :], lane_grp) + bsf_ref[...]
            sgu_ref[rows, :] = zu[rows, :] * mixed

    return pl.pallas_call(
        body, name="fwd_a", grid=(nt,),
        in_specs=[_row_spec(tm, D_MODEL), _const_spec((1, D_MODEL)), _const_spec((D_MODEL, 3 * D_ATTN)),
                  _const_spec((D_MODEL, LANES)), _const_spec((1, LANES)), _const_spec((D_MODEL, 2 * D_SGU)),
                  _const_spec((1, D_SGU)), _const_spec((1, D_SGU)), _const_spec((N_GROUPS, SGU_CHUNK, SGU_CHUNK)),
                  _const_spec((SGU_CHUNK, D_SGU))],
        out_specs=[_row_spec(tm, D_ATTN), _row_spec(tm, D_ATTN), _row_spec(tm, D_ATTN), _row_spec(tm, LANES),
                   _row_spec(tm, LANES), _row_spec(tm, 2 * D_SGU), _row_spec(tm, D_SGU)],
        out_shape=[SDS((t, D_ATTN), BF16), SDS((t, D_ATTN), BF16), SDS((t, D_ATTN), BF16), SDS((t, LANES), F32),
                   SDS((t, LANES), F32), SDS((t, 2 * D_SGU), F32), SDS((t, D_SGU), F32)],
        scratch_shapes=[pltpu.VMEM((1, LANES), F32)],
        compiler_params=_params("arbitrary"),
    )(x, gmix, wqkv, wf, bf, wz, lng, lnb, wsm, bsf)


def _attn_fwd(q, k, v, c, c_row, *, tq):
    t = q.shape[0]
    nq = t // tq

    def body(q_ref, k_ref, v_ref, c_ref, cr_ref, o_ref, lse_ref):
        p = pl.program_id(0)
        i = pl.program_id(1)
        lane = lax.broadcasted_iota(jnp.int32, (tq, LANES), 1)
        is_a = lane < HEAD_DIM
        q2 = q_ref[...]
        zero = jnp.zeros_like(q2)
        q_h = (jnp.where(is_a, q2, zero), jnp.where(is_a, zero, q2))
        cc = c_ref[...]
        cq = tuple(jnp.sum(jnp.where(lane == 2 * p + h, cc, 0.0), axis=1, keepdims=True) for h in range(2))
        rowq = lax.broadcasted_iota(jnp.int32, (tq, tq), 0)
        colk = lax.broadcasted_iota(jnp.int32, (tq, tq), 1)

        def step(j, carry, masked):
            start = pl.multiple_of(j * tq, tq)
            kb = k_ref[pl.ds(start, tq), :]
            vb = v_ref[pl.ds(start, tq), :]
            new = []
            for h in range(2):
                m, l, acc = carry[h]
                s = _dot_nt(q_h[h], kb)
                s = (s + cq[h]) - cr_ref[0, h, pl.ds(j, 1), :]
                if masked:
                    s = jnp.where(colk <= rowq, s, -jnp.inf)
                m_new = jnp.maximum(m, jnp.max(s, axis=1, keepdims=True))
                alpha = jnp.exp(m - m_new)
                pr = jnp.exp(s - m_new)
                l_new = alpha * l + jnp.sum(pr, axis=1, keepdims=True)
                acc_new = acc * alpha + _dot(pr.astype(BF16), vb)
                new.append((m_new, l_new, acc_new))
            return tuple(new)

        init = tuple((jnp.full((tq, 1), -jnp.inf, F32), jnp.zeros((tq, 1), F32), jnp.zeros((tq, LANES), F32))
                     for _ in range(2))
        carry = lax.fori_loop(0, i, lambda j, cr: step(j, cr, False), init)
        (m_a, l_a, acc_a), (m_b, l_b, acc_b) = step(i, carry, True)
        o_ref[...] = jnp.where(is_a, acc_a / l_a, acc_b / l_b)
        lse_a = m_a + jnp.log(l_a)
        lse_b = m_b + jnp.log(l_b)
        lse_ref[0] = jnp.where(lane == 0, lse_a, jnp.where(lane == 1, lse_b, 0.0))

    return pl.pallas_call(
        body, name="attn_fwd", grid=(HEAD_PAIRS, nq),
        in_specs=[pl.BlockSpec((tq, LANES), lambda p, i: (i, p)),
                  pl.BlockSpec((t, LANES), lambda p, i: (0, p)),
                  pl.BlockSpec((t, LANES), lambda p, i: (0, p)),
                  pl.BlockSpec((tq, LANES), lambda p, i: (i, 0)),
                  pl.BlockSpec((1, 2, nq, tq), lambda p, i: (p, 0, 0, 0))],
        out_specs=[pl.BlockSpec((tq, LANES), lambda p, i: (i, p)),
                   pl.BlockSpec((1, tq, LANES), lambda p, i: (p, i, 0))],
        out_shape=[SDS((t, D_ATTN), F32), SDS((HEAD_PAIRS, t, LANES), F32)],
        compiler_params=_params("arbitrary", "arbitrary"),
    )(q, k, v, c, c_row)


def _attn_bwd(q, k, v, do, c, lse_row, delta_row, c_row, *, tq):
    t = q.shape[0]
    nq = t // tq

    def body(q_ref, do_ref, k_ref, v_ref, c_ref, lse_ref, dl_ref, cr_ref, dqt_ref, dk_ref, dv_ref, dck_ref, dcq_ref):
        p = pl.program_id(0)
        j = pl.program_id(1)

        @pl.when(j == 0)
        def _():
            dqt_ref[...] = jnp.zeros_like(dqt_ref)
            dcq_ref[...] = jnp.zeros_like(dcq_ref)

        lane = lax.broadcasted_iota(jnp.int32, (tq, LANES), 1)
        is_a = lane < HEAD_DIM
        kb = k_ref[...]
        vb = v_ref[...]
        zero = jnp.zeros_like(kb)
        k_h = (jnp.where(is_a, kb, zero), jnp.where(is_a, zero, kb))
        v_h = (jnp.where(is_a, vb, zero), jnp.where(is_a, zero, vb))
        kt_h = tuple(kh.astype(F32).T.astype(BF16) for kh in k_h)
        cc = c_ref[...]
        ck = tuple(jnp.sum(jnp.where(lane == 2 * p + h, cc, 0.0), axis=1, keepdims=True) for h in range(2))
        rowk = lax.broadcasted_iota(jnp.int32, (tq, tq), 0)
        colq = lax.broadcasted_iota(jnp.int32, (tq, tq), 1)

        def step(i, carry, masked):
            start = pl.multiple_of(i * tq, tq)
            q2 = q_ref[pl.ds(start, tq), :]
            do2 = do_ref[pl.ds(start, tq), :]
            new = []
            dq_t = jnp.zeros((LANES, tq), F32)
            for h in range(2):
                dk_acc, dv_acc, dc_acc = carry[h]
                st = _dot_nt(k_h[h], q2)
                st = ((st + cr_ref[0, h, pl.ds(i, 1), :]) - ck[h]) - lse_ref[0, h, pl.ds(i, 1), :]
                if masked:
                    st = jnp.where(rowk <= colq, st, -jnp.inf)
                pt = jnp.exp(st)
                dpt = _dot_nt(v_h[h], do2)
                dst = pt * (dpt - dl_ref[0, h, pl.ds(i, 1), :])
                dsb = dst.astype(BF16)
                dv_acc = dv_acc + _dot(pt.astype(BF16), do2)
                dk_acc = dk_acc + _dot(dsb, q2)
                dc_acc = dc_acc + jnp.sum(dst, axis=1, keepdims=True)
                dcq_ref[0, h, pl.ds(i, 1), :] += jnp.sum(dst, axis=0, keepdims=True)
                dq_t = dq_t + _dot(kt_h[h], dsb)
                new.append((dk_acc, dv_acc, dc_acc))
            dqt_ref[0, i] += dq_t
            return tuple(new)

        init = tuple((jnp.zeros((tq, LANES), F32), jnp.zeros((tq, LANES), F32), jnp.zeros((tq, 1), F32))
                     for _ in range(2))
        carry = step(j, init, True)
        (dk_a, dv_a, dc_a), (dk_b, dv_b, dc_b) = lax.fori_loop(j + 1, nq, lambda i, cr: step(i, cr, False), carry)
        dk_ref[...] = jnp.where(is_a, dk_a, dk_b).astype(BF16)
        dv_ref[...] = jnp.where(is_a, dv_a, dv_b).astype(BF16)
        dck_ref[0] = jnp.where(lane == 0, -dc_a, jnp.where(lane == 1, -dc_b, 0.0))

    rows = pl.BlockSpec((1, 2, nq, tq), lambda p, j: (p, 0, 0, 0))
    return pl.pallas_call(
        body, name="attn_bwd", grid=(HEAD_PAIRS, nq),
        in_specs=[pl.BlockSpec((t, LANES), lambda p, j: (0, p)),
                  pl.BlockSpec((t, LANES), lambda p, j: (0, p)),
                  pl.BlockSpec((tq, LANES), lambda p, j: (j, p)),
                  pl.BlockSpec((tq, LANES), lambda p, j: (j, p)),
                  pl.BlockSpec((tq, LANES), lambda p, j: (j, 0)),
                  rows, rows, rows],
        out_specs=[pl.BlockSpec((1, nq, LANES, tq), lambda p, j: (p, 0, 0, 0)),
                   pl.BlockSpec((tq, LANES), lambda p, j: (j, p)),
                   pl.BlockSpec((tq, LANES), lambda p, j: (j, p)),
                   pl.BlockSpec((1, tq, LANES), lambda p, j: (p, j, 0)), rows],
        out_shape=[SDS((HEAD_PAIRS, nq, LANES, tq), F32), SDS((t, D_ATTN), BF16), SDS((t, D_ATTN), BF16),
                   SDS((HEAD_PAIRS, t, LANES), F32), SDS((HEAD_PAIRS, 2, nq, tq), F32)],
        compiler_params=_params("arbitrary", "arbitrary"),
    )(q, do, k, v, c, lse_row, delta_row, c_row)


def _load_weights_once(pairs):
    @pl.when(pl.program_id(0) == 0)
    def _():
        for src, dst in pairs:
            pltpu.sync_copy(src, dst)


def _fwd_b(x, o, sgu, gout, gffn, w_out, w_gu, w_dn, *, tm):
    t = x.shape[0]
    nt = t // tm

    def body(x_ref, o_ref, s_ref, gout_ref, gffn_ref, wout_hbm, wgu_hbm, wdn_hbm,
             x1_ref, x2_ref, gu_ref, wout, wgu, wdn):
        _load_weights_once(((wout_hbm, wout), (wgu_hbm, wgu), (wdn_hbm, wdn)))
        ov = o_ref[...]
        sv = s_ref[...]
        mo = ((ov * lax.rsqrt(_mean(ov * ov) + EPS)) * gout_ref[:, :D_ATTN]).astype(BF16)
        ms = ((sv * lax.rsqrt(_mean(sv * sv) + EPS)) * gout_ref[:, D_ATTN:]).astype(BF16)
        x1 = x_ref[...] + (_dot(mo, wout[:D_ATTN, :]) + _dot(ms, wout[D_ATTN:, :]))
        x1_ref[...] = x1
        xn2 = ((x1 * lax.rsqrt(_mean(x1 * x1) + EPS)) * gffn_ref[...]).astype(BF16)
        y = jnp.zeros((tm, D_MODEL), F32)
        for n in range(N_FF_CHUNKS):
            lo, hi = n * FF_CHUNK, (n + 1) * FF_CHUNK
            gate = _dot(xn2, wgu[:, lo:hi])
            up = _dot(xn2, wgu[:, D_FF + lo:D_FF + hi])
            gu_ref[:, lo:hi] = gate
            gu_ref[:, D_FF + lo:D_FF + hi] = up
            a = ((gate * _sigmoid(gate)) * up).astype(BF16)
            y = y + _dot(a, wdn[lo:hi, :])
        x2_ref[...] = x1 + y

    return pl.pallas_call(
        body, name="fwd_b", grid=(nt,),
        in_specs=[_row_spec(tm, D_MODEL), _row_spec(tm, D_ATTN), _row_spec(tm, D_SGU),
                  _const_spec((1, D_MODEL)), _const_spec((1, D_MODEL)), ANY, ANY, ANY],
        out_specs=[_row_spec(tm, D_MODEL), _row_spec(tm, D_MODEL), _row_spec(tm, 2 * D_FF)],
        out_shape=[SDS((t, D_MODEL), F32), SDS((t, D_MODEL), F32), SDS((t, 2 * D_FF), F32)],
        scratch_shapes=[pltpu.VMEM((D_MODEL, D_MODEL), BF16), pltpu.VMEM((D_MODEL, 2 * D_FF), BF16),
                        pltpu.VMEM((D_FF, D_MODEL), BF16)],
        compiler_params=_params("arbitrary"),
    )(x, o, sgu, gout, gffn, w_out, w_gu, w_dn)


def _loss_bwd(x, tgt, gfin, *, tm):
    t = x.shape[0]
    nt = t // tm

    def body(x_ref, t_ref, g_ref, dx_ref, loss_ref, dg_ref):
        @pl.when(pl.program_id(0) == 0)
        def _():
            loss_ref[...] = jnp.zeros_like(loss_ref)
            dg_ref[...] = jnp.zeros_like(dg_ref)

        xt = x_ref[...]
        g = g_ref[...]
        r = lax.rsqrt(_mean(xt * xt) + EPS)
        xh = xt * r
        err = xh * g - t_ref[...]
        loss_ref[...] += 0.5 * jnp.sum(_mean(err * err), axis=0, keepdims=True)
        dy = err * (1.0 / D_MODEL)
        dg_ref[...] += jnp.sum(dy * xh, axis=0, keepdims=True)
        dyg = dy * g
        dx_ref[...] = r * (dyg - xh * _mean(dyg * xh))

    return pl.pallas_call(
        body, name="loss_bwd", grid=(nt,),
        in_specs=[_row_spec(tm, D_MODEL), _row_spec(tm, D_MODEL), _const_spec((1, D_MODEL))],
        out_specs=[_row_spec(tm, D_MODEL), _const_spec((1, 1)), _const_spec((1, D_MODEL))],
        out_shape=[SDS((t, D_MODEL), F32), SDS((1, 1), F32), SDS((1, D_MODEL), F32)],
        compiler_params=_params("arbitrary"),
    )(x, tgt, gfin)


def _bwd_b(dx2, x1, gu, o, sgu, gout, gffn, w_out, w_gu, w_dn, *, tm):
    t = dx2.shape[0]
    nt = t // tm

    def body(dx2_ref, x1_ref, gu_ref, o_ref, s_ref, gout_ref, gffn_ref, wout_hbm, wgu_hbm, wdn_hbm,
             dx1_ref, dx2b_ref, a_ref, dgu_ref, xn2_ref, mrg_ref, dx1b_ref, do_ref, dl_ref, dsgu_ref,
             dgffn_ref, dgout_ref, wout, wgu, wdn):
        _load_weights_once(((wout_hbm, wout), (wgu_hbm, wgu), (wdn_hbm, wdn)))

        @pl.when(pl.program_id(0) == 0)
        def _():
            dgffn_ref[...] = jnp.zeros_like(dgffn_ref)
            dgout_ref[...] = jnp.zeros_like(dgout_ref)

        dx2 = dx2_ref[...]
        dx2b = dx2.astype(BF16)
        dx2b_ref[...] = dx2b
        dxn2 = jnp.zeros((tm, D_MODEL), F32)
        for n in range(N_FF_CHUNKS):
            lo, hi = n * FF_CHUNK, (n + 1) * FF_CHUNK
            gate = gu_ref[:, lo:hi]
            up = gu_ref[:, D_FF + lo:D_FF + hi]
            sg = _sigmoid(gate)
            si = gate * sg
            a_ref[:, lo:hi] = (si * up).astype(BF16)
            d_a = _dot_nt(dx2b, wdn[lo:hi, :])
            dgb = ((d_a * up) * (sg * (1.0 + gate * (1.0 - sg)))).astype(BF16)
            dub = (d_a * si).astype(BF16)
            dgu_ref[:, lo:hi] = dgb
            dgu_ref[:, D_FF + lo:D_FF + hi] = dub
            dxn2 = dxn2 + (_dot_nt(dgb, wgu[:, lo:hi]) + _dot_nt(dub, wgu[:, D_FF + lo:D_FF + hi]))

        x1 = x1_ref[...]
        gffn = gffn_ref[...]
        r1 = lax.rsqrt(_mean(x1 * x1) + EPS)
        xh1 = x1 * r1
        xn2_ref[...] = (xh1 * gffn).astype(BF16)
        dgffn_ref[...] += jnp.sum(dxn2 * xh1, axis=0, keepdims=True)
        dyg = dxn2 * gffn
        dx1 = dx2 + r1 * (dyg - xh1 * _mean(dyg * xh1))
        dx1_ref[...] = dx1
        dx1b = dx1.astype(BF16)
        dx1b_ref[...] = dx1b

        ov = o_ref[...]
        sv = s_ref[...]
        go = gout_ref[:, :D_ATTN]
        gs = gout_ref[:, D_ATTN:]
        ro = lax.rsqrt(_mean(ov * ov) + EPS)
        rs = lax.rsqrt(_mean(sv * sv) + EPS)
        oh = ov * ro
        sh = sv * rs
        mrg_ref[:, :D_ATTN] = (oh * go).astype(BF16)
        mrg_ref[:, D_ATTN:] = (sh * gs).astype(BF16)
        dmo = _dot_nt(dx1b, wout[:D_ATTN, :])
        dms = _dot_nt(dx1b, wout[D_ATTN:, :])
        dgout_ref[:, :D_ATTN] += jnp.sum(dmo * oh, axis=0, keepdims=True)
        dgout_ref[:, D_ATTN:] += jnp.sum(dms * sh, axis=0, keepdims=True)
        dmog = dmo * go
        d_o = ro * (dmog - oh * _mean(dmog * oh))
        do_ref[...] = d_o.astype(BF16)
        dl_ref[...] = _dot_exact(d_o * ov, _group_indicator())
        dmsg = dms * gs
        dsgu_ref[...] = rs * (dmsg - sh * _mean(dmsg * sh))

    return pl.pallas_call(
        body, name="bwd_b", grid=(nt,),
        in_specs=[_row_spec(tm, D_MODEL), _row_spec(tm, D_MODEL), _row_spec(tm, 2 * D_FF), _row_spec(tm, D_ATTN),
                  _row_spec(tm, D_SGU), _const_spec((1, D_MODEL)), _const_spec((1, D_MODEL)), ANY, ANY, ANY],
        out_specs=[_row_spec(tm, D_MODEL), _row_spec(tm, D_MODEL), _row_spec(tm, D_FF), _row_spec(tm, 2 * D_FF),
                   _row_spec(tm, D_MODEL), _row_spec(tm, D_MODEL), _row_spec(tm, D_MODEL), _row_spec(tm, D_ATTN),
                   _row_spec(tm, LANES), _row_spec(tm, D_SGU), _const_spec((1, D_MODEL)), _const_spec((1, D_MODEL))],
        out_shape=[SDS((t, D_MODEL), F32), SDS((t, D_MODEL), BF16), SDS((t, D_FF), BF16), SDS((t, 2 * D_FF), BF16),
                   SDS((t, D_MODEL), BF16), SDS((t, D_MODEL), BF16), SDS((t, D_MODEL), BF16), SDS((t, D_ATTN), BF16),
                   SDS((t, LANES), F32), SDS((t, D_SGU), F32), SDS((1, D_MODEL), F32), SDS((1, D_MODEL), F32)],
        scratch_shapes=[pltpu.VMEM((D_MODEL, D_MODEL), BF16), pltpu.VMEM((D_MODEL, 2 * D_FF), BF16),
                        pltpu.VMEM((D_FF, D_MODEL), BF16)],
        compiler_params=_params("arbitrary"),
    )(dx2, x1, gu, o, sgu, gout, gffn, w_out, w_gu, w_dn)


def _bwd_a(dx1, x, z, fl, dsgu, dq, dk, dv, dc, gmix, wqkv, wf, wz, lng, lnb, wsm, wsm_t, bsf, mask, *, tm):
    t = x.shape[0]
    nt = t // tm
    nch = tm // SGU_CHUNK

    def body(dx1_ref, x_ref, z_ref, fl_ref, dsgu_ref, dq_ref, dk_ref, dv_ref, dc_ref, gmix_ref, wqkv_ref, wf_ref,
             wz_ref, lng_ref, lnb_ref, wsm_ref, wsmt_ref, bsf_ref, mask_ref,
             dx_ref, xn_ref, dqkv_ref, dflb_ref, dzb_ref, dgmix_ref, dbf_ref, dlng_ref, dlnb_ref, dws_ref, dbs_ref,
             carry_ref, dzvn_ref, dzu_ref, dbacc_ref):
        step = pl.program_id(0)

        @pl.when(step == 0)
        def _():
            carry_ref[...] = jnp.zeros_like(carry_ref)
            dbacc_ref[...] = jnp.zeros_like(dbacc_ref)
            for ref in (dgmix_ref, dbf_ref, dlng_ref, dlnb_ref, dws_ref):
                ref[...] = jnp.zeros_like(ref)

        z = z_ref[...]
        erf = lax.erf(z * INV_SQRT2)
        cdf = 0.5 * (1.0 + erf)
        zg = z * cdf
        zu = zg[:, :D_SGU]
        zv = zg[:, D_SGU:]
        xc = zv - _mean(zv)
        rln = lax.rsqrt(_mean(xc * xc) + EPS)
        zh = xc * rln
        lng = lng_ref[...]
        zvn = (zh * lng + lnb_ref[...]).astype(BF16)
        dsgu = dsgu_ref[...]
        lane_grp = lax.broadcasted_iota(jnp.int32, (SGU_CHUNK, D_SGU), 1) >> 6
        for ch in range(nch):
            rows = slice(ch * SGU_CHUNK, (ch + 1) * SGU_CHUNK)
            zc = zvn[rows, :]
            ds_c = dsgu[rows, :]
            mixed = _sgu_mix(wsm_ref, zc, lane_grp) + bsf_ref[...]
            dzu_ref[rows, :] = ds_c * mixed
            dmix = ds_c * zu[rows, :]
            dbacc_ref[...] += dmix
            dmb = dmix.astype(BF16)
            dzvn_ref[rows, :] = _sgu_mix(wsmt_ref, dmb, lane_grp)
            for g in range(N_GROUPS):
                dws_ref[g] += _dot_nt(jnp.where(lane_grp == g, dmb, jnp.zeros_like(dmb)), zc)
        dzvn = dzvn_ref[...]
        dlng_ref[...] += jnp.sum(dzvn * zh, axis=0, keepdims=True)
        dlnb_ref[...] += jnp.sum(dzvn, axis=0, keepdims=True)
        dzh = dzvn * lng
        dzv = rln * ((dzh - _mean(dzh)) - zh * _mean(dzh * zh))
        pdf = jnp.exp(-0.5 * (z * z)) * INV_SQRT_2PI
        dgelu = cdf + z * pdf
        dzb_ref[:, :D_SGU] = (dzu_ref[...] * dgelu[:, :D_SGU]).astype(BF16)
        dzb_ref[:, D_SGU:] = (dzv * dgelu[:, D_SGU:]).astype(BF16)

        dc = dc_ref[...]
        row = lax.broadcasted_iota(jnp.int32, (tm, tm), 0)
        col = lax.broadcasted_iota(jnp.int32, (tm, tm), 1)
        dlogf = _dot_exact((col >= row).astype(F32), dc) + carry_ref[...]
        carry_ref[...] = dlogf[0:1, :]
        dfl = dlogf * _sigmoid(-fl_ref[...])
        dbf_ref[...] += jnp.sum(dfl, axis=0, keepdims=True)
        dflb = dfl.astype(BF16)
        dflb_ref[...] = dflb

        dqkv_ref[:, :D_ATTN] = dq_ref[...]
        dqkv_ref[:, D_ATTN:2 * D_ATTN] = dk_ref[...]
        dqkv_ref[:, 2 * D_ATTN:] = dv_ref[...]
        dxn = _dot_nt(dqkv_ref[...], wqkv_ref[...]) + _dot_nt(dflb, wf_ref[...]) + _dot_nt(dzb_ref[...], wz_ref[...])

        xt = x_ref[...]
        gmix = gmix_ref[...]
        r = lax.rsqrt(_mean(xt * xt) + EPS)
        xh = xt * r
        xn_ref[...] = (xh * gmix).astype(BF16)
        dgmix_ref[...] += jnp.sum(dxn * xh, axis=0, keepdims=True)
        dyg = dxn * gmix
        dx_ref[...] = dx1_ref[...] + r * (dyg - xh * _mean(dyg * xh))

        @pl.when(step == nt - 1)
        def _():
            for g in range(N_GROUPS):
                dws_ref[g] = dws_ref[g] * mask_ref[...]
            dbs_ref[...] = _dot_exact(dbacc_ref[...], _group_indicator())

    rev = functools.partial(_rev_spec, nt=nt)
    return pl.pallas_call(
        body, name="bwd_a", grid=(nt,),
        in_specs=[rev(tm, D_MODEL), rev(tm, D_MODEL), rev(tm, 2 * D_SGU), rev(tm, LANES), rev(tm, D_SGU),
                  rev(tm, D_ATTN), rev(tm, D_ATTN), rev(tm, D_ATTN), rev(tm, LANES),
                  _const_spec((1, D_MODEL)), _const_spec((D_MODEL, 3 * D_ATTN)), _const_spec((D_MODEL, LANES)),
                  _const_spec((D_MODEL, 2 * D_SGU)), _const_spec((1, D_SGU)), _const_spec((1, D_SGU)),
                  _const_spec((N_GROUPS, SGU_CHUNK, SGU_CHUNK)), _const_spec((N_GROUPS, SGU_CHUNK, SGU_CHUNK)),
                  _const_spec((SGU_CHUNK, D_SGU)), _const_spec((SGU_CHUNK, SGU_CHUNK))],
        out_specs=[rev(tm, D_MODEL), rev(tm, D_MODEL), rev(tm, 3 * D_ATTN), rev(tm, LANES), rev(tm, 2 * D_SGU),
                   _const_spec((1, D_MODEL)), _const_spec((1, LANES)), _const_spec((1, D_SGU)), _const_spec((1, D_SGU)),
                   _const_spec((N_GROUPS, SGU_CHUNK, SGU_CHUNK)), _const_spec((SGU_CHUNK, LANES))],
        out_shape=[SDS((t, D_MODEL), F32), SDS((t, D_MODEL), BF16), SDS((t, 3 * D_ATTN), BF16), SDS((t, LANES), BF16),
                   SDS((t, 2 * D_SGU), BF16), SDS((1, D_MODEL), F32), SDS((1, LANES), F32), SDS((1, D_SGU), F32),
                   SDS((1, D_SGU), F32), SDS((N_GROUPS, SGU_CHUNK, SGU_CHUNK), F32), SDS((SGU_CHUNK, LANES), F32)],
        scratch_shapes=[pltpu.VMEM((1, LANES), F32), pltpu.VMEM((tm, D_SGU), F32), pltpu.VMEM((tm, D_SGU), F32),
                        pltpu.VMEM((SGU_CHUNK, D_SGU), F32)],
        compiler_params=_params("arbitrary"),
    )(dx1, x, z, fl, dsgu, dq, dk, dv, dc, gmix, wqkv, wf, wz, lng, lnb, wsm, wsm_t, bsf, mask)


def _pick(n, cap):
    if n <= cap:
        return n
    best = LANES
    for cand in range(LANES, cap + 1, LANES):
        if n % cand == 0:
            best = cand
    return best


def _tn_matmul(a, b, *, bt):
    t, k1 = a.shape
    n = b.shape[1]
    bk = _pick(k1, 1408)
    bn = _pick(n, 1408)
    nsteps = t // bt

    def body(a_ref, b_ref, o_ref):
        @pl.when(pl.program_id(2) == 0)
        def _():
            o_ref[...] = jnp.zeros_like(o_ref)

        o_ref[...] += _dot_tn(a_ref[...], b_ref[...])

    return pl.pallas_call(
        body, name=f"wgrad_{k1}x{n}", grid=(k1 // bk, n // bn, nsteps),
        in_specs=[pl.BlockSpec((bt, bk), lambda i, j, s: (s, i)), pl.BlockSpec((bt, bn), lambda i, j, s: (s, j))],
        out_specs=pl.BlockSpec((bk, bn), lambda i, j, s: (i, j)),
        out_shape=SDS((k1, n), F32),
        compiler_params=_params("arbitrary", "arbitrary", "arbitrary"),
    )(a, b)


def _adamw(parts, w, m, v, *, name):
    rows, cols = w.shape
    br = _pick_rows(rows, cols)
    c1 = 1.0 - ADAM_B1 ** ADAM_STEP
    c2 = 1.0 - ADAM_B2 ** ADAM_STEP

    def body(p_ref, w_ref, m_ref, v_ref, g_ref, d_ref, nm_ref, nv_ref):
        g = p_ref[0].astype(F32)
        for j in range(1, N_DEV):
            g = g + p_ref[j].astype(F32)
        g_ref[...] = g
        nm = ADAM_B1 * m_ref[...] + (1.0 - ADAM_B1) * g
        nv = ADAM_B2 * v_ref[...] + (1.0 - ADAM_B2) * (g * g)
        nm_ref[...] = nm
        nv_ref[...] = nv
        d_ref[...] = -ADAM_LR * ((nm / c1) / (jnp.sqrt(nv / c2) + ADAM_EPS) + ADAM_WD * w_ref[...])

    spec = pl.BlockSpec((br, cols), lambda i: (i, 0))
    return pl.pallas_call(
        body, name=name, grid=(rows // br,),
        in_specs=[pl.BlockSpec((N_DEV, br, cols), lambda i: (0, i, 0)), spec, spec, spec],
        out_specs=[spec] * 4, out_shape=[SDS((rows, cols), F32)] * 4,
        compiler_params=_params("arbitrary"),
    )(parts, w, m, v)


def _pick_rows(rows, cols):
    target = max(8, (256 * 1024) // cols)
    best = 8
    for cand in range(8, min(rows, target) + 1, 8):
        if rows % cand == 0:
            best = cand
    return best


def _peer(k):
    x, y, c = lax.axis_index("x"), lax.axis_index("y"), lax.axis_index("c")
    px = 1 - x if k & 4 else x
    py = 1 - y if k & 2 else y
    pc = 1 - c if k & 1 else c
    return (px, py, pc), 4 * px + 2 * py + pc


def _exchange(arrs, *, scatter, name):
    n = len(arrs)

    def body(*refs):
        ins, outs = refs[:n], refs[n:2 * n]
        send_sems, recv_sems, local_sems = refs[2 * n:]
        me = 4 * lax.axis_index("x") + 2 * lax.axis_index("y") + lax.axis_index("c")
        local = []
        for a in range(n):
            src = ins[a].at[me] if scatter else ins[a]
            cp = pltpu.make_async_copy(src, outs[a].at[me], local_sems.at[a])
            cp.start()
            local.append(cp)
        for k in range(1, N_DEV):
            peer, pidx = _peer(k)
            for a in range(n):
                pltpu.make_async_remote_copy(
                    src_ref=ins[a].at[pidx] if scatter else ins[a], dst_ref=outs[a].at[me],
                    send_sem=send_sems.at[k - 1, a], recv_sem=recv_sems.at[k - 1, a],
                    device_id=peer, device_id_type=MESH).start()
        for k in range(1, N_DEV):
            peer, pidx = _peer(k)
            for a in range(n):
                pltpu.make_async_remote_copy(
                    src_ref=ins[a].at[pidx] if scatter else ins[a], dst_ref=outs[a].at[pidx],
                    send_sem=send_sems.at[k - 1, a], recv_sem=recv_sems.at[k - 1, a],
                    device_id=peer, device_id_type=MESH).wait()
        for cp in local:
            cp.wait()

    out_shape = [SDS(a.shape if scatter else (N_DEV,) + a.shape, a.dtype) for a in arrs]
    return pl.pallas_call(
        body, name=name, in_specs=[ANY] * n, out_specs=[ANY] * n, out_shape=out_shape,
        scratch_shapes=[pltpu.SemaphoreType.DMA((N_DEV - 1, n)), pltpu.SemaphoreType.DMA((N_DEV - 1, n)),
                        pltpu.SemaphoreType.DMA((n,))],
    )(*arrs)


def _rows_of(col8, tq):
    t = col8.shape[0]
    return col8[:, :N_HEADS].T.reshape(HEAD_PAIRS, 2, t // tq, tq)


def _cols_of(rows):
    cols = rows.reshape(N_HEADS, -1).T
    return jnp.pad(cols, ((0, 0), (0, LANES - N_HEADS)))


def _pairs_to_cols(pair_arr):
    t = pair_arr.shape[1]
    cols = pair_arr[:, :, :2].transpose(1, 0, 2).reshape(t, N_HEADS)
    return jnp.pad(cols, ((0, 0), (0, LANES - N_HEADS)))


def _local_step(x, tgt, small, big):
    t = x.shape[0]
    tm, tq = _tiles(t)
    r = jnp.arange(SGU_CHUNK, dtype=jnp.int32) // SGU_BLOCK
    mask = (r[None, :] <= r[:, None]).astype(F32)

    layers = []
    saved = []
    for l in range(DEPTH):
        w_in, w_out, w_gu, w_dn = big[l]
        wqkv = w_in[:, :3 * D_ATTN]
        wf = jnp.pad(w_in[:, 3 * D_ATTN:3 * D_ATTN + N_HEADS], ((0, 0), (0, LANES - N_HEADS)))
        wz = w_in[:, 3 * D_ATTN + N_HEADS:]
        bf = jnp.pad(small["b_f"][l], (0, LANES - N_HEADS))[None, :]
        wsm = (small["w_s"][l] * mask[None]).astype(BF16)
        wsm_t = jnp.swapaxes(wsm, 1, 2)
        bsf = jnp.repeat(small["b_s"][l].T, GROUP_DIM, axis=1)
        lw = dict(wqkv=wqkv, wf=wf, wz=wz, bf=bf, wsm=wsm, wsm_t=wsm_t, bsf=bsf, w_out=w_out, w_gu=w_gu, w_dn=w_dn,
                  gmix=small["mix_norm_g"][l][None, :], lng=small["sgu_ln_g"][l][None, :],
                  lnb=small["sgu_ln_b"][l][None, :], gout=small["out_norm_g"][l][None, :],
                  gffn=small["ffn_norm_g"][l][None, :])
        layers.append(lw)
        q, k, v, fl, c, z, sgu = _fwd_a(x, lw["gmix"], wqkv, wf, bf, wz, lw["lng"], lw["lnb"], wsm, bsf, tm=tm)
        c_row = _rows_of(c, tq)
        o, lse = _attn_fwd(q, k, v, c, c_row, tq=tq)
        x1, x2, gu = _fwd_b(x, o, sgu, lw["gout"], lw["gffn"], w_out, w_gu, w_dn, tm=tm)
        saved.append(dict(x=x, q=q, k=k, v=v, fl=fl, c=c, c_row=c_row, z=z, sgu=sgu, o=o, lse=lse, x1=x1, gu=gu))
        x = x2

    dx, loss, dgfin = _loss_bwd(x, tgt, small["final_norm_g"][None, :], tm=tm)
    grads = {n: [None] * DEPTH for n in ("mix_norm_g", "w_in", "b_f", "sgu_ln_g", "sgu_ln_b", "w_s", "b_s",
                                          "out_norm_g", "w_out", "ffn_norm_g", "w_gate_up", "w_down")}
    for l in reversed(range(DEPTH)):
        lw, sv = layers[l], saved[l]
        (dx1, dx2b, a, dgu, xn2, mrg, dx1b, do, delta, dsgu, dgffn, dgout) = _bwd_b(
            dx, sv["x1"], sv["gu"], sv["o"], sv["sgu"], lw["gout"], lw["gffn"], lw["w_out"], lw["w_gu"], lw["w_dn"],
            tm=tm)
        grads["w_down"][l] = _tn_matmul(a, dx2b, bt=tq)
        grads["w_gate_up"][l] = _tn_matmul(xn2, dgu, bt=tq)
        grads["w_out"][l] = _tn_matmul(mrg, dx1b, bt=tq)
        lse_row = _rows_of(_pairs_to_cols(sv["lse"]), tq)
        dqt, dk, dv, dck, dcq = _attn_bwd(sv["q"], sv["k"], sv["v"], do, sv["c"], lse_row, _rows_of(delta, tq),
                                     sv["c_row"], tq=tq)
        dq = (dqt.transpose(1, 3, 0, 2).reshape(t, D_ATTN) * QK_SCALE).astype(BF16)
        (dx, xn, dqkv, dflb, dzb, dgmix, dbf, dlng, dlnb, dws, dbs) = _bwd_a(
            dx1, sv["x"], sv["z"], sv["fl"], dsgu, dq, dk, dv, _pairs_to_cols(dck) + _cols_of(dcq), lw["gmix"], lw["wqkv"], lw["wf"],
            lw["wz"], lw["lng"], lw["lnb"], lw["wsm"], lw["wsm_t"], lw["bsf"], mask, tm=tm)
        grads["w_in"][l] = jnp.concatenate(
            [_tn_matmul(xn, dqkv, bt=tq), _tn_matmul(xn, dflb, bt=tq)[:, :N_HEADS], _tn_matmul(xn, dzb, bt=tq)], axis=1)
        grads["mix_norm_g"][l] = dgmix[0]
        grads["b_f"][l] = dbf[0, :N_HEADS]
        grads["sgu_ln_g"][l] = dlng[0]
        grads["sgu_ln_b"][l] = dlnb[0]
        grads["w_s"][l] = dws
        grads["b_s"][l] = dbs[:, :N_GROUPS].T
        grads["out_norm_g"][l] = dgout[0]
        grads["ffn_norm_g"][l] = dgffn[0]
    grads = {n: jnp.stack(g) for n, g in grads.items()}
    grads["final_norm_g"] = dgfin[0]
    return loss[0, 0], dx, grads


SMALL = ("mix_norm_g", "b_f", "sgu_ln_g", "sgu_ln_b", "w_s", "b_s", "out_norm_g", "ffn_norm_g", "final_norm_g")
BIG = ("w_in", "w_out", "w_gate_up", "w_down")
WEIGHTS = ("mix_norm_g", "w_in", "b_f", "sgu_ln_g", "sgu_ln_b", "w_s", "b_s", "out_norm_g", "w_out", "ffn_norm_g",
           "w_gate_up", "w_down", "final_norm_g")
SHARD_AXIS = {"w_in": 1, "w_out": 0, "w_gate_up": 1, "w_down": 0}


def _assemble(gathered, name):
    if SHARD_AXIS[name] == 0:
        full = gathered.transpose(1, 0, 2, 3)
        return full.reshape(DEPTH, -1, full.shape[-1])
    full = gathered.transpose(1, 2, 0, 3)
    return full.reshape(DEPTH, full.shape[1], -1)


def _split(full, name):
    d, rows, cols = full.shape
    if SHARD_AXIS[name] == 0:
        return full.reshape(d, N_DEV, rows // N_DEV, cols).transpose(1, 0, 2, 3)
    return full.reshape(d, rows, N_DEV, cols // N_DEV).transpose(2, 0, 1, 3)


def _pack(tree):
    flat = jnp.concatenate([tree[n].reshape(-1) for n in SMALL])
    pad = (-flat.shape[0]) % (8 * LANES)
    return jnp.pad(flat, (0, pad)).reshape(-1, LANES)


def _unpack(packed, like):
    flat = packed.reshape(-1)
    out, at = {}, 0
    for n in SMALL:
        size = like[n].size
        out[n] = flat[at:at + size].reshape(like[n].shape)
        at += size
    return out


def kernel(x, mix_norm_g, w_in, b_f, sgu_ln_g, sgu_ln_b, w_s, b_s, out_norm_g, w_out, ffn_norm_g, w_gate_up, w_down, final_norm_g, loss_target, m_mix_norm_g, m_w_in, m_b_f, m_sgu_ln_g, m_sgu_ln_b, m_w_s, m_b_s, m_out_norm_g, m_w_out, m_ffn_norm_g, m_w_gate_up, m_w_down, m_final_norm_g, v_mix_norm_g, v_w_in, v_b_f, v_sgu_ln_g, v_sgu_ln_b, v_w_s, v_b_s, v_out_norm_g, v_w_out, v_ffn_norm_g, v_w_gate_up, v_w_down, v_final_norm_g):
    w = dict(mix_norm_g=mix_norm_g, w_in=w_in, b_f=b_f, sgu_ln_g=sgu_ln_g, sgu_ln_b=sgu_ln_b, w_s=w_s, b_s=b_s,
             out_norm_g=out_norm_g, w_out=w_out, ffn_norm_g=ffn_norm_g, w_gate_up=w_gate_up, w_down=w_down,
             final_norm_g=final_norm_g)
    m = dict(mix_norm_g=m_mix_norm_g, w_in=m_w_in, b_f=m_b_f, sgu_ln_g=m_sgu_ln_g, sgu_ln_b=m_sgu_ln_b, w_s=m_w_s,
             b_s=m_b_s, out_norm_g=m_out_norm_g, w_out=m_w_out, ffn_norm_g=m_ffn_norm_g, w_gate_up=m_w_gate_up,
             w_down=m_w_down, final_norm_g=m_final_norm_g)
    v = dict(mix_norm_g=v_mix_norm_g, w_in=v_w_in, b_f=v_b_f, sgu_ln_g=v_sgu_ln_g, sgu_ln_b=v_sgu_ln_b, w_s=v_w_s,
             b_s=v_b_s, out_norm_g=v_out_norm_g, w_out=v_w_out, ffn_norm_g=v_ffn_norm_g, w_gate_up=v_w_gate_up,
             w_down=v_w_down, final_norm_g=v_final_norm_g)

    gathered = _exchange([w[n].astype(BF16) for n in BIG], scatter=False, name="gather_weights")
    full = {n: _assemble(g, n) for n, g in zip(BIG, gathered)}
    big = [tuple(full[n][l] for n in BIG) for l in range(DEPTH)]
    small = {n: w[n] for n in SMALL}

    loss, dx, grads = _local_step(x[0], loss_target[0], small, big)
    loss = lax.psum(loss, ("x", "y", "c"))

    parts = _exchange([_split(grads[n], n).astype(BF16) for n in BIG], scatter=True, name="scatter_grads")
    small_parts = _exchange([_pack(grads)], scatter=False, name="gather_small_grads")[0]

    g_out, d_out, m_out, v_out = {}, {}, {}, {}
    for n, part in zip(BIG, parts):
        shape = w[n].shape
        two_d = lambda a: a.reshape(-1, shape[-1])
        res = _adamw(part.reshape(N_DEV, -1, shape[-1]), two_d(w[n]), two_d(m[n]), two_d(v[n]), name=f"adamw_{n}")
        g_out[n], d_out[n], m_out[n], v_out[n] = (r.reshape(shape) for r in res)
    res = _adamw(small_parts, _pack(w), _pack(m), _pack(v), name="adamw_small")
    for dst, packed in zip((g_out, d_out, m_out, v_out), res):
        dst.update(_unpack(packed, w))

    return (loss, dx[None], *[g_out[n] for n in WEIGHTS], *[d_out[n] for n in WEIGHTS],
            *[m_out[n] for n in WEIGHTS], *[v_out[n] for n in WEIGHTS])
```

```python
import functools
import math

import jax
import jax.numpy as jnp
from jax import lax
from jax.experimental import pallas as pl
from jax.experimental.pallas import tpu as pltpu

F32, BF16 = jnp.float32, jnp.bfloat16
HIGHEST = lax.Precision.HIGHEST
MESH = pl.DeviceIdType.MESH
ANY = pl.BlockSpec(memory_space=pl.ANY)
SDS = jax.ShapeDtypeStruct

N_DEV = 8
DEPTH = 4
D_MODEL = 1024
D_ATTN = 512
D_SGU = 512
N_HEADS = 8
HEAD_DIM = 64
HEAD_PAIRS = N_HEADS // 2
SGU_CHUNK = 128
SGU_BLOCK = 64
N_GROUPS = 8
GROUP_DIM = 64
D_FF = 2816
FF_CHUNK = 1408
N_FF_CHUNKS = D_FF // FF_CHUNK
D_IN = 3 * D_ATTN + N_HEADS + 2 * D_SGU
LANES = 128
EPS = 1e-6
QK_SCALE = HEAD_DIM ** -0.5
INV_SQRT2 = 1.0 / math.sqrt(2.0)
INV_SQRT_2PI = 1.0 / math.sqrt(2.0 * math.pi)
LOG2E = 1.0 / math.log(2.0)
LN2 = math.log(2.0)
ROW_CHUNK = 32

ADAM_LR = 0.001
ADAM_B1 = 0.9
ADAM_B2 = 0.999
ADAM_EPS = 1e-08
ADAM_WD = 0.01
ADAM_STEP = 10

VMEM_LIMIT_BYTES = 56 * 1024 * 1024


def _params(*sem):
    return pltpu.CompilerParams(dimension_semantics=sem or None, vmem_limit_bytes=VMEM_LIMIT_BYTES)


def _dot(a, b):
    return jnp.dot(a, b, preferred_element_type=F32)


def _dot_nt(a, b):
    return lax.dot_general(a, b, (((1,), (1,)), ((), ())), preferred_element_type=F32)


def _dot_tn(a, b):
    return lax.dot_general(a, b, (((0,), (0,)), ((), ())), preferred_element_type=F32)


def _dot_exact(a, b):
    return jnp.dot(a, b, precision=HIGHEST, preferred_element_type=F32)


def _mean(v):
    return jnp.mean(v, axis=-1, keepdims=True)


def _sigmoid(v):
    return 1.0 / (1.0 + jnp.exp(-v))


def _row_spec(tm, n):
    return pl.BlockSpec((tm, n), lambda i: (i, 0))


def _rev_spec(tm, n, nt):
    return pl.BlockSpec((tm, n), lambda i: (nt - 1 - i, 0))


def _const_spec(shape):
    return pl.BlockSpec(shape, lambda i: (0,) * len(shape))


def _tiles(t):
    return min(256, t), min(512, t)


def _group_indicator():
    r = lax.broadcasted_iota(jnp.int32, (D_ATTN, LANES), 0)
    c = lax.broadcasted_iota(jnp.int32, (D_ATTN, LANES), 1)
    return ((r >> 6) == c).astype(F32)


def _sgu_mix(w_ref, zc, lane_grp):
    out = jnp.zeros((SGU_CHUNK, D_SGU), F32)
    for g in range(N_GROUPS):
        out = out + jnp.where(lane_grp == g, _dot(w_ref[g], zc), 0.0)
    return out


def _fwd_a(x, gmix, wqkv, wf, bf, wz, lng, lnb, wsm, bsf, *, tm):
    t = x.shape[0]
    nt = t // tm
    nch = tm // SGU_CHUNK

    def body(x_ref, gmix_ref, wqkv_ref, wf_ref, bf_ref, wz_ref, lng_ref, lnb_ref, wsm_ref, bsf_ref,
             q_ref, k_ref, v_ref, fl_ref, z_ref, sgu_ref, carry_ref):
        @pl.when(pl.program_id(0) == 0)
        def _():
            carry_ref[...] = jnp.zeros_like(carry_ref)

        xt = x_ref[...]
        r = lax.rsqrt(_mean(xt * xt) + EPS)
        xn = ((xt * r) * gmix_ref[...]).astype(BF16)
        qkv = _dot(xn, wqkv_ref[...])

        fl = _dot(xn, wf_ref[...]) + bf_ref[...]
        fl_ref[...] = fl
        logf = jnp.minimum(fl, 0.0) - jnp.log1p(jnp.exp(-jnp.abs(fl)))
        row = lax.broadcasted_iota(jnp.int32, (tm, tm), 0)
        col = lax.broadcasted_iota(jnp.int32, (tm, tm), 1)
        c = _dot_exact((col <= row).astype(F32), logf) + carry_ref[...]
        carry_ref[...] = c[tm - 1:tm, :]

        c2 = c * LOG2E
        lane = lax.broadcasted_iota(jnp.int32, (tm, LANES), 1)
        for h in range(N_HEADS):
            pair, hh = divmod(h, 2)
            base = _aug_lane(hh)
            in_head = (lane >= hh * HEAD_DIM) & (lane < (hh + 1) * HEAD_DIM)
            col_h = jnp.sum(jnp.where(lane == h, c2, 0.0), axis=1, keepdims=True)
            hi = col_h.astype(BF16).astype(F32)
            mid = (col_h - hi).astype(BF16).astype(F32)
            lo = (col_h - hi) - mid
            split = jnp.where(lane == base, hi, jnp.where(lane == base + 1, mid, jnp.where(lane == base + 2, lo, 0.0)))
            split_k = jnp.where(lane == base + 3, hi, jnp.where(lane == base + 4, mid,
                                                                 jnp.where(lane == base + 5, lo, 0.0)))
            ones_q = ((lane >= base + 3) & (lane < base + 6)).astype(F32)
            ones_k = ((lane >= base) & (lane < base + 3)).astype(F32)
            blk = slice(pair * LANES, (pair + 1) * LANES)
            q_ref[h] = jnp.where(in_head, qkv[:, blk] * (QK_SCALE * LOG2E), split + ones_q).astype(BF16)
            k_ref[h] = jnp.where(in_head, qkv[:, D_ATTN:2 * D_ATTN][:, blk], ones_k - split_k).astype(BF16)
            v_ref[h] = jnp.where(in_head, qkv[:, 2 * D_ATTN:][:, blk], (lane == base).astype(F32)).astype(BF16)

        z = _dot(xn, wz_ref[...])
        z_ref[...] = z
        zg = 0.5 * z * (1.0 + lax.erf(z * INV_SQRT2))
        zu = zg[:, :D_SGU]
        zv = zg[:, D_SGU:]
        xc = zv - _mean(zv)
        zvn = ((xc * lax.rsqrt(_mean(xc * xc) + EPS)) * lng_ref[...] + lnb_ref[...]).astype(BF16)
        lane_grp = lax.broadcasted_iota(jnp.int32, (SGU_CHUNK, D_SGU), 1) >> 6
        for ch in range(nch):
            rows = slice(ch * SGU_CHUNK, (ch + 1) * SGU_CHUNK)
            mixed = _sgu_mix(wsm_ref, zvn[rows, :], lane_grp) + bsf_ref[...]
            sgu_ref[rows, :] = zu[rows, :] * mixed

    head_spec = pl.BlockSpec((N_HEADS, tm, LANES), lambda i: (0, i, 0))
    return pl.pallas_call(
        body, name="fwd_a", grid=(nt,),
        in_specs=[_row_spec(tm, D_MODEL), _const_spec((1, D_MODEL)), _const_spec((D_MODEL, 3 * D_ATTN)),
                  _const_spec((D_MODEL, LANES)), _const_spec((1, LANES)), _const_spec((D_MODEL, 2 * D_SGU)),
                  _const_spec((1, D_SGU)), _const_spec((1, D_SGU)), _const_spec((N_GROUPS, SGU_CHUNK, SGU_CHUNK)),
                  _const_spec((SGU_CHUNK, D_SGU))],
        out_specs=[head_spec, head_spec, head_spec, _row_spec(tm, LANES), _row_spec(tm, 2 * D_SGU),
                   _row_spec(tm, D_SGU)],
        out_shape=[SDS((N_HEADS, t, LANES), BF16)] * 3 + [SDS((t, LANES), F32), SDS((t, 2 * D_SGU), F32),
                                                          SDS((t, D_SGU), F32)],
        scratch_shapes=[pltpu.VMEM((1, LANES), F32)],
        compiler_params=_params("arbitrary"),
    )(x, gmix, wqkv, wf, bf, wz, lng, lnb, wsm, bsf)


def _aug_lane(hh):
    return (1 - hh) * HEAD_DIM


def _attn_fwd(qa, ka, va, *, tq):
    t = qa.shape[1]
    nq = t // tq
    nrc = tq // ROW_CHUNK

    def body(q_ref, k_hbm, v_hbm, o_ref, lse_ref, k_vm, v_vm, s_ref, p_ref, m_ref, a_ref, acc_ref):
        p = pl.program_id(0)
        i = pl.program_id(1)

        @pl.when(i == 0)
        def _():
            pltpu.sync_copy(k_hbm.at[pl.ds(2 * p, 2)], k_vm)
            pltpu.sync_copy(v_hbm.at[pl.ds(2 * p, 2)], v_vm)

        m_ref[...] = jnp.full(m_ref.shape, -jnp.inf, F32)
        acc_ref[...] = jnp.zeros_like(acc_ref)
        rowq = lax.broadcasted_iota(jnp.int32, (ROW_CHUNK, tq), 0)
        colk = lax.broadcasted_iota(jnp.int32, (ROW_CHUNK, tq), 1)

        def step(j, masked):
            start = pl.multiple_of(j * tq, tq)
            for h in range(2):
                s_ref[h] = _dot_nt(q_ref[h], k_vm[h, pl.ds(start, tq), :])
            for h in range(2):
                for r in range(nrc):
                    rows = slice(r * ROW_CHUNK, (r + 1) * ROW_CHUNK)
                    sc = s_ref[h, rows, :]
                    if masked:
                        sc = jnp.where(colk <= rowq + r * ROW_CHUNK, sc, -jnp.inf)
                    m_old = m_ref[h, rows, :]
                    m_new = jnp.maximum(m_old, jnp.max(sc, axis=1, keepdims=True))
                    p_ref[h, rows, :] = jnp.exp2(sc - m_new).astype(BF16)
                    a_ref[h, rows, :] = jnp.exp2(m_old - m_new)
                    m_ref[h, rows, :] = m_new
            for h in range(2):
                acc_ref[h] = acc_ref[h] * a_ref[h] + _dot(p_ref[h], v_vm[h, pl.ds(start, tq), :])

        def loop_body(j, carry):
            step(j, False)
            return carry

        lax.fori_loop(0, i, loop_body, 0)
        step(i, True)

        lane = lax.broadcasted_iota(jnp.int32, (tq, LANES), 1)
        l_h = [jnp.sum(jnp.where(lane == _aug_lane(h), acc_ref[h], 0.0), axis=1, keepdims=True) for h in range(2)]
        o_ref[...] = jnp.where(lane < HEAD_DIM, acc_ref[0] / l_h[0], acc_ref[1] / l_h[1])
        lse = [m_ref[h] + jnp.log2(l_h[h]) for h in range(2)]
        lse_ref[0] = jnp.where(lane == 0, lse[0], jnp.where(lane == 1, lse[1], 0.0))

    return pl.pallas_call(
        body, name="attn_fwd", grid=(HEAD_PAIRS, nq),
        in_specs=[pl.BlockSpec((2, tq, LANES), lambda p, i: (p, i, 0)), ANY, ANY],
        out_specs=[pl.BlockSpec((tq, LANES), lambda p, i: (i, p)),
                   pl.BlockSpec((1, tq, LANES), lambda p, i: (p, i, 0))],
        out_shape=[SDS((t, D_ATTN), F32), SDS((HEAD_PAIRS, t, LANES), F32)],
        scratch_shapes=[pltpu.VMEM((2, t, LANES), BF16), pltpu.VMEM((2, t, LANES), BF16),
                        pltpu.VMEM((2, tq, tq), F32), pltpu.VMEM((2, tq, tq), BF16),
                        pltpu.VMEM((2, tq, 1), F32), pltpu.VMEM((2, tq, 1), F32), pltpu.VMEM((2, tq, LANES), F32)],
        compiler_params=_params("arbitrary", "arbitrary"),
    )(qa, ka, va)


def _attn_bwd(qa, ka, va, do, lse_row, delta_row, *, tq):
    t = qa.shape[1]
    nq = t // tq
    nrc = tq // ROW_CHUNK

    def body(q_hbm, do_ref, k_ref, v_ref, lse_ref, dl_ref, dqt_ref, dk_ref, dv_ref, dck_ref, dcq_ref,
             q_vm, st_ref, dp_ref, pt_ref, ds_ref, dka_ref, dva_ref):
        p = pl.program_id(0)
        j = pl.program_id(1)

        @pl.when(j == 0)
        def _():
            pltpu.sync_copy(q_hbm.at[pl.ds(2 * p, 2)], q_vm)
            dqt_ref[...] = jnp.zeros_like(dqt_ref)
            dcq_ref[...] = jnp.zeros_like(dcq_ref)

        dka_ref[...] = jnp.zeros_like(dka_ref)
        dva_ref[...] = jnp.zeros_like(dva_ref)
        lane = lax.broadcasted_iota(jnp.int32, (tq, LANES), 1)
        in_head = (lane < HEAD_DIM, lane >= HEAD_DIM)
        k_aug = [k_ref[h] for h in range(2)]
        zero = jnp.zeros_like(k_aug[0])
        v_head = [jnp.where(in_head[h], v_ref[h], zero) for h in range(2)]
        kt_ext = [jnp.concatenate([jnp.where(in_head[h], k_aug[h], zero).astype(F32).T.astype(BF16),
                                   jnp.ones((8, tq), BF16)], axis=0) for h in range(2)]
        rowk = lax.broadcasted_iota(jnp.int32, (ROW_CHUNK, tq), 0)
        colq = lax.broadcasted_iota(jnp.int32, (ROW_CHUNK, tq), 1)

        def step(i, masked):
            start = pl.multiple_of(i * tq, tq)
            do2 = do_ref[pl.ds(start, tq), :]
            q_h = [q_vm[h, pl.ds(start, tq), :] for h in range(2)]
            for h in range(2):
                st_ref[h] = _dot_nt(k_aug[h], q_h[h])
                dp_ref[h] = _dot_nt(v_head[h], do2)
            for h in range(2):
                lse = lse_ref[0, h, pl.ds(i, 1), :]
                delta = dl_ref[0, h, pl.ds(i, 1), :]
                for r in range(nrc):
                    rows = slice(r * ROW_CHUNK, (r + 1) * ROW_CHUNK)
                    st = st_ref[h, rows, :]
                    if masked:
                        st = jnp.where(rowk + r * ROW_CHUNK <= colq, st, -jnp.inf)
                    pt = jnp.exp2(st - lse)
                    pt_ref[h, rows, :] = pt.astype(BF16)
                    ds_ref[h, rows, :] = (pt * (dp_ref[h, rows, :] - delta)).astype(BF16)
            dq_t = jnp.zeros((LANES, tq), F32)
            for h in range(2):
                dva_ref[h] += _dot(pt_ref[h], do2)
                dka_ref[h] += _dot(ds_ref[h], q_h[h])
                ext = _dot(kt_ext[h], ds_ref[h])
                dq_t = dq_t + ext[:LANES, :]
                dcq_ref[0, h, pl.ds(i, 1), :] += ext[LANES:LANES + 1, :]
            dqt_ref[0, i] += dq_t

        def loop_body(i, carry):
            step(i, False)
            return carry

        step(j, True)
        lax.fori_loop(j + 1, nq, loop_body, 0)

        dk_ref[...] = (jnp.where(in_head[0], dka_ref[0], dka_ref[1]) * LN2).astype(BF16)
        dv_ref[...] = jnp.where(in_head[0], dva_ref[0], dva_ref[1]).astype(BF16)
        dck = [jnp.sum(jnp.where(lane == _aug_lane(h) + 3, dka_ref[h], 0.0), axis=1, keepdims=True) for h in range(2)]
        dck_ref[0] = jnp.where(lane == 0, -dck[0], jnp.where(lane == 1, -dck[1], 0.0))

    rows = pl.BlockSpec((1, 2, nq, tq), lambda p, j: (p, 0, 0, 0))
    return pl.pallas_call(
        body, name="attn_bwd", grid=(HEAD_PAIRS, nq),
        in_specs=[ANY, pl.BlockSpec((t, LANES), lambda p, j: (0, p)),
                  pl.BlockSpec((2, tq, LANES), lambda p, j: (p, j, 0)),
                  pl.BlockSpec((2, tq, LANES), lambda p, j: (p, j, 0)), rows, rows],
        out_specs=[pl.BlockSpec((1, nq, LANES, tq), lambda p, j: (p, 0, 0, 0)),
                   pl.BlockSpec((tq, LANES), lambda p, j: (j, p)),
                   pl.BlockSpec((tq, LANES), lambda p, j: (j, p)),
                   pl.BlockSpec((1, tq, LANES), lambda p, j: (p, j, 0)), rows],
        out_shape=[SDS((HEAD_PAIRS, nq, LANES, tq), F32), SDS((t, D_ATTN), BF16), SDS((t, D_ATTN), BF16),
                   SDS((HEAD_PAIRS, t, LANES), F32), SDS((HEAD_PAIRS, 2, nq, tq), F32)],
        scratch_shapes=[pltpu.VMEM((2, t, LANES), BF16), pltpu.VMEM((2, tq, tq), F32), pltpu.VMEM((2, tq, tq), F32),
                        pltpu.VMEM((2, tq, tq), BF16), pltpu.VMEM((2, tq, tq), BF16),
                        pltpu.VMEM((2, tq, LANES), F32), pltpu.VMEM((2, tq, LANES), F32)],
        compiler_params=_params("arbitrary", "arbitrary"),
    )(qa, do, ka, va, lse_row, delta_row)

def _load_weights_once(pairs):
    @pl.when(pl.program_id(0) == 0)
    def _():
        for src, dst in pairs:
            pltpu.sync_copy(src, dst)


def _fwd_b(x, o, sgu, gout, gffn, w_out, w_gu, w_dn, *, tm):
    t = x.shape[0]
    nt = t // tm

    def body(x_ref, o_ref, s_ref, gout_ref, gffn_ref, wout_hbm, wgu_hbm, wdn_hbm,
             x1_ref, x2_ref, gu_ref, wout, wgu, wdn):
        _load_weights_once(((wout_hbm, wout), (wgu_hbm, wgu), (wdn_hbm, wdn)))
        ov = o_ref[...]
        sv = s_ref[...]
        mo = ((ov * lax.rsqrt(_mean(ov * ov) + EPS)) * gout_ref[:, :D_ATTN]).astype(BF16)
        ms = ((sv * lax.rsqrt(_mean(sv * sv) + EPS)) * gout_ref[:, D_ATTN:]).astype(BF16)
        x1 = x_ref[...] + (_dot(mo, wout[:D_ATTN, :]) + _dot(ms, wout[D_ATTN:, :]))
        x1_ref[...] = x1
        xn2 = ((x1 * lax.rsqrt(_mean(x1 * x1) + EPS)) * gffn_ref[...]).astype(BF16)
        y = jnp.zeros((tm, D_MODEL), F32)
        for n in range(N_FF_CHUNKS):
            lo, hi = n * FF_CHUNK, (n + 1) * FF_CHUNK
            gate = _dot(xn2, wgu[:, lo:hi])
            up = _dot(xn2, wgu[:, D_FF + lo:D_FF + hi])
            gu_ref[:, lo:hi] = gate
            gu_ref[:, D_FF + lo:D_FF + hi] = up
            a = ((gate * _sigmoid(gate)) * up).astype(BF16)
            y = y + _dot(a, wdn[lo:hi, :])
        x2_ref[...] = x1 + y

    return pl.pallas_call(
        body, name="fwd_b", grid=(nt,),
        in_specs=[_row_spec(tm, D_MODEL), _row_spec(tm, D_ATTN), _row_spec(tm, D_SGU),
                  _const_spec((1, D_MODEL)), _const_spec((1, D_MODEL)), ANY, ANY, ANY],
        out_specs=[_row_spec(tm, D_MODEL), _row_spec(tm, D_MODEL), _row_spec(tm, 2 * D_FF)],
        out_shape=[SDS((t, D_MODEL), F32), SDS((t, D_MODEL), F32), SDS((t, 2 * D_FF), F32)],
        scratch_shapes=[pltpu.VMEM((D_MODEL, D_MODEL), BF16), pltpu.VMEM((D_MODEL, 2 * D_FF), BF16),
                        pltpu.VMEM((D_FF, D_MODEL), BF16)],
        compiler_params=_params("arbitrary"),
    )(x, o, sgu, gout, gffn, w_out, w_gu, w_dn)


def _loss_bwd(x, tgt, gfin, *, tm):
    t = x.shape[0]
    nt = t // tm

    def body(x_ref, t_ref, g_ref, dx_ref, loss_ref, dg_ref):
        @pl.when(pl.program_id(0) == 0)
        def _():
            loss_ref[...] = jnp.zeros_like(loss_ref)
            dg_ref[...] = jnp.zeros_like(dg_ref)

        xt = x_ref[...]
        g = g_ref[...]
        r = lax.rsqrt(_mean(xt * xt) + EPS)
        xh = xt * r
        err = xh * g - t_ref[...]
        loss_ref[...] += 0.5 * jnp.sum(_mean(err * err), axis=0, keepdims=True)
        dy = err * (1.0 / D_MODEL)
        dg_ref[...] += jnp.sum(dy * xh, axis=0, keepdims=True)
        dyg = dy * g
        dx_ref[...] = r * (dyg - xh * _mean(dyg * xh))

    return pl.pallas_call(
        body, name="loss_bwd", grid=(nt,),
        in_specs=[_row_spec(tm, D_MODEL), _row_spec(tm, D_MODEL), _const_spec((1, D_MODEL))],
        out_specs=[_row_spec(tm, D_MODEL), _const_spec((1, 1)), _const_spec((1, D_MODEL))],
        out_shape=[SDS((t, D_MODEL), F32), SDS((1, 1), F32), SDS((1, D_MODEL), F32)],
        compiler_params=_params("arbitrary"),
    )(x, tgt, gfin)


def _bwd_b(dx2, x1, gu, o, sgu, gout, gffn, w_out, w_gu, w_dn, *, tm):
    t = dx2.shape[0]
    nt = t // tm

    def body(dx2_ref, x1_ref, gu_ref, o_ref, s_ref, gout_ref, gffn_ref, wout_hbm, wgu_hbm, wdn_hbm,
             dx1_ref, dx2b_ref, a_ref, dgu_ref, xn2_ref, mrg_ref, dx1b_ref, do_ref, dl_ref, dsgu_ref,
             dgffn_ref, dgout_ref, wout, wgu, wdn):
        _load_weights_once(((wout_hbm, wout), (wgu_hbm, wgu), (wdn_hbm, wdn)))

        @pl.when(pl.program_id(0) == 0)
        def _():
            dgffn_ref[...] = jnp.zeros_like(dgffn_ref)
            dgout_ref[...] = jnp.zeros_like(dgout_ref)

        dx2 = dx2_ref[...]
        dx2b = dx2.astype(BF16)
        dx2b_ref[...] = dx2b
        dxn2 = jnp.zeros((tm, D_MODEL), F32)
        for n in range(N_FF_CHUNKS):
            lo, hi = n * FF_CHUNK, (n + 1) * FF_CHUNK
            gate = gu_ref[:, lo:hi]
            up = gu_ref[:, D_FF + lo:D_FF + hi]
            sg = _sigmoid(gate)
            si = gate * sg
            a_ref[:, lo:hi] = (si * up).astype(BF16)
            d_a = _dot_nt(dx2b, wdn[lo:hi, :])
            dgb = ((d_a * up) * (sg * (1.0 + gate * (1.0 - sg)))).astype(BF16)
            dub = (d_a * si).astype(BF16)
            dgu_ref[:, lo:hi] = dgb
            dgu_ref[:, D_FF + lo:D_FF + hi] = dub
            dxn2 = dxn2 + (_dot_nt(dgb, wgu[:, lo:hi]) + _dot_nt(dub, wgu[:, D_FF + lo:D_FF + hi]))

        x1 = x1_ref[...]
        gffn = gffn_ref[...]
        r1 = lax.rsqrt(_mean(x1 * x1) + EPS)
        xh1 = x1 * r1
        xn2_ref[...] = (xh1 * gffn).astype(BF16)
        dgffn_ref[...] += jnp.sum(dxn2 * xh1, axis=0, keepdims=True)
        dyg = dxn2 * gffn
        dx1 = dx2 + r1 * (dyg - xh1 * _mean(dyg * xh1))
        dx1_ref[...] = dx1
        dx1b = dx1.astype(BF16)
        dx1b_ref[...] = dx1b

        ov = o_ref[...]
        sv = s_ref[...]
        go = gout_ref[:, :D_ATTN]
        gs = gout_ref[:, D_ATTN:]
        ro = lax.rsqrt(_mean(ov * ov) + EPS)
        rs = lax.rsqrt(_mean(sv * sv) + EPS)
        oh = ov * ro
        sh = sv * rs
        mrg_ref[:, :D_ATTN] = (oh * go).astype(BF16)
        mrg_ref[:, D_ATTN:] = (sh * gs).astype(BF16)
        dmo = _dot_nt(dx1b, wout[:D_ATTN, :])
        dms = _dot_nt(dx1b, wout[D_ATTN:, :])
        dgout_ref[:, :D_ATTN] += jnp.sum(dmo * oh, axis=0, keepdims=True)
        dgout_ref[:, D_ATTN:] += jnp.sum(dms * sh, axis=0, keepdims=True)
        dmog = dmo * go
        d_o = ro * (dmog - oh * _mean(dmog * oh))
        do_ref[...] = d_o.astype(BF16)
        dl_ref[...] = _dot_exact(d_o * ov, _group_indicator())
        dmsg = dms * gs
        dsgu_ref[...] = rs * (dmsg - sh * _mean(dmsg * sh))

    return pl.pallas_call(
        body, name="bwd_b", grid=(nt,),
        in_specs=[_row_spec(tm, D_MODEL), _row_spec(tm, D_MODEL), _row_spec(tm, 2 * D_FF), _row_spec(tm, D_ATTN),
                  _row_spec(tm, D_SGU), _const_spec((1, D_MODEL)), _const_spec((1, D_MODEL)), ANY, ANY, ANY],
        out_specs=[_row_spec(tm, D_MODEL), _row_spec(tm, D_MODEL), _row_spec(tm, D_FF), _row_spec(tm, 2 * D_FF),
                   _row_spec(tm, D_MODEL), _row_spec(tm, D_MODEL), _row_spec(tm, D_MODEL), _row_spec(tm, D_ATTN),
                   _row_spec(tm, LANES), _row_spec(tm, D_SGU), _const_spec((1, D_MODEL)), _const_spec((1, D_MODEL))],
        out_shape=[SDS((t, D_MODEL), F32), SDS((t, D_MODEL), BF16), SDS((t, D_FF), BF16), SDS((t, 2 * D_FF), BF16),
                   SDS((t, D_MODEL), BF16), SDS((t, D_MODEL), BF16), SDS((t, D_MODEL), BF16), SDS((t, D_ATTN), BF16),
                   SDS((t, LANES), F32), SDS((t, D_SGU), F32), SDS((1, D_MODEL), F32), SDS((1, D_MODEL), F32)],
        scratch_shapes=[pltpu.VMEM((D_MODEL, D_MODEL), BF16), pltpu.VMEM((D_MODEL, 2 * D_FF), BF16),
                        pltpu.VMEM((D_FF, D_MODEL), BF16)],
        compiler_params=_params("arbitrary"),
    )(dx2, x1, gu, o, sgu, gout, gffn, w_out, w_gu, w_dn)


def _bwd_a(dx1, x, z, fl, dsgu, dq, dk, dv, dc, gmix, wqkv, wf, wz, lng, lnb, wsm, wsm_t, bsf, mask, *, tm):
    t = x.shape[0]
    nt = t // tm
    nch = tm // SGU_CHUNK

    def body(dx1_ref, x_ref, z_ref, fl_ref, dsgu_ref, dq_ref, dk_ref, dv_ref, dc_ref, gmix_ref, wqkv_ref, wf_ref,
             wz_ref, lng_ref, lnb_ref, wsm_ref, wsmt_ref, bsf_ref, mask_ref,
             dx_ref, xn_ref, dqkv_ref, dflb_ref, dzb_ref, dgmix_ref, dbf_ref, dlng_ref, dlnb_ref, dws_ref, dbs_ref,
             carry_ref, dzvn_ref, dzu_ref, dbacc_ref):
        step = pl.program_id(0)

        @pl.when(step == 0)
        def _():
            carry_ref[...] = jnp.zeros_like(carry_ref)
            dbacc_ref[...] = jnp.zeros_like(dbacc_ref)
            for ref in (dgmix_ref, dbf_ref, dlng_ref, dlnb_ref, dws_ref):
                ref[...] = jnp.zeros_like(ref)

        z = z_ref[...]
        erf = lax.erf(z * INV_SQRT2)
        cdf = 0.5 * (1.0 + erf)
        zg = z * cdf
        zu = zg[:, :D_SGU]
        zv = zg[:, D_SGU:]
        xc = zv - _mean(zv)
        rln = lax.rsqrt(_mean(xc * xc) + EPS)
        zh = xc * rln
        lng = lng_ref[...]
        zvn = (zh * lng + lnb_ref[...]).astype(BF16)
        dsgu = dsgu_ref[...]
        lane_grp = lax.broadcasted_iota(jnp.int32, (SGU_CHUNK, D_SGU), 1) >> 6
        for ch in range(nch):
            rows = slice(ch * SGU_CHUNK, (ch + 1) * SGU_CHUNK)
            zc = zvn[rows, :]
            ds_c = dsgu[rows, :]
            mixed = _sgu_mix(wsm_ref, zc, lane_grp) + bsf_ref[...]
            dzu_ref[rows, :] = ds_c * mixed
            dmix = ds_c * zu[rows, :]
            dbacc_ref[...] += dmix
            dmb = dmix.astype(BF16)
            dzvn_ref[rows, :] = _sgu_mix(wsmt_ref, dmb, lane_grp)
            for g in range(N_GROUPS):
                dws_ref[g] += _dot_nt(jnp.where(lane_grp == g, dmb, jnp.zeros_like(dmb)), zc)
        dzvn = dzvn_ref[...]
        dlng_ref[...] += jnp.sum(dzvn * zh, axis=0, keepdims=True)
        dlnb_ref[...] += jnp.sum(dzvn, axis=0, keepdims=True)
        dzh = dzvn * lng
        dzv = rln * ((dzh - _mean(dzh)) - zh * _mean(dzh * zh))
        pdf = jnp.exp(-0.5 * (z * z)) * INV_SQRT_2PI
        dgelu = cdf + z * pdf
        dzb_ref[:, :D_SGU] = (dzu_ref[...] * dgelu[:, :D_SGU]).astype(BF16)
        dzb_ref[:, D_SGU:] = (dzv * dgelu[:, D_SGU:]).astype(BF16)

        dc = dc_ref[...]
        row = lax.broadcasted_iota(jnp.int32, (tm, tm), 0)
        col = lax.broadcasted_iota(jnp.int32, (tm, tm), 1)
        dlogf = _dot_exact((col >= row).astype(F32), dc) + carry_ref[...]
        carry_ref[...] = dlogf[0:1, :]
        dfl = dlogf * _sigmoid(-fl_ref[...])
        dbf_ref[...] += jnp.sum(dfl, axis=0, keepdims=True)
        dflb = dfl.astype(BF16)
        dflb_ref[...] = dflb

        dqkv_ref[:, :D_ATTN] = dq_ref[...]
        dqkv_ref[:, D_ATTN:2 * D_ATTN] = dk_ref[...]
        dqkv_ref[:, 2 * D_ATTN:] = dv_ref[...]
        dxn = _dot_nt(dqkv_ref[...], wqkv_ref[...]) + _dot_nt(dflb, wf_ref[...]) + _dot_nt(dzb_ref[...], wz_ref[...])

        xt = x_ref[...]
        gmix = gmix_ref[...]
        r = lax.rsqrt(_mean(xt * xt) + EPS)
        xh = xt * r
        xn_ref[...] = (xh * gmix).astype(BF16)
        dgmix_ref[...] += jnp.sum(dxn * xh, axis=0, keepdims=True)
        dyg = dxn * gmix
        dx_ref[...] = dx1_ref[...] + r * (dyg - xh * _mean(dyg * xh))

        @pl.when(step == nt - 1)
        def _():
            for g in range(N_GROUPS):
                dws_ref[g] = dws_ref[g] * mask_ref[...]
            dbs_ref[...] = _dot_exact(dbacc_ref[...], _group_indicator())

    rev = functools.partial(_rev_spec, nt=nt)
    return pl.pallas_call(
        body, name="bwd_a", grid=(nt,),
        in_specs=[rev(tm, D_MODEL), rev(tm, D_MODEL), rev(tm, 2 * D_SGU), rev(tm, LANES), rev(tm, D_SGU),
                  rev(tm, D_ATTN), rev(tm, D_ATTN), rev(tm, D_ATTN), rev(tm, LANES),
                  _const_spec((1, D_MODEL)), _const_spec((D_MODEL, 3 * D_ATTN)), _const_spec((D_MODEL, LANES)),
                  _const_spec((D_MODEL, 2 * D_SGU)), _const_spec((1, D_SGU)), _const_spec((1, D_SGU)),
                  _const_spec((N_GROUPS, SGU_CHUNK, SGU_CHUNK)), _const_spec((N_GROUPS, SGU_CHUNK, SGU_CHUNK)),
                  _const_spec((SGU_CHUNK, D_SGU)), _const_spec((SGU_CHUNK, SGU_CHUNK))],
        out_specs=[rev(tm, D_MODEL), rev(tm, D_MODEL), rev(tm, 3 * D_ATTN), rev(tm, LANES), rev(tm, 2 * D_SGU),
                   _const_spec((1, D_MODEL)), _const_spec((1, LANES)), _const_spec((1, D_SGU)), _const_spec((1, D_SGU)),
                   _const_spec((N_GROUPS, SGU_CHUNK, SGU_CHUNK)), _const_spec((SGU_CHUNK, LANES))],
        out_shape=[SDS((t, D_MODEL), F32), SDS((t, D_MODEL), BF16), SDS((t, 3 * D_ATTN), BF16), SDS((t, LANES), BF16),
                   SDS((t, 2 * D_SGU), BF16), SDS((1, D_MODEL), F32), SDS((1, LANES), F32), SDS((1, D_SGU), F32),
                   SDS((1, D_SGU), F32), SDS((N_GROUPS, SGU_CHUNK, SGU_CHUNK), F32), SDS((SGU_CHUNK, LANES), F32)],
        scratch_shapes=[pltpu.VMEM((1, LANES), F32), pltpu.VMEM((tm, D_SGU), F32), pltpu.VMEM((tm, D_SGU), F32),
                        pltpu.VMEM((SGU_CHUNK, D_SGU), F32)],
        compiler_params=_params("arbitrary"),
    )(dx1, x, z, fl, dsgu, dq, dk, dv, dc, gmix, wqkv, wf, wz, lng, lnb, wsm, wsm_t, bsf, mask)


def _pick(n, cap):
    if n <= cap:
        return n
    best = LANES
    for cand in range(LANES, cap + 1, LANES):
        if n % cand == 0:
            best = cand
    return best


def _tn_matmul(a, b, *, bt):
    t, k1 = a.shape
    n = b.shape[1]
    bk = _pick(k1, 1408)
    bn = _pick(n, 1408)
    nsteps = t // bt

    def body(a_ref, b_ref, o_ref):
        @pl.when(pl.program_id(2) == 0)
        def _():
            o_ref[...] = jnp.zeros_like(o_ref)

        o_ref[...] += _dot_tn(a_ref[...], b_ref[...])

    return pl.pallas_call(
        body, name=f"wgrad_{k1}x{n}", grid=(k1 // bk, n // bn, nsteps),
        in_specs=[pl.BlockSpec((bt, bk), lambda i, j, s: (s, i)), pl.BlockSpec((bt, bn), lambda i, j, s: (s, j))],
        out_specs=pl.BlockSpec((bk, bn), lambda i, j, s: (i, j)),
        out_shape=SDS((k1, n), F32),
        compiler_params=_params("arbitrary", "arbitrary", "arbitrary"),
    )(a, b)


def _adamw(parts, w, m, v, *, name):
    rows, cols = w.shape
    br = _pick_rows(rows, cols)
    c1 = 1.0 - ADAM_B1 ** ADAM_STEP
    c2 = 1.0 - ADAM_B2 ** ADAM_STEP

    def body(p_ref, w_ref, m_ref, v_ref, g_ref, d_ref, nm_ref, nv_ref):
        g = p_ref[0].astype(F32)
        for j in range(1, N_DEV):
            g = g + p_ref[j].astype(F32)
        g_ref[...] = g
        nm = ADAM_B1 * m_ref[...] + (1.0 - ADAM_B1) * g
        nv = ADAM_B2 * v_ref[...] + (1.0 - ADAM_B2) * (g * g)
        nm_ref[...] = nm
        nv_ref[...] = nv
        d_ref[...] = -ADAM_LR * ((nm / c1) / (jnp.sqrt(nv / c2) + ADAM_EPS) + ADAM_WD * w_ref[...])

    spec = pl.BlockSpec((br, cols), lambda i: (i, 0))
    return pl.pallas_call(
        body, name=name, grid=(rows // br,),
        in_specs=[pl.BlockSpec((N_DEV, br, cols), lambda i: (0, i, 0)), spec, spec, spec],
        out_specs=[spec] * 4, out_shape=[SDS((rows, cols), F32)] * 4,
        compiler_params=_params("arbitrary"),
    )(parts, w, m, v)


def _pick_rows(rows, cols):
    target = max(8, (256 * 1024) // cols)
    best = 8
    for cand in range(8, min(rows, target) + 1, 8):
        if rows % cand == 0:
            best = cand
    return best


def _peer(k):
    x, y, c = lax.axis_index("x"), lax.axis_index("y"), lax.axis_index("c")
    px = 1 - x if k & 4 else x
    py = 1 - y if k & 2 else y
    pc = 1 - c if k & 1 else c
    return (px, py, pc), 4 * px + 2 * py + pc


def _exchange(arrs, *, scatter, name):
    n = len(arrs)

    def body(*refs):
        ins, outs = refs[:n], refs[n:2 * n]
        send_sems, recv_sems, local_sems = refs[2 * n:]
        me = 4 * lax.axis_index("x") + 2 * lax.axis_index("y") + lax.axis_index("c")
        local = []
        for a in range(n):
            src = ins[a].at[me] if scatter else ins[a]
            cp = pltpu.make_async_copy(src, outs[a].at[me], local_sems.at[a])
            cp.start()
            local.append(cp)
        for k in range(1, N_DEV):
            peer, pidx = _peer(k)
            for a in range(n):
                pltpu.make_async_remote_copy(
                    src_ref=ins[a].at[pidx] if scatter else ins[a], dst_ref=outs[a].at[me],
                    send_sem=send_sems.at[k - 1, a], recv_sem=recv_sems.at[k - 1, a],
                    device_id=peer, device_id_type=MESH).start()
        for k in range(1, N_DEV):
            peer, pidx = _peer(k)
            for a in range(n):
                pltpu.make_async_remote_copy(
                    src_ref=ins[a].at[pidx] if scatter else ins[a], dst_ref=outs[a].at[pidx],
                    send_sem=send_sems.at[k - 1, a], recv_sem=recv_sems.at[k - 1, a],
                    device_id=peer, device_id_type=MESH).wait()
        for cp in local:
            cp.wait()

    out_shape = [SDS(a.shape if scatter else (N_DEV,) + a.shape, a.dtype) for a in arrs]
    return pl.pallas_call(
        body, name=name, in_specs=[ANY] * n, out_specs=[ANY] * n, out_shape=out_shape,
        scratch_shapes=[pltpu.SemaphoreType.DMA((N_DEV - 1, n)), pltpu.SemaphoreType.DMA((N_DEV - 1, n)),
                        pltpu.SemaphoreType.DMA((n,))],
    )(*arrs)


def _rows_of(col8, tq):
    t = col8.shape[0]
    return col8[:, :N_HEADS].T.reshape(HEAD_PAIRS, 2, t // tq, tq)


def _cols_of(rows):
    cols = rows.reshape(N_HEADS, -1).T
    return jnp.pad(cols, ((0, 0), (0, LANES - N_HEADS)))


def _pairs_to_cols(pair_arr):
    t = pair_arr.shape[1]
    cols = pair_arr[:, :, :2].transpose(1, 0, 2).reshape(t, N_HEADS)
    return jnp.pad(cols, ((0, 0), (0, LANES - N_HEADS)))


def _local_step(x, tgt, small, big):
    t = x.shape[0]
    tm, tq = _tiles(t)
    r = jnp.arange(SGU_CHUNK, dtype=jnp.int32) // SGU_BLOCK
    mask = (r[None, :] <= r[:, None]).astype(F32)

    layers = []
    saved = []
    for l in range(DEPTH):
        w_in, w_out, w_gu, w_dn = big[l]
        wqkv = w_in[:, :3 * D_ATTN]
        wf = jnp.pad(w_in[:, 3 * D_ATTN:3 * D_ATTN + N_HEADS], ((0, 0), (0, LANES - N_HEADS)))
        wz = w_in[:, 3 * D_ATTN + N_HEADS:]
        bf = jnp.pad(small["b_f"][l], (0, LANES - N_HEADS))[None, :]
        wsm = (small["w_s"][l] * mask[None]).astype(BF16)
        wsm_t = jnp.swapaxes(wsm, 1, 2)
        bsf = jnp.repeat(small["b_s"][l].T, GROUP_DIM, axis=1)
        lw = dict(wqkv=wqkv, wf=wf, wz=wz, bf=bf, wsm=wsm, wsm_t=wsm_t, bsf=bsf, w_out=w_out, w_gu=w_gu, w_dn=w_dn,
                  gmix=small["mix_norm_g"][l][None, :], lng=small["sgu_ln_g"][l][None, :],
                  lnb=small["sgu_ln_b"][l][None, :], gout=small["out_norm_g"][l][None, :],
                  gffn=small["ffn_norm_g"][l][None, :])
        layers.append(lw)
        q, k, v, fl, z, sgu = _fwd_a(x, lw["gmix"], wqkv, wf, bf, wz, lw["lng"], lw["lnb"], wsm, bsf, tm=tm)
        o, lse = _attn_fwd(q, k, v, tq=tq)
        x1, x2, gu = _fwd_b(x, o, sgu, lw["gout"], lw["gffn"], w_out, w_gu, w_dn, tm=tm)
        saved.append(dict(x=x, q=q, k=k, v=v, fl=fl, z=z, sgu=sgu, o=o, lse=lse, x1=x1, gu=gu))
        x = x2

    dx, loss, dgfin = _loss_bwd(x, tgt, small["final_norm_g"][None, :], tm=tm)
    grads = {n: [None] * DEPTH for n in ("mix_norm_g", "w_in", "b_f", "sgu_ln_g", "sgu_ln_b", "w_s", "b_s",
                                          "out_norm_g", "w_out", "ffn_norm_g", "w_gate_up", "w_down")}
    for l in reversed(range(DEPTH)):
        lw, sv = layers[l], saved[l]
        (dx1, dx2b, a, dgu, xn2, mrg, dx1b, do, delta, dsgu, dgffn, dgout) = _bwd_b(
            dx, sv["x1"], sv["gu"], sv["o"], sv["sgu"], lw["gout"], lw["gffn"], lw["w_out"], lw["w_gu"], lw["w_dn"],
            tm=tm)
        grads["w_down"][l] = _tn_matmul(a, dx2b, bt=tq)
        grads["w_gate_up"][l] = _tn_matmul(xn2, dgu, bt=tq)
        grads["w_out"][l] = _tn_matmul(mrg, dx1b, bt=tq)
        lse_row = _rows_of(_pairs_to_cols(sv["lse"]), tq)
        dqt, dk, dv, dck, dcq = _attn_bwd(sv["q"], sv["k"], sv["v"], do, lse_row, _rows_of(delta, tq), tq=tq)
        dq = (dqt.transpose(1, 3, 0, 2).reshape(t, D_ATTN) * QK_SCALE).astype(BF16)
        (dx, xn, dqkv, dflb, dzb, dgmix, dbf, dlng, dlnb, dws, dbs) = _bwd_a(
            dx1, sv["x"], sv["z"], sv["fl"], dsgu, dq, dk, dv, _pairs_to_cols(dck) + _cols_of(dcq), lw["gmix"], lw["wqkv"], lw["wf"],
            lw["wz"], lw["lng"], lw["lnb"], lw["wsm"], lw["wsm_t"], lw["bsf"], mask, tm=tm)
        grads["w_in"][l] = jnp.concatenate(
            [_tn_matmul(xn, dqkv, bt=tq), _tn_matmul(xn, dflb, bt=tq)[:, :N_HEADS], _tn_matmul(xn, dzb, bt=tq)], axis=1)
        grads["mix_norm_g"][l] = dgmix[0]
        grads["b_f"][l] = dbf[0, :N_HEADS]
        grads["sgu_ln_g"][l] = dlng[0]
        grads["sgu_ln_b"][l] = dlnb[0]
        grads["w_s"][l] = dws
        grads["b_s"][l] = dbs[:, :N_GROUPS].T
        grads["out_norm_g"][l] = dgout[0]
        grads["ffn_norm_g"][l] = dgffn[0]
    grads = {n: jnp.stack(g) for n, g in grads.items()}
    grads["final_norm_g"] = dgfin[0]
    return loss[0, 0], dx, grads


SMALL = ("mix_norm_g", "b_f", "sgu_ln_g", "sgu_ln_b", "w_s", "b_s", "out_norm_g", "ffn_norm_g", "final_norm_g")
BIG = ("w_in", "w_out", "w_gate_up", "w_down")
WEIGHTS = ("mix_norm_g", "w_in", "b_f", "sgu_ln_g", "sgu_ln_b", "w_s", "b_s", "out_norm_g", "w_out", "ffn_norm_g",
           "w_gate_up", "w_down", "final_norm_g")
SHARD_AXIS = {"w_in": 1, "w_out": 0, "w_gate_up": 1, "w_down": 0}


def _assemble(gathered, name):
    if SHARD_AXIS[name] == 0:
        full = gathered.transpose(1, 0, 2, 3)
        return full.reshape(DEPTH, -1, full.shape[-1])
    full = gathered.transpose(1, 2, 0, 3)
    return full.reshape(DEPTH, full.shape[1], -1)


def _split(full, name):
    d, rows, cols = full.shape
    if SHARD_AXIS[name] == 0:
        return full.reshape(d, N_DEV, rows // N_DEV, cols).transpose(1, 0, 2, 3)
    return full.reshape(d, rows, N_DEV, cols // N_DEV).transpose(2, 0, 1, 3)


def _pack(tree):
    flat = jnp.concatenate([tree[n].reshape(-1) for n in SMALL])
    pad = (-flat.shape[0]) % (8 * LANES)
    return jnp.pad(flat, (0, pad)).reshape(-1, LANES)


def _unpack(packed, like):
    flat = packed.reshape(-1)
    out, at = {}, 0
    for n in SMALL:
        size = like[n].size
        out[n] = flat[at:at + size].reshape(like[n].shape)
        at += size
    return out


def kernel(x, mix_norm_g, w_in, b_f, sgu_ln_g, sgu_ln_b, w_s, b_s, out_norm_g, w_out, ffn_norm_g, w_gate_up, w_down, final_norm_g, loss_target, m_mix_norm_g, m_w_in, m_b_f, m_sgu_ln_g, m_sgu_ln_b, m_w_s, m_b_s, m_out_norm_g, m_w_out, m_ffn_norm_g, m_w_gate_up, m_w_down, m_final_norm_g, v_mix_norm_g, v_w_in, v_b_f, v_sgu_ln_g, v_sgu_ln_b, v_w_s, v_b_s, v_out_norm_g, v_w_out, v_ffn_norm_g, v_w_gate_up, v_w_down, v_final_norm_g):
    w = dict(mix_norm_g=mix_norm_g, w_in=w_in, b_f=b_f, sgu_ln_g=sgu_ln_g, sgu_ln_b=sgu_ln_b, w_s=w_s, b_s=b_s,
             out_norm_g=out_norm_g, w_out=w_out, ffn_norm_g=ffn_norm_g, w_gate_up=w_gate_up, w_down=w_down,
             final_norm_g=final_norm_g)
    m = dict(mix_norm_g=m_mix_norm_g, w_in=m_w_in, b_f=m_b_f, sgu_ln_g=m_sgu_ln_g, sgu_ln_b=m_sgu_ln_b, w_s=m_w_s,
             b_s=m_b_s, out_norm_g=m_out_norm_g, w_out=m_w_out, ffn_norm_g=m_ffn_norm_g, w_gate_up=m_w_gate_up,
             w_down=m_w_down, final_norm_g=m_final_norm_g)
    v = dict(mix_norm_g=v_mix_norm_g, w_in=v_w_in, b_f=v_b_f, sgu_ln_g=v_sgu_ln_g, sgu_ln_b=v_sgu_ln_b, w_s=v_w_s,
             b_s=v_b_s, out_norm_g=v_out_norm_g, w_out=v_w_out, ffn_norm_g=v_ffn_norm_g, w_gate_up=v_w_gate_up,
             w_down=v_w_down, final_norm_g=v_final_norm_g)

    gathered = _exchange([w[n].astype(BF16) for n in BIG], scatter=False, name="gather_weights")
    full = {n: _assemble(g, n) for n, g in zip(BIG, gathered)}
    big = [tuple(full[n][l] for n in BIG) for l in range(DEPTH)]
    small = {n: w[n] for n in SMALL}

    loss, dx, grads = _local_step(x[0], loss_target[0], small, big)
    loss = lax.psum(loss, ("x", "y", "c"))

    parts = _exchange([_split(grads[n], n).astype(BF16) for n in BIG], scatter=True, name="scatter_grads")
    small_parts = _exchange([_pack(grads)], scatter=False, name="gather_small_grads")[0]

    g_out, d_out, m_out, v_out = {}, {}, {}, {}
    for n, part in zip(BIG, parts):
        shape = w[n].shape
        two_d = lambda a: a.reshape(-1, shape[-1])
        res = _adamw(part.reshape(N_DEV, -1, shape[-1]), two_d(w[n]), two_d(m[n]), two_d(v[n]), name=f"adamw_{n}")
        g_out[n], d_out[n], m_out[n], v_out[n] = (r.reshape(shape) for r in res)
    res = _adamw(small_parts, _pack(w), _pack(m), _pack(v), name="adamw_small")
    for dst, packed in zip((g_out, d_out, m_out, v_out), res):
        dst.update(_unpack(packed, w))

    return (loss, dx[None], *[g_out[n] for n in WEIGHTS], *[d_out[n] for n in WEIGHTS],
            *[m_out[n] for n in WEIGHTS], *[v_out[n] for n in WEIGHTS])
```

```python
import functools
import math

import jax
import jax.numpy as jnp
from jax import lax
from jax.experimental import pallas as pl
from jax.experimental.pallas import tpu as pltpu

F32, BF16 = jnp.float32, jnp.bfloat16
HIGHEST = lax.Precision.HIGHEST
MESH = pl.DeviceIdType.MESH
ANY = pl.BlockSpec(memory_space=pl.ANY)
SDS = jax.ShapeDtypeStruct

N_DEV = 8
DEPTH = 4
D_MODEL = 1024
D_ATTN = 512
D_SGU = 512
N_HEADS = 8
HEAD_DIM = 64
HEAD_PAIRS = N_HEADS // 2
SGU_CHUNK = 128
SGU_BLOCK = 64
N_GROUPS = 8
GROUP_DIM = 64
D_FF = 2816
FF_CHUNK = 1408
N_FF_CHUNKS = D_FF // FF_CHUNK
D_IN = 3 * D_ATTN + N_HEADS + 2 * D_SGU
LANES = 128
EPS = 1e-6
QK_SCALE = HEAD_DIM ** -0.5
INV_SQRT2 = 1.0 / math.sqrt(2.0)
INV_SQRT_2PI = 1.0 / math.sqrt(2.0 * math.pi)
LOG2E = 1.0 / math.log(2.0)
LN2 = math.log(2.0)
ROW_CHUNK = 32

ADAM_LR = 0.001
ADAM_B1 = 0.9
ADAM_B2 = 0.999
ADAM_EPS = 1e-08
ADAM_WD = 0.01
ADAM_STEP = 10

VMEM_LIMIT_BYTES = 56 * 1024 * 1024


def _params(*sem):
    return pltpu.CompilerParams(dimension_semantics=sem or None, vmem_limit_bytes=VMEM_LIMIT_BYTES)


def _dot(a, b):
    return jnp.dot(a, b, preferred_element_type=F32)


def _dot_nt(a, b):
    return lax.dot_general(a, b, (((1,), (1,)), ((), ())), preferred_element_type=F32)


def _dot_tn(a, b):
    return lax.dot_general(a, b, (((0,), (0,)), ((), ())), preferred_element_type=F32)


def _dot_exact(a, b):
    return jnp.dot(a, b, precision=HIGHEST, preferred_element_type=F32)


def _mean(v):
    return jnp.mean(v, axis=-1, keepdims=True)


def _sigmoid(v):
    return 1.0 / (1.0 + jnp.exp(-v))


def _row_spec(tm, n):
    return pl.BlockSpec((tm, n), lambda i: (i, 0))


def _rev_spec(tm, n, nt):
    return pl.BlockSpec((tm, n), lambda i: (nt - 1 - i, 0))


def _const_spec(shape):
    return pl.BlockSpec(shape, lambda i: (0,) * len(shape))


def _tiles(t):
    return min(256, t), min(512, t)


def _group_indicator():
    r = lax.broadcasted_iota(jnp.int32, (D_ATTN, LANES), 0)
    c = lax.broadcasted_iota(jnp.int32, (D_ATTN, LANES), 1)
    return ((r >> 6) == c).astype(F32)


def _sgu_mix(w_ref, zc, lane_grp):
    out = jnp.zeros((SGU_CHUNK, D_SGU), F32)
    for g in range(N_GROUPS):
        out = out + jnp.where(lane_grp == g, _dot(w_ref[g], zc), 0.0)
    return out


def _fwd_a(x, gmix, wqkv, wf, bf, wz, lng, lnb, wsm, bsf, *, tm):
    t = x.shape[0]
    nt = t // tm
    nch = tm // SGU_CHUNK

    def body(x_ref, gmix_ref, wqkv_ref, wf_ref, bf_ref, wz_ref, lng_ref, lnb_ref, wsm_ref, bsf_ref,
             q_ref, k_ref, v_ref, fl_ref, z_ref, sgu_ref, carry_ref):
        @pl.when(pl.program_id(0) == 0)
        def _():
            carry_ref[...] = jnp.zeros_like(carry_ref)

        xt = x_ref[...]
        r = lax.rsqrt(_mean(xt * xt) + EPS)
        xn = ((xt * r) * gmix_ref[...]).astype(BF16)
        qkv = _dot(xn, wqkv_ref[...])

        fl = _dot(xn, wf_ref[...]) + bf_ref[...]
        fl_ref[...] = fl
        logf = jnp.minimum(fl, 0.0) - jnp.log1p(jnp.exp(-jnp.abs(fl)))
        row = lax.broadcasted_iota(jnp.int32, (tm, tm), 0)
        col = lax.broadcasted_iota(jnp.int32, (tm, tm), 1)
        c = _dot_exact((col <= row).astype(F32), logf) + carry_ref[...]
        carry_ref[...] = c[tm - 1:tm, :]

        c2 = c * LOG2E
        lane = lax.broadcasted_iota(jnp.int32, (tm, LANES), 1)
        for h in range(N_HEADS):
            pair, hh = divmod(h, 2)
            base = _aug_lane(hh)
            in_head = (lane >= hh * HEAD_DIM) & (lane < (hh + 1) * HEAD_DIM)
            col_h = jnp.sum(jnp.where(lane == h, c2, 0.0), axis=1, keepdims=True)
            hi = col_h.astype(BF16).astype(F32)
            mid = (col_h - hi).astype(BF16).astype(F32)
            lo = (col_h - hi) - mid
            split = jnp.where(lane == base, hi, jnp.where(lane == base + 1, mid, jnp.where(lane == base + 2, lo, 0.0)))
            split_k = jnp.where(lane == base + 3, hi, jnp.where(lane == base + 4, mid,
                                                                 jnp.where(lane == base + 5, lo, 0.0)))
            ones_q = ((lane >= base + 3) & (lane < base + 6)).astype(F32)
            ones_k = ((lane >= base) & (lane < base + 3)).astype(F32)
            blk = slice(pair * LANES, (pair + 1) * LANES)
            q_ref[h] = jnp.where(in_head, qkv[:, blk] * (QK_SCALE * LOG2E), split + ones_q).astype(BF16)
            k_ref[h] = jnp.where(in_head, qkv[:, D_ATTN:2 * D_ATTN][:, blk], ones_k - split_k).astype(BF16)
            v_ref[h] = jnp.where(in_head, qkv[:, 2 * D_ATTN:][:, blk], (lane == base).astype(F32)).astype(BF16)

        z = _dot(xn, wz_ref[...])
        z_ref[...] = z
        zg = 0.5 * z * (1.0 + lax.erf(z * INV_SQRT2))
        zu = zg[:, :D_SGU]
        zv = zg[:, D_SGU:]
        xc = zv - _mean(zv)
        zvn = ((xc * lax.rsqrt(_mean(xc * xc) + EPS)) * lng_ref[...] + lnb_ref[...]).astype(BF16)
        lane_grp = lax.broadcasted_iota(jnp.int32, (SGU_CHUNK, D_SGU), 1) >> 6
        for ch in range(nch):
            rows = slice(ch * SGU_CHUNK, (ch + 1) * SGU_CHUNK)
            mixed = _sgu_mix(wsm_ref, zvn[rows, :], lane_grp) + bsf_ref[...]
            sgu_ref[rows, :] = zu[rows, :] * mixed

    head_spec = pl.BlockSpec((N_HEADS, tm, LANES), lambda i: (0, i, 0))
    return pl.pallas_call(
        body, name="fwd_a", grid=(nt,),
        in_specs=[_row_spec(tm, D_MODEL), _const_spec((1, D_MODEL)), _const_spec((D_MODEL, 3 * D_ATTN)),
                  _const_spec((D_MODEL, LANES)), _const_spec((1, LANES)), _const_spec((D_MODEL, 2 * D_SGU)),
                  _const_spec((1, D_SGU)), _const_spec((1, D_SGU)), _const_spec((N_GROUPS, SGU_CHUNK, SGU_CHUNK)),
                  _const_spec((SGU_CHUNK, D_SGU))],
        out_specs=[head_spec, head_spec, head_spec, _row_spec(tm, LANES), _row_spec(tm, 2 * D_SGU),
                   _row_spec(tm, D_SGU)],
        out_shape=[SDS((N_HEADS, t, LANES), BF16)] * 3 + [SDS((t, LANES), F32), SDS((t, 2 * D_SGU), F32),
                                                          SDS((t, D_SGU), F32)],
        scratch_shapes=[pltpu.VMEM((1, LANES), F32)],
        compiler_params=_params("arbitrary"),
    )(x, gmix, wqkv, wf, bf, wz, lng, lnb, wsm, bsf)


def _aug_lane(hh):
    return (1 - hh) * HEAD_DIM


def _attn_fwd(qa, ka, va, *, tq):
    t = qa.shape[1]
    nq = t // tq
    nrc = tq // ROW_CHUNK

    def body(q_ref, k_hbm, v_hbm, o_ref, lse_ref, k_vm, v_vm, s_ref, p_ref, m_ref, a_ref, acc_ref):
        p = pl.program_id(0)
        i = pl.program_id(1)

        @pl.when(i == 0)
        def _():
            pltpu.sync_copy(k_hbm.at[pl.ds(2 * p, 2)], k_vm)
            pltpu.sync_copy(v_hbm.at[pl.ds(2 * p, 2)], v_vm)

        m_ref[...] = jnp.full(m_ref.shape, -jnp.inf, F32)
        acc_ref[...] = jnp.zeros_like(acc_ref)
        rowq = lax.broadcasted_iota(jnp.int32, (ROW_CHUNK, tq), 0)
        colk = lax.broadcasted_iota(jnp.int32, (ROW_CHUNK, tq), 1)

        def scores(j, slot):
            start = pl.multiple_of(j * tq, tq)
            for h in range(2):
                s_ref[slot, h] = _dot_nt(q_ref[h], k_vm[h, pl.ds(start, tq), :])

        def softmax(slot, masked):
            for h in range(2):
                for r in range(nrc):
                    rows = slice(r * ROW_CHUNK, (r + 1) * ROW_CHUNK)
                    sc = s_ref[slot, h, rows, :]
                    if masked:
                        sc = jnp.where(colk <= rowq + r * ROW_CHUNK, sc, -jnp.inf)
                    m_old = m_ref[h, rows, :]
                    m_new = jnp.maximum(m_old, jnp.max(sc, axis=1, keepdims=True))
                    p_ref[slot, h, rows, :] = jnp.exp2(sc - m_new).astype(BF16)
                    a_ref[slot, h, rows, :] = jnp.exp2(m_old - m_new)
                    m_ref[h, rows, :] = m_new

        def accumulate(j, slot):
            start = pl.multiple_of(j * tq, tq)
            for h in range(2):
                acc_ref[h] = acc_ref[h] * a_ref[slot, h] + _dot(p_ref[slot, h], v_vm[h, pl.ds(start, tq), :])

        scores(0, 0)

        @pl.when(i > 0)
        def _():
            scores(1, 1)
            softmax(0, False)

        def stage(j, slot):
            scores(j + 1, 1 - slot)
            softmax(slot, False)
            accumulate(j - 1, 1 - slot)

        def pair_body(n, carry):
            j = 1 + 2 * n
            stage(j, 1)
            stage(j + 1, 0)
            return carry

        rest = i - 1
        lax.fori_loop(0, rest // 2, pair_body, 0)

        @pl.when((rest > 0) & (lax.rem(rest, 2) == 1))
        def _():
            stage(i - 1, 1)

        @pl.when(lax.rem(i, 2) == 0)
        def _():
            softmax(0, True)

            @pl.when(i > 0)
            def _():
                accumulate(i - 1, 1)

            accumulate(i, 0)

        @pl.when(lax.rem(i, 2) == 1)
        def _():
            softmax(1, True)
            accumulate(i - 1, 0)
            accumulate(i, 1)

        lane = lax.broadcasted_iota(jnp.int32, (tq, LANES), 1)
        l_h = [jnp.sum(jnp.where(lane == _aug_lane(h), acc_ref[h], 0.0), axis=1, keepdims=True) for h in range(2)]
        o_ref[...] = jnp.where(lane < HEAD_DIM, acc_ref[0] / l_h[0], acc_ref[1] / l_h[1])
        lse = [m_ref[h] + jnp.log2(l_h[h]) for h in range(2)]
        lse_ref[0] = jnp.where(lane == 0, lse[0], jnp.where(lane == 1, lse[1], 0.0))

    return pl.pallas_call(
        body, name="attn_fwd", grid=(HEAD_PAIRS, nq),
        in_specs=[pl.BlockSpec((2, tq, LANES), lambda p, i: (p, i, 0)), ANY, ANY],
        out_specs=[pl.BlockSpec((tq, LANES), lambda p, i: (i, p)),
                   pl.BlockSpec((1, tq, LANES), lambda p, i: (p, i, 0))],
        out_shape=[SDS((t, D_ATTN), F32), SDS((HEAD_PAIRS, t, LANES), F32)],
        scratch_shapes=[pltpu.VMEM((2, t, LANES), BF16), pltpu.VMEM((2, t, LANES), BF16),
                        pltpu.VMEM((2, 2, tq, tq), F32), pltpu.VMEM((2, 2, tq, tq), BF16),
                        pltpu.VMEM((2, tq, 1), F32), pltpu.VMEM((2, 2, tq, 1), F32), pltpu.VMEM((2, tq, LANES), F32)],
        compiler_params=_params("arbitrary", "arbitrary"),
    )(qa, ka, va)


def _attn_bwd(qa, ka, va, do, lse_row, delta_row, *, tq):
    t = qa.shape[1]
    nq = t // tq
    nrc = tq // ROW_CHUNK

    def body(q_hbm, do_ref, k_ref, v_ref, lse_ref, dl_ref, dqt_ref, dk_ref, dv_ref, dck_ref, dcq_ref,
             q_vm, st_ref, dp_ref, pt_ref, ds_ref, dka_ref, dva_ref):
        p = pl.program_id(0)
        j = pl.program_id(1)

        @pl.when(j == 0)
        def _():
            pltpu.sync_copy(q_hbm.at[pl.ds(2 * p, 2)], q_vm)
            dqt_ref[...] = jnp.zeros_like(dqt_ref)
            dcq_ref[...] = jnp.zeros_like(dcq_ref)

        dka_ref[...] = jnp.zeros_like(dka_ref)
        dva_ref[...] = jnp.zeros_like(dva_ref)
        lane = lax.broadcasted_iota(jnp.int32, (tq, LANES), 1)
        in_head = (lane < HEAD_DIM, lane >= HEAD_DIM)
        k_aug = [k_ref[h] for h in range(2)]
        zero = jnp.zeros_like(k_aug[0])
        v_head = [jnp.where(in_head[h], v_ref[h], zero) for h in range(2)]
        kt_ext = [jnp.concatenate([jnp.where(in_head[h], k_aug[h], zero).astype(F32).T.astype(BF16),
                                   jnp.ones((8, tq), BF16)], axis=0) for h in range(2)]
        rowk = lax.broadcasted_iota(jnp.int32, (ROW_CHUNK, tq), 0)
        colq = lax.broadcasted_iota(jnp.int32, (ROW_CHUNK, tq), 1)

        def step(i, masked):
            start = pl.multiple_of(i * tq, tq)
            do2 = do_ref[pl.ds(start, tq), :]
            q_h = [q_vm[h, pl.ds(start, tq), :] for h in range(2)]
            for h in range(2):
                st_ref[h] = _dot_nt(k_aug[h], q_h[h])
                dp_ref[h] = _dot_nt(v_head[h], do2)
            for h in range(2):
                lse = lse_ref[0, h, pl.ds(i, 1), :]
                delta = dl_ref[0, h, pl.ds(i, 1), :]
                for r in range(nrc):
                    rows = slice(r * ROW_CHUNK, (r + 1) * ROW_CHUNK)
                    st = st_ref[h, rows, :]
                    if masked:
                        st = jnp.where(rowk + r * ROW_CHUNK <= colq, st, -jnp.inf)
                    pt = jnp.exp2(st - lse)
                    pt_ref[h, rows, :] = pt.astype(BF16)
                    ds_ref[h, rows, :] = (pt * (dp_ref[h, rows, :] - delta)).astype(BF16)
            dq_t = jnp.zeros((LANES, tq), F32)
            for h in range(2):
                dva_ref[h] += _dot(pt_ref[h], do2)
                dka_ref[h] += _dot(ds_ref[h], q_h[h])
                ext = _dot(kt_ext[h], ds_ref[h])
                dq_t = dq_t + ext[:LANES, :]
                dcq_ref[0, h, pl.ds(i, 1), :] += ext[LANES:LANES + 1, :]
            dqt_ref[0, i] += dq_t

        def loop_body(i, carry):
            step(i, False)
            return carry

        step(j, True)
        lax.fori_loop(j + 1, nq, loop_body, 0)

        dk_ref[...] = (jnp.where(in_head[0], dka_ref[0], dka_ref[1]) * LN2).astype(BF16)
        dv_ref[...] = jnp.where(in_head[0], dva_ref[0], dva_ref[1]).astype(BF16)
        dck = [jnp.sum(jnp.where(lane == _aug_lane(h) + 3, dka_ref[h], 0.0), axis=1, keepdims=True) for h in range(2)]
        dck_ref[0] = jnp.where(lane == 0, -dck[0], jnp.where(lane == 1, -dck[1], 0.0))

    rows = pl.BlockSpec((1, 2, nq, tq), lambda p, j: (p, 0, 0, 0))
    return pl.pallas_call(
        body, name="attn_bwd", grid=(HEAD_PAIRS, nq),
        in_specs=[ANY, pl.BlockSpec((t, LANES), lambda p, j: (0, p)),
                  pl.BlockSpec((2, tq, LANES), lambda p, j: (p, j, 0)),
                  pl.BlockSpec((2, tq, LANES), lambda p, j: (p, j, 0)), rows, rows],
        out_specs=[pl.BlockSpec((1, nq, LANES, tq), lambda p, j: (p, 0, 0, 0)),
                   pl.BlockSpec((tq, LANES), lambda p, j: (j, p)),
                   pl.BlockSpec((tq, LANES), lambda p, j: (j, p)),
                   pl.BlockSpec((1, tq, LANES), lambda p, j: (p, j, 0)), rows],
        out_shape=[SDS((HEAD_PAIRS, nq, LANES, tq), F32), SDS((t, D_ATTN), BF16), SDS((t, D_ATTN), BF16),
                   SDS((HEAD_PAIRS, t, LANES), F32), SDS((HEAD_PAIRS, 2, nq, tq), F32)],
        scratch_shapes=[pltpu.VMEM((2, t, LANES), BF16), pltpu.VMEM((2, tq, tq), F32), pltpu.VMEM((2, tq, tq), F32),
                        pltpu.VMEM((2, tq, tq), BF16), pltpu.VMEM((2, tq, tq), BF16),
                        pltpu.VMEM((2, tq, LANES), F32), pltpu.VMEM((2, tq, LANES), F32)],
        compiler_params=_params("arbitrary", "arbitrary"),
    )(qa, do, ka, va, lse_row, delta_row)

def _load_weights_once(pairs):
    @pl.when(pl.program_id(0) == 0)
    def _():
        for src, dst in pairs:
            pltpu.sync_copy(src, dst)


def _row_tile_call(body, name, nt, operands, *, in_specs, out_specs, out_shape, scratch_shapes, exchange, scatter):
    if not exchange:
        return pl.pallas_call(body, name=name, grid=(nt,), in_specs=in_specs, out_specs=out_specs,
                              out_shape=out_shape, scratch_shapes=scratch_shapes,
                              compiler_params=_params("arbitrary"))(*operands)
    flags = [scatter] * len(exchange)
    n = len(exchange)
    return pl.pallas_call(
        _fused_exchange(body, len(operands), len(out_shape), flags, nt),
        name=name + ("_scatter" if scatter else "_gather"), grid=(nt,),
        in_specs=list(in_specs) + [ANY] * n, out_specs=list(out_specs) + [ANY] * n,
        out_shape=list(out_shape) + _exchange_shapes(exchange, flags),
        scratch_shapes=list(scratch_shapes) + _exchange_scratch(n),
        compiler_params=_params("arbitrary"))(*operands, *exchange)


def _fwd_b(x, o, sgu, gout, gffn, w_out, w_gu, w_dn, *, tm, gather=()):
    t = x.shape[0]
    nt = t // tm

    def body(x_ref, o_ref, s_ref, gout_ref, gffn_ref, wout_hbm, wgu_hbm, wdn_hbm,
             x1_ref, x2_ref, gu_ref, wout, wgu, wdn):
        _load_weights_once(((wout_hbm, wout), (wgu_hbm, wgu), (wdn_hbm, wdn)))
        ov = o_ref[...]
        sv = s_ref[...]
        mo = ((ov * lax.rsqrt(_mean(ov * ov) + EPS)) * gout_ref[:, :D_ATTN]).astype(BF16)
        ms = ((sv * lax.rsqrt(_mean(sv * sv) + EPS)) * gout_ref[:, D_ATTN:]).astype(BF16)
        x1 = x_ref[...] + (_dot(mo, wout[:D_ATTN, :]) + _dot(ms, wout[D_ATTN:, :]))
        x1_ref[...] = x1
        xn2 = ((x1 * lax.rsqrt(_mean(x1 * x1) + EPS)) * gffn_ref[...]).astype(BF16)
        y = jnp.zeros((tm, D_MODEL), F32)
        for n in range(N_FF_CHUNKS):
            lo, hi = n * FF_CHUNK, (n + 1) * FF_CHUNK
            gate = _dot(xn2, wgu[:, lo:hi])
            up = _dot(xn2, wgu[:, D_FF + lo:D_FF + hi])
            gu_ref[:, lo:hi] = gate
            gu_ref[:, D_FF + lo:D_FF + hi] = up
            a = ((gate * _sigmoid(gate)) * up).astype(BF16)
            y = y + _dot(a, wdn[lo:hi, :])
        x2_ref[...] = x1 + y

    return _row_tile_call(
        body, "fwd_b", nt, (x, o, sgu, gout, gffn, w_out, w_gu, w_dn),
        in_specs=[_row_spec(tm, D_MODEL), _row_spec(tm, D_ATTN), _row_spec(tm, D_SGU),
                  _const_spec((1, D_MODEL)), _const_spec((1, D_MODEL)), ANY, ANY, ANY],
        out_specs=[_row_spec(tm, D_MODEL), _row_spec(tm, D_MODEL), _row_spec(tm, 2 * D_FF)],
        out_shape=[SDS((t, D_MODEL), F32), SDS((t, D_MODEL), F32), SDS((t, 2 * D_FF), F32)],
        scratch_shapes=[pltpu.VMEM((D_MODEL, D_MODEL), BF16), pltpu.VMEM((D_MODEL, 2 * D_FF), BF16),
                        pltpu.VMEM((D_FF, D_MODEL), BF16)],
        exchange=gather, scatter=False)


def _loss_bwd(x, tgt, gfin, *, tm):
    t = x.shape[0]
    nt = t // tm

    def body(x_ref, t_ref, g_ref, dx_ref, loss_ref, dg_ref):
        @pl.when(pl.program_id(0) == 0)
        def _():
            loss_ref[...] = jnp.zeros_like(loss_ref)
            dg_ref[...] = jnp.zeros_like(dg_ref)

        xt = x_ref[...]
        g = g_ref[...]
        r = lax.rsqrt(_mean(xt * xt) + EPS)
        xh = xt * r
        err = xh * g - t_ref[...]
        loss_ref[...] += 0.5 * jnp.sum(_mean(err * err), axis=0, keepdims=True)
        dy = err * (1.0 / D_MODEL)
        dg_ref[...] += jnp.sum(dy * xh, axis=0, keepdims=True)
        dyg = dy * g
        dx_ref[...] = r * (dyg - xh * _mean(dyg * xh))

    return pl.pallas_call(
        body, name="loss_bwd", grid=(nt,),
        in_specs=[_row_spec(tm, D_MODEL), _row_spec(tm, D_MODEL), _const_spec((1, D_MODEL))],
        out_specs=[_row_spec(tm, D_MODEL), _const_spec((1, 1)), _const_spec((1, D_MODEL))],
        out_shape=[SDS((t, D_MODEL), F32), SDS((1, 1), F32), SDS((1, D_MODEL), F32)],
        compiler_params=_params("arbitrary"),
    )(x, tgt, gfin)


def _bwd_b(dx2, x1, gu, o, sgu, gout, gffn, w_out, w_gu, w_dn, *, tm, scatter=()):
    t = dx2.shape[0]
    nt = t // tm

    def body(dx2_ref, x1_ref, gu_ref, o_ref, s_ref, gout_ref, gffn_ref, wout_hbm, wgu_hbm, wdn_hbm,
             dx1_ref, dx2b_ref, a_ref, dgu_ref, xn2_ref, mrg_ref, dx1b_ref, do_ref, dl_ref, dsgu_ref,
             dgffn_ref, dgout_ref, wout, wgu, wdn):
        _load_weights_once(((wout_hbm, wout), (wgu_hbm, wgu), (wdn_hbm, wdn)))

        @pl.when(pl.program_id(0) == 0)
        def _():
            dgffn_ref[...] = jnp.zeros_like(dgffn_ref)
            dgout_ref[...] = jnp.zeros_like(dgout_ref)

        dx2 = dx2_ref[...]
        dx2b = dx2.astype(BF16)
        dx2b_ref[...] = dx2b
        dxn2 = jnp.zeros((tm, D_MODEL), F32)
        for n in range(N_FF_CHUNKS):
            lo, hi = n * FF_CHUNK, (n + 1) * FF_CHUNK
            gate = gu_ref[:, lo:hi]
            up = gu_ref[:, D_FF + lo:D_FF + hi]
            sg = _sigmoid(gate)
            si = gate * sg
            a_ref[:, lo:hi] = (si * up).astype(BF16)
            d_a = _dot_nt(dx2b, wdn[lo:hi, :])
            dgb = ((d_a * up) * (sg * (1.0 + gate * (1.0 - sg)))).astype(BF16)
            dub = (d_a * si).astype(BF16)
            dgu_ref[:, lo:hi] = dgb
            dgu_ref[:, D_FF + lo:D_FF + hi] = dub
            dxn2 = dxn2 + (_dot_nt(dgb, wgu[:, lo:hi]) + _dot_nt(dub, wgu[:, D_FF + lo:D_FF + hi]))

        x1 = x1_ref[...]
        gffn = gffn_ref[...]
        r1 = lax.rsqrt(_mean(x1 * x1) + EPS)
        xh1 = x1 * r1
        xn2_ref[...] = (xh1 * gffn).astype(BF16)
        dgffn_ref[...] += jnp.sum(dxn2 * xh1, axis=0, keepdims=True)
        dyg = dxn2 * gffn
        dx1 = dx2 + r1 * (dyg - xh1 * _mean(dyg * xh1))
        dx1_ref[...] = dx1
        dx1b = dx1.astype(BF16)
        dx1b_ref[...] = dx1b

        ov = o_ref[...]
        sv = s_ref[...]
        go = gout_ref[:, :D_ATTN]
        gs = gout_ref[:, D_ATTN:]
        ro = lax.rsqrt(_mean(ov * ov) + EPS)
        rs = lax.rsqrt(_mean(sv * sv) + EPS)
        oh = ov * ro
        sh = sv * rs
        mrg_ref[:, :D_ATTN] = (oh * go).astype(BF16)
        mrg_ref[:, D_ATTN:] = (sh * gs).astype(BF16)
        dmo = _dot_nt(dx1b, wout[:D_ATTN, :])
        dms = _dot_nt(dx1b, wout[D_ATTN:, :])
        dgout_ref[:, :D_ATTN] += jnp.sum(dmo * oh, axis=0, keepdims=True)
        dgout_ref[:, D_ATTN:] += jnp.sum(dms * sh, axis=0, keepdims=True)
        dmog = dmo * go
        d_o = ro * (dmog - oh * _mean(dmog * oh))
        do_ref[...] = d_o.astype(BF16)
        dl_ref[...] = _dot_exact(d_o * ov, _group_indicator())
        dmsg = dms * gs
        dsgu_ref[...] = rs * (dmsg - sh * _mean(dmsg * sh))

    return _row_tile_call(
        body, "bwd_b", nt, (dx2, x1, gu, o, sgu, gout, gffn, w_out, w_gu, w_dn),
        in_specs=[_row_spec(tm, D_MODEL), _row_spec(tm, D_MODEL), _row_spec(tm, 2 * D_FF), _row_spec(tm, D_ATTN),
                  _row_spec(tm, D_SGU), _const_spec((1, D_MODEL)), _const_spec((1, D_MODEL)), ANY, ANY, ANY],
        out_specs=[_row_spec(tm, D_MODEL), _row_spec(tm, D_MODEL), _row_spec(tm, D_FF), _row_spec(tm, 2 * D_FF),
                   _row_spec(tm, D_MODEL), _row_spec(tm, D_MODEL), _row_spec(tm, D_MODEL), _row_spec(tm, D_ATTN),
                   _row_spec(tm, LANES), _row_spec(tm, D_SGU), _const_spec((1, D_MODEL)), _const_spec((1, D_MODEL))],
        out_shape=[SDS((t, D_MODEL), F32), SDS((t, D_MODEL), BF16), SDS((t, D_FF), BF16), SDS((t, 2 * D_FF), BF16),
                   SDS((t, D_MODEL), BF16), SDS((t, D_MODEL), BF16), SDS((t, D_MODEL), BF16), SDS((t, D_ATTN), BF16),
                   SDS((t, LANES), F32), SDS((t, D_SGU), F32), SDS((1, D_MODEL), F32), SDS((1, D_MODEL), F32)],
        scratch_shapes=[pltpu.VMEM((D_MODEL, D_MODEL), BF16), pltpu.VMEM((D_MODEL, 2 * D_FF), BF16),
                        pltpu.VMEM((D_FF, D_MODEL), BF16)],
        exchange=scatter, scatter=True)


def _bwd_a(dx1, x, z, fl, dsgu, dq, dk, dv, dc, gmix, wqkv, wf, wz, lng, lnb, wsm, wsm_t, bsf, mask, *, tm):
    t = x.shape[0]
    nt = t // tm
    nch = tm // SGU_CHUNK

    def body(dx1_ref, x_ref, z_ref, fl_ref, dsgu_ref, dq_ref, dk_ref, dv_ref, dc_ref, gmix_ref, wqkv_ref, wf_ref,
             wz_ref, lng_ref, lnb_ref, wsm_ref, wsmt_ref, bsf_ref, mask_ref,
             dx_ref, xn_ref, dqkv_ref, dflb_ref, dzb_ref, dgmix_ref, dbf_ref, dlng_ref, dlnb_ref, dws_ref, dbs_ref,
             carry_ref, dzvn_ref, dzu_ref, dbacc_ref):
        step = pl.program_id(0)

        @pl.when(step == 0)
        def _():
            carry_ref[...] = jnp.zeros_like(carry_ref)
            dbacc_ref[...] = jnp.zeros_like(dbacc_ref)
            for ref in (dgmix_ref, dbf_ref, dlng_ref, dlnb_ref, dws_ref):
                ref[...] = jnp.zeros_like(ref)

        z = z_ref[...]
        erf = lax.erf(z * INV_SQRT2)
        cdf = 0.5 * (1.0 + erf)
        zg = z * cdf
        zu = zg[:, :D_SGU]
        zv = zg[:, D_SGU:]
        xc = zv - _mean(zv)
        rln = lax.rsqrt(_mean(xc * xc) + EPS)
        zh = xc * rln
        lng = lng_ref[...]
        zvn = (zh * lng + lnb_ref[...]).astype(BF16)
        dsgu = dsgu_ref[...]
        lane_grp = lax.broadcasted_iota(jnp.int32, (SGU_CHUNK, D_SGU), 1) >> 6
        for ch in range(nch):
            rows = slice(ch * SGU_CHUNK, (ch + 1) * SGU_CHUNK)
            zc = zvn[rows, :]
            ds_c = dsgu[rows, :]
            mixed = _sgu_mix(wsm_ref, zc, lane_grp) + bsf_ref[...]
            dzu_ref[rows, :] = ds_c * mixed
            dmix = ds_c * zu[rows, :]
            dbacc_ref[...] += dmix
            dmb = dmix.astype(BF16)
            dzvn_ref[rows, :] = _sgu_mix(wsmt_ref, dmb, lane_grp)
            for g in range(N_GROUPS):
                dws_ref[g] += _dot_nt(jnp.where(lane_grp == g, dmb, jnp.zeros_like(dmb)), zc)
        dzvn = dzvn_ref[...]
        dlng_ref[...] += jnp.sum(dzvn * zh, axis=0, keepdims=True)
        dlnb_ref[...] += jnp.sum(dzvn, axis=0, keepdims=True)
        dzh = dzvn * lng
        dzv = rln * ((dzh - _mean(dzh)) - zh * _mean(dzh * zh))
        pdf = jnp.exp(-0.5 * (z * z)) * INV_SQRT_2PI
        dgelu = cdf + z * pdf
        dzb_ref[:, :D_SGU] = (dzu_ref[...] * dgelu[:, :D_SGU]).astype(BF16)
        dzb_ref[:, D_SGU:] = (dzv * dgelu[:, D_SGU:]).astype(BF16)

        dc = dc_ref[...]
        row = lax.broadcasted_iota(jnp.int32, (tm, tm), 0)
        col = lax.broadcasted_iota(jnp.int32, (tm, tm), 1)
        dlogf = _dot_exact((col >= row).astype(F32), dc) + carry_ref[...]
        carry_ref[...] = dlogf[0:1, :]
        dfl = dlogf * _sigmoid(-fl_ref[...])
        dbf_ref[...] += jnp.sum(dfl, axis=0, keepdims=True)
        dflb = dfl.astype(BF16)
        dflb_ref[...] = dflb

        dqkv_ref[:, :D_ATTN] = dq_ref[...]
        dqkv_ref[:, D_ATTN:2 * D_ATTN] = dk_ref[...]
        dqkv_ref[:, 2 * D_ATTN:] = dv_ref[...]
        dxn = _dot_nt(dqkv_ref[...], wqkv_ref[...]) + _dot_nt(dflb, wf_ref[...]) + _dot_nt(dzb_ref[...], wz_ref[...])

        xt = x_ref[...]
        gmix = gmix_ref[...]
        r = lax.rsqrt(_mean(xt * xt) + EPS)
        xh = xt * r
        xn_ref[...] = (xh * gmix).astype(BF16)
        dgmix_ref[...] += jnp.sum(dxn * xh, axis=0, keepdims=True)
        dyg = dxn * gmix
        dx_ref[...] = dx1_ref[...] + r * (dyg - xh * _mean(dyg * xh))

        @pl.when(step == nt - 1)
        def _():
            for g in range(N_GROUPS):
                dws_ref[g] = dws_ref[g] * mask_ref[...]
            dbs_ref[...] = _dot_exact(dbacc_ref[...], _group_indicator())

    rev = functools.partial(_rev_spec, nt=nt)
    return pl.pallas_call(
        body, name="bwd_a", grid=(nt,),
        in_specs=[rev(tm, D_MODEL), rev(tm, D_MODEL), rev(tm, 2 * D_SGU), rev(tm, LANES), rev(tm, D_SGU),
                  rev(tm, D_ATTN), rev(tm, D_ATTN), rev(tm, D_ATTN), rev(tm, LANES),
                  _const_spec((1, D_MODEL)), _const_spec((D_MODEL, 3 * D_ATTN)), _const_spec((D_MODEL, LANES)),
                  _const_spec((D_MODEL, 2 * D_SGU)), _const_spec((1, D_SGU)), _const_spec((1, D_SGU)),
                  _const_spec((N_GROUPS, SGU_CHUNK, SGU_CHUNK)), _const_spec((N_GROUPS, SGU_CHUNK, SGU_CHUNK)),
                  _const_spec((SGU_CHUNK, D_SGU)), _const_spec((SGU_CHUNK, SGU_CHUNK))],
        out_specs=[rev(tm, D_MODEL), rev(tm, D_MODEL), rev(tm, 3 * D_ATTN), rev(tm, LANES), rev(tm, 2 * D_SGU),
                   _const_spec((1, D_MODEL)), _const_spec((1, LANES)), _const_spec((1, D_SGU)), _const_spec((1, D_SGU)),
                   _const_spec((N_GROUPS, SGU_CHUNK, SGU_CHUNK)), _const_spec((SGU_CHUNK, LANES))],
        out_shape=[SDS((t, D_MODEL), F32), SDS((t, D_MODEL), BF16), SDS((t, 3 * D_ATTN), BF16), SDS((t, LANES), BF16),
                   SDS((t, 2 * D_SGU), BF16), SDS((1, D_MODEL), F32), SDS((1, LANES), F32), SDS((1, D_SGU), F32),
                   SDS((1, D_SGU), F32), SDS((N_GROUPS, SGU_CHUNK, SGU_CHUNK), F32), SDS((SGU_CHUNK, LANES), F32)],
        scratch_shapes=[pltpu.VMEM((1, LANES), F32), pltpu.VMEM((tm, D_SGU), F32), pltpu.VMEM((tm, D_SGU), F32),
                        pltpu.VMEM((SGU_CHUNK, D_SGU), F32)],
        compiler_params=_params("arbitrary"),
    )(dx1, x, z, fl, dsgu, dq, dk, dv, dc, gmix, wqkv, wf, wz, lng, lnb, wsm, wsm_t, bsf, mask)


def _pick(n, cap):
    if n <= cap:
        return n
    best = LANES
    for cand in range(LANES, cap + 1, LANES):
        if n % cand == 0:
            best = cand
    return best


def _tn_matmul(a, b, *, bt):
    t, k1 = a.shape
    n = b.shape[1]
    bk = _pick(k1, 1408)
    bn = _pick(n, 1408)
    nsteps = t // bt

    def body(a_ref, b_ref, o_ref):
        @pl.when(pl.program_id(2) == 0)
        def _():
            o_ref[...] = jnp.zeros_like(o_ref)

        o_ref[...] += _dot_tn(a_ref[...], b_ref[...])

    return pl.pallas_call(
        body, name=f"wgrad_{k1}x{n}", grid=(k1 // bk, n // bn, nsteps),
        in_specs=[pl.BlockSpec((bt, bk), lambda i, j, s: (s, i)), pl.BlockSpec((bt, bn), lambda i, j, s: (s, j))],
        out_specs=pl.BlockSpec((bk, bn), lambda i, j, s: (i, j)),
        out_shape=SDS((k1, n), F32),
        compiler_params=_params("arbitrary", "arbitrary", "arbitrary"),
    )(a, b)


def _adamw(parts, w, m, v, *, name):
    rows, cols = w.shape
    br = _pick_rows(rows, cols)
    c1 = 1.0 - ADAM_B1 ** ADAM_STEP
    c2 = 1.0 - ADAM_B2 ** ADAM_STEP

    def body(p_ref, w_ref, m_ref, v_ref, g_ref, d_ref, nm_ref, nv_ref):
        g = p_ref[0].astype(F32)
        for j in range(1, N_DEV):
            g = g + p_ref[j].astype(F32)
        g_ref[...] = g
        nm = ADAM_B1 * m_ref[...] + (1.0 - ADAM_B1) * g
        nv = ADAM_B2 * v_ref[...] + (1.0 - ADAM_B2) * (g * g)
        nm_ref[...] = nm
        nv_ref[...] = nv
        d_ref[...] = -ADAM_LR * ((nm / c1) / (jnp.sqrt(nv / c2) + ADAM_EPS) + ADAM_WD * w_ref[...])

    spec = pl.BlockSpec((br, cols), lambda i: (i, 0))
    return pl.pallas_call(
        body, name=name, grid=(rows // br,),
        in_specs=[pl.BlockSpec((N_DEV, br, cols), lambda i: (0, i, 0)), spec, spec, spec],
        out_specs=[spec] * 4, out_shape=[SDS((rows, cols), F32)] * 4,
        compiler_params=_params("arbitrary"),
    )(parts, w, m, v)


def _pick_rows(rows, cols):
    target = max(8, (256 * 1024) // cols)
    best = 8
    for cand in range(8, min(rows, target) + 1, 8):
        if rows % cand == 0:
            best = cand
    return best


def _peer(k):
    x, y, c = lax.axis_index("x"), lax.axis_index("y"), lax.axis_index("c")
    px = 1 - x if k & 4 else x
    py = 1 - y if k & 2 else y
    pc = 1 - c if k & 1 else c
    return (px, py, pc), 4 * px + 2 * py + pc


def _exchange_scratch(n):
    return [pltpu.SemaphoreType.DMA((N_DEV - 1, n)), pltpu.SemaphoreType.DMA((N_DEV - 1, n)),
            pltpu.SemaphoreType.DMA((n,))]


def _exchange_copies(ins, outs, sems, scatter, landing):
    send_sems, recv_sems, local_sems = sems
    me = 4 * lax.axis_index("x") + 2 * lax.axis_index("y") + lax.axis_index("c")
    copies = [pltpu.make_async_copy(ins[a].at[me] if scatter[a] else ins[a], outs[a].at[me], local_sems.at[a])
              for a in range(len(ins))]
    for k in range(1, N_DEV):
        peer, pidx = _peer(k)
        for a in range(len(ins)):
            copies.append(pltpu.make_async_remote_copy(
                src_ref=ins[a].at[pidx] if scatter[a] else ins[a], dst_ref=outs[a].at[pidx if landing else me],
                send_sem=send_sems.at[k - 1, a], recv_sem=recv_sems.at[k - 1, a], device_id=peer, device_id_type=MESH))
    return copies


def _exchange_start(ins, outs, sems, scatter):
    for cp in _exchange_copies(ins, outs, sems, scatter, landing=False):
        cp.start()


def _exchange_wait(ins, outs, sems, scatter):
    for cp in _exchange_copies(ins, outs, sems, scatter, landing=True):
        cp.wait()


def _exchange_shapes(arrs, scatter):
    return [SDS(a.shape if sc else (N_DEV,) + a.shape, a.dtype) for a, sc in zip(arrs, scatter)]


def _exchange(arrs, scatter, *, name):
    n = len(arrs)

    def body(*refs):
        ins, outs, sems = refs[:n], refs[n:2 * n], refs[2 * n:]
        _exchange_start(ins, outs, sems, scatter)
        _exchange_wait(ins, outs, sems, scatter)

    return pl.pallas_call(
        body, name=name, in_specs=[ANY] * n, out_specs=[ANY] * n, out_shape=_exchange_shapes(arrs, scatter),
        scratch_shapes=_exchange_scratch(n),
    )(*arrs)


def _fused_exchange(body, n_in, n_out, scatter, nsteps):
    n = len(scatter)

    def wrapped(*refs):
        ins, ex_in = refs[:n_in], refs[n_in:n_in + n]
        outs, ex_out = refs[n_in + n:n_in + n + n_out], refs[n_in + n + n_out:n_in + 2 * n + n_out]
        scratch, sems = refs[n_in + 2 * n + n_out:-3], refs[-3:]

        @pl.when(pl.program_id(0) == 0)
        def _():
            _exchange_start(ex_in, ex_out, sems, scatter)

        body(*ins, *outs, *scratch)

        @pl.when(pl.program_id(0) == nsteps - 1)
        def _():
            _exchange_wait(ex_in, ex_out, sems, scatter)

    return wrapped


def _rows_of(col8, tq):
    t = col8.shape[0]
    return col8[:, :N_HEADS].T.reshape(HEAD_PAIRS, 2, t // tq, tq)


def _cols_of(rows):
    cols = rows.reshape(N_HEADS, -1).T
    return jnp.pad(cols, ((0, 0), (0, LANES - N_HEADS)))


def _pairs_to_cols(pair_arr):
    t = pair_arr.shape[1]
    cols = pair_arr[:, :, :2].transpose(1, 0, 2).reshape(t, N_HEADS)
    return jnp.pad(cols, ((0, 0), (0, LANES - N_HEADS)))


def _step(x, tgt, small, shards):
    t = x.shape[0]
    tm, tq = _tiles(t)
    r = jnp.arange(SGU_CHUNK, dtype=jnp.int32) // SGU_BLOCK
    mask = (r[None, :] <= r[:, None]).astype(F32)
    layer_shards = lambda l: [shards[n][l] for n in BIG]

    layers = []
    saved = []
    gathered = _exchange(layer_shards(0), [False] * len(BIG), name="gather_weights")
    for l in range(DEPTH):
        w_in, w_out, w_gu, w_dn = (_assemble(g, n) for g, n in zip(gathered, BIG))
        wqkv = w_in[:, :3 * D_ATTN]
        wf = jnp.pad(w_in[:, 3 * D_ATTN:3 * D_ATTN + N_HEADS], ((0, 0), (0, LANES - N_HEADS)))
        wz = w_in[:, 3 * D_ATTN + N_HEADS:]
        bf = jnp.pad(small["b_f"][l], (0, LANES - N_HEADS))[None, :]
        wsm = (small["w_s"][l] * mask[None]).astype(BF16)
        wsm_t = jnp.swapaxes(wsm, 1, 2)
        bsf = jnp.repeat(small["b_s"][l].T, GROUP_DIM, axis=1)
        lw = dict(wqkv=wqkv, wf=wf, wz=wz, bf=bf, wsm=wsm, wsm_t=wsm_t, bsf=bsf, w_out=w_out, w_gu=w_gu, w_dn=w_dn,
                  gmix=small["mix_norm_g"][l][None, :], lng=small["sgu_ln_g"][l][None, :],
                  lnb=small["sgu_ln_b"][l][None, :], gout=small["out_norm_g"][l][None, :],
                  gffn=small["ffn_norm_g"][l][None, :])
        layers.append(lw)
        q, k, v, fl, z, sgu = _fwd_a(x, lw["gmix"], wqkv, wf, bf, wz, lw["lng"], lw["lnb"], wsm, bsf, tm=tm)
        o, lse = _attn_fwd(q, k, v, tq=tq)
        x1, x2, gu, *gathered = _fwd_b(x, o, sgu, lw["gout"], lw["gffn"], w_out, w_gu, w_dn, tm=tm,
                                       gather=layer_shards(l + 1) if l + 1 < DEPTH else ())
        saved.append(dict(x=x, q=q, k=k, v=v, fl=fl, z=z, sgu=sgu, o=o, lse=lse, x1=x1, gu=gu))
        x = x2

    dx, loss, dgfin = _loss_bwd(x, tgt, small["final_norm_g"][None, :], tm=tm)
    grads = {n: [None] * DEPTH for n in SMALL if n != "final_norm_g"}
    parts = [None] * DEPTH
    pending = ()
    for l in reversed(range(DEPTH)):
        lw, sv = layers[l], saved[l]
        (dx1, dx2b, a, dgu, xn2, mrg, dx1b, do, delta, dsgu, dgffn, dgout, *landed) = _bwd_b(
            dx, sv["x1"], sv["gu"], sv["o"], sv["sgu"], lw["gout"], lw["gffn"], lw["w_out"], lw["w_gu"], lw["w_dn"],
            tm=tm, scatter=pending)
        if pending:
            parts[l + 1] = landed
        big_grads = {"w_down": _tn_matmul(a, dx2b, bt=tq), "w_gate_up": _tn_matmul(xn2, dgu, bt=tq),
                     "w_out": _tn_matmul(mrg, dx1b, bt=tq)}
        lse_row = _rows_of(_pairs_to_cols(sv["lse"]), tq)
        dqt, dk, dv, dck, dcq = _attn_bwd(sv["q"], sv["k"], sv["v"], do, lse_row, _rows_of(delta, tq), tq=tq)
        dq = (dqt.transpose(1, 3, 0, 2).reshape(t, D_ATTN) * QK_SCALE).astype(BF16)
        (dx, xn, dqkv, dflb, dzb, dgmix, dbf, dlng, dlnb, dws, dbs) = _bwd_a(
            dx1, sv["x"], sv["z"], sv["fl"], dsgu, dq, dk, dv, _pairs_to_cols(dck) + _cols_of(dcq), lw["gmix"], lw["wqkv"], lw["wf"],
            lw["wz"], lw["lng"], lw["lnb"], lw["wsm"], lw["wsm_t"], lw["bsf"], mask, tm=tm)
        big_grads["w_in"] = jnp.concatenate(
            [_tn_matmul(xn, dqkv, bt=tq), _tn_matmul(xn, dflb, bt=tq)[:, :N_HEADS], _tn_matmul(xn, dzb, bt=tq)], axis=1)
        pending = [_split(big_grads[n], n).astype(BF16) for n in BIG]
        grads["mix_norm_g"][l] = dgmix[0]
        grads["b_f"][l] = dbf[0, :N_HEADS]
        grads["sgu_ln_g"][l] = dlng[0]
        grads["sgu_ln_b"][l] = dlnb[0]
        grads["w_s"][l] = dws
        grads["b_s"][l] = dbs[:, :N_GROUPS].T
        grads["out_norm_g"][l] = dgout[0]
        grads["ffn_norm_g"][l] = dgffn[0]
    grads = {n: jnp.stack(g) for n, g in grads.items()}
    grads["final_norm_g"] = dgfin[0]
    *parts[0], small_parts = _exchange(pending + [_pack(grads)], [True] * len(BIG) + [False], name="scatter_grads")
    big_parts = {}
    for a, n in enumerate(BIG):
        stacked = jnp.stack([parts[l][a] for l in range(DEPTH)], axis=1)
        big_parts[n] = stacked.reshape(N_DEV, -1, stacked.shape[-1])
    return loss[0, 0], dx, big_parts, small_parts


SMALL = ("mix_norm_g", "b_f", "sgu_ln_g", "sgu_ln_b", "w_s", "b_s", "out_norm_g", "ffn_norm_g", "final_norm_g")
BIG = ("w_in", "w_out", "w_gate_up", "w_down")
WEIGHTS = ("mix_norm_g", "w_in", "b_f", "sgu_ln_g", "sgu_ln_b", "w_s", "b_s", "out_norm_g", "w_out", "ffn_norm_g",
           "w_gate_up", "w_down", "final_norm_g")
SHARD_AXIS = {"w_in": 1, "w_out": 0, "w_gate_up": 1, "w_down": 0}


def _assemble(gathered, name):
    if SHARD_AXIS[name] == 0:
        return gathered.reshape(-1, gathered.shape[-1])
    return gathered.transpose(1, 0, 2).reshape(gathered.shape[1], -1)


def _split(full, name):
    rows, cols = full.shape
    if SHARD_AXIS[name] == 0:
        return full.reshape(N_DEV, rows // N_DEV, cols)
    return full.reshape(rows, N_DEV, cols // N_DEV).transpose(1, 0, 2)


def _pack(tree):
    flat = jnp.concatenate([tree[n].reshape(-1) for n in SMALL])
    pad = (-flat.shape[0]) % (8 * LANES)
    return jnp.pad(flat, (0, pad)).reshape(-1, LANES)


def _unpack(packed, like):
    flat = packed.reshape(-1)
    out, at = {}, 0
    for n in SMALL:
        size = like[n].size
        out[n] = flat[at:at + size].reshape(like[n].shape)
        at += size
    return out


def kernel(x, mix_norm_g, w_in, b_f, sgu_ln_g, sgu_ln_b, w_s, b_s, out_norm_g, w_out, ffn_norm_g, w_gate_up, w_down, final_norm_g, loss_target, m_mix_norm_g, m_w_in, m_b_f, m_sgu_ln_g, m_sgu_ln_b, m_w_s, m_b_s, m_out_norm_g, m_w_out, m_ffn_norm_g, m_w_gate_up, m_w_down, m_final_norm_g, v_mix_norm_g, v_w_in, v_b_f, v_sgu_ln_g, v_sgu_ln_b, v_w_s, v_b_s, v_out_norm_g, v_w_out, v_ffn_norm_g, v_w_gate_up, v_w_down, v_final_norm_g):
    w = dict(mix_norm_g=mix_norm_g, w_in=w_in, b_f=b_f, sgu_ln_g=sgu_ln_g, sgu_ln_b=sgu_ln_b, w_s=w_s, b_s=b_s,
             out_norm_g=out_norm_g, w_out=w_out, ffn_norm_g=ffn_norm_g, w_gate_up=w_gate_up, w_down=w_down,
             final_norm_g=final_norm_g)
    m = dict(mix_norm_g=m_mix_norm_g, w_in=m_w_in, b_f=m_b_f, sgu_ln_g=m_sgu_ln_g, sgu_ln_b=m_sgu_ln_b, w_s=m_w_s,
             b_s=m_b_s, out_norm_g=m_out_norm_g, w_out=m_w_out, ffn_norm_g=m_ffn_norm_g, w_gate_up=m_w_gate_up,
             w_down=m_w_down, final_norm_g=m_final_norm_g)
    v = dict(mix_norm_g=v_mix_norm_g, w_in=v_w_in, b_f=v_b_f, sgu_ln_g=v_sgu_ln_g, sgu_ln_b=v_sgu_ln_b, w_s=v_w_s,
             b_s=v_b_s, out_norm_g=v_out_norm_g, w_out=v_w_out, ffn_norm_g=v_ffn_norm_g, w_gate_up=v_w_gate_up,
             w_down=v_w_down, final_norm_g=v_final_norm_g)

    loss, dx, big_parts, small_parts = _step(x[0], loss_target[0], {n: w[n] for n in SMALL},
                                             {n: w[n].astype(BF16) for n in BIG})
    loss = lax.psum(loss, ("x", "y", "c"))

    g_out, d_out, m_out, v_out = {}, {}, {}, {}
    for n in BIG:
        shape = w[n].shape
        two_d = lambda a: a.reshape(-1, shape[-1])
        res = _adamw(big_parts[n], two_d(w[n]), two_d(m[n]), two_d(v[n]), name=f"adamw_{n}")
        g_out[n], d_out[n], m_out[n], v_out[n] = (r.reshape(shape) for r in res)
    res = _adamw(small_parts, _pack(w), _pack(m), _pack(v), name="adamw_small")
    for dst, packed in zip((g_out, d_out, m_out, v_out), res):
        dst.update(_unpack(packed, w))

    return (loss, dx[None], *[g_out[n] for n in WEIGHTS], *[d_out[n] for n in WEIGHTS],
            *[m_out[n] for n in WEIGHTS], *[v_out[n] for n in WEIGHTS])
```

```python
import functools
import math

import jax
import jax.numpy as jnp
from jax import lax
from jax.experimental import pallas as pl
from jax.experimental.pallas import tpu as pltpu

F32, BF16 = jnp.float32, jnp.bfloat16
HIGHEST = lax.Precision.HIGHEST
MESH = pl.DeviceIdType.MESH
ANY = pl.BlockSpec(memory_space=pl.ANY)
SDS = jax.ShapeDtypeStruct

N_DEV = 8
DEPTH = 4
D_MODEL = 1024
D_ATTN = 512
D_SGU = 512
N_HEADS = 8
HEAD_DIM = 64
HEAD_PAIRS = N_HEADS // 2
SGU_CHUNK = 128
SGU_BLOCK = 64
N_GROUPS = 8
GROUP_DIM = 64
D_FF = 2816
FF_CHUNK = 1408
N_FF_CHUNKS = D_FF // FF_CHUNK
D_IN = 3 * D_ATTN + N_HEADS + 2 * D_SGU
LANES = 128
EPS = 1e-6
QK_SCALE = HEAD_DIM ** -0.5
INV_SQRT2 = 1.0 / math.sqrt(2.0)
INV_SQRT_2PI = 1.0 / math.sqrt(2.0 * math.pi)
LOG2E = 1.0 / math.log(2.0)
LN2 = math.log(2.0)
ROW_CHUNK = 32
KX_ROWS = LANES + 16

ADAM_LR = 0.001
ADAM_B1 = 0.9
ADAM_B2 = 0.999
ADAM_EPS = 1e-08
ADAM_WD = 0.01
ADAM_STEP = 10

VMEM_LIMIT_BYTES = 56 * 1024 * 1024


def _params(*sem):
    return pltpu.CompilerParams(dimension_semantics=sem or None, vmem_limit_bytes=VMEM_LIMIT_BYTES)


def _dot(a, b):
    return jnp.dot(a, b, preferred_element_type=F32)


def _dot_nt(a, b):
    return lax.dot_general(a, b, (((1,), (1,)), ((), ())), preferred_element_type=F32)


def _dot_tn(a, b):
    return lax.dot_general(a, b, (((0,), (0,)), ((), ())), preferred_element_type=F32)


def _dot_exact(a, b):
    return jnp.dot(a, b, precision=HIGHEST, preferred_element_type=F32)


def _mean(v):
    return jnp.mean(v, axis=-1, keepdims=True)


def _sigmoid(v):
    return 1.0 / (1.0 + jnp.exp(-v))


def _row_spec(tm, n):
    return pl.BlockSpec((tm, n), lambda i: (i, 0))


def _rev_spec(tm, n, nt):
    return pl.BlockSpec((tm, n), lambda i: (nt - 1 - i, 0))


def _const_spec(shape):
    return pl.BlockSpec(shape, lambda i: (0,) * len(shape))


def _tiles(t):
    return min(256, t), min(512, t)


def _group_indicator():
    r = lax.broadcasted_iota(jnp.int32, (D_ATTN, LANES), 0)
    c = lax.broadcasted_iota(jnp.int32, (D_ATTN, LANES), 1)
    return ((r >> 6) == c).astype(F32)


def _sgu_mix(w_ref, zc, lane_grp):
    out = jnp.zeros((SGU_CHUNK, D_SGU), F32)
    for g in range(N_GROUPS):
        out = out + jnp.where(lane_grp == g, _dot(w_ref[g], zc), 0.0)
    return out


def _fwd_a(x, gmix, wqkv, wf, bf, wz, lng, lnb, wsm, bsf, *, tm, gather=()):
    t = x.shape[0]
    nt = t // tm
    nch = tm // SGU_CHUNK

    def body(x_ref, gmix_ref, wqkv_ref, wf_ref, bf_ref, wz_ref, lng_ref, lnb_ref, wsm_ref, bsf_ref,
             q_ref, qt_ref, k_ref, v_ref, fl_ref, z_ref, sgu_ref, carry_ref):
        @pl.when(pl.program_id(0) == 0)
        def _():
            carry_ref[...] = jnp.zeros_like(carry_ref)

        xt = x_ref[...]
        r = lax.rsqrt(_mean(xt * xt) + EPS)
        xn = ((xt * r) * gmix_ref[...]).astype(BF16)
        qkv = _dot(xn, wqkv_ref[...])

        fl = _dot(xn, wf_ref[...]) + bf_ref[...]
        fl_ref[...] = fl
        logf = jnp.minimum(fl, 0.0) - jnp.log1p(jnp.exp(-jnp.abs(fl)))
        row = lax.broadcasted_iota(jnp.int32, (tm, tm), 0)
        col = lax.broadcasted_iota(jnp.int32, (tm, tm), 1)
        c = _dot_exact((col <= row).astype(F32), logf) + carry_ref[...]
        carry_ref[...] = c[tm - 1:tm, :]

        c2 = c * LOG2E
        lane = lax.broadcasted_iota(jnp.int32, (tm, LANES), 1)
        for h in range(N_HEADS):
            pair, hh = divmod(h, 2)
            base = _aug_lane(hh)
            in_head = (lane >= hh * HEAD_DIM) & (lane < (hh + 1) * HEAD_DIM)
            col_h = jnp.sum(jnp.where(lane == h, c2, 0.0), axis=1, keepdims=True)
            hi = col_h.astype(BF16).astype(F32)
            mid = (col_h - hi).astype(BF16).astype(F32)
            lo = (col_h - hi) - mid
            split = jnp.where(lane == base, hi, jnp.where(lane == base + 1, mid, jnp.where(lane == base + 2, lo, 0.0)))
            split_k = jnp.where(lane == base + 3, hi, jnp.where(lane == base + 4, mid,
                                                                 jnp.where(lane == base + 5, lo, 0.0)))
            ones_q = ((lane >= base + 3) & (lane < base + 6)).astype(F32)
            ones_k = ((lane >= base) & (lane < base + 3)).astype(F32)
            blk = slice(pair * LANES, (pair + 1) * LANES)
            q_h = jnp.where(in_head, qkv[:, blk] * (QK_SCALE * LOG2E), split + ones_q)
            q_ref[h] = q_h.astype(BF16)
            qt_ref[h] = q_h.T.astype(BF16)
            k_ref[h] = jnp.where(in_head, qkv[:, D_ATTN:2 * D_ATTN][:, blk], ones_k - split_k).astype(BF16)
            v_ref[h] = jnp.where(in_head, qkv[:, 2 * D_ATTN:][:, blk], (lane == base).astype(F32)).astype(BF16)

        z = _dot(xn, wz_ref[...])
        z_ref[...] = z
        zg = 0.5 * z * (1.0 + lax.erf(z * INV_SQRT2))
        zu = zg[:, :D_SGU]
        zv = zg[:, D_SGU:]
        xc = zv - _mean(zv)
        zvn = ((xc * lax.rsqrt(_mean(xc * xc) + EPS)) * lng_ref[...] + lnb_ref[...]).astype(BF16)
        lane_grp = lax.broadcasted_iota(jnp.int32, (SGU_CHUNK, D_SGU), 1) >> 6
        for ch in range(nch):
            rows = slice(ch * SGU_CHUNK, (ch + 1) * SGU_CHUNK)
            mixed = _sgu_mix(wsm_ref, zvn[rows, :], lane_grp) + bsf_ref[...]
            sgu_ref[rows, :] = zu[rows, :] * mixed

    head_spec = pl.BlockSpec((N_HEADS, tm, LANES), lambda i: (0, i, 0))
    return _row_tile_call(
        body, "fwd_a", nt, (x, gmix, wqkv, wf, bf, wz, lng, lnb, wsm, bsf), exchange=gather, scatter=False,
        in_specs=[_row_spec(tm, D_MODEL), _const_spec((1, D_MODEL)), _const_spec((D_MODEL, 3 * D_ATTN)),
                  _const_spec((D_MODEL, LANES)), _const_spec((1, LANES)), _const_spec((D_MODEL, 2 * D_SGU)),
                  _const_spec((1, D_SGU)), _const_spec((1, D_SGU)), _const_spec((N_GROUPS, SGU_CHUNK, SGU_CHUNK)),
                  _const_spec((SGU_CHUNK, D_SGU))],
        out_specs=[head_spec, pl.BlockSpec((N_HEADS, LANES, tm), lambda i: (0, 0, i)), head_spec, head_spec,
                   _row_spec(tm, LANES), _row_spec(tm, 2 * D_SGU), _row_spec(tm, D_SGU)],
        out_shape=[SDS((N_HEADS, t, LANES), BF16), SDS((N_HEADS, LANES, t), BF16), SDS((N_HEADS, t, LANES), BF16),
                   SDS((N_HEADS, t, LANES), BF16), SDS((t, LANES), F32), SDS((t, 2 * D_SGU), F32),
                   SDS((t, D_SGU), F32)],
        scratch_shapes=[pltpu.VMEM((1, LANES), F32)])


def _aug_lane(hh):
    return (1 - hh) * HEAD_DIM


def _attn_fwd(qa, ka, va, *, tq):
    t = qa.shape[1]
    nq = t // tq
    nrc = tq // ROW_CHUNK

    def body(q_ref, k_hbm, v_hbm, o_ref, lse_ref, k_vm, v_vm, s_ref, p_ref, m_ref, a_ref, acc_ref):
        p = pl.program_id(0)
        i = pl.program_id(1)

        @pl.when(i == 0)
        def _():
            pltpu.sync_copy(k_hbm.at[pl.ds(2 * p, 2)], k_vm)
            pltpu.sync_copy(v_hbm.at[pl.ds(2 * p, 2)], v_vm)

        m_ref[...] = jnp.full(m_ref.shape, -jnp.inf, F32)
        acc_ref[...] = jnp.zeros_like(acc_ref)
        rowq = lax.broadcasted_iota(jnp.int32, (ROW_CHUNK, tq), 0)
        colk = lax.broadcasted_iota(jnp.int32, (ROW_CHUNK, tq), 1)

        def scores(j, slot):
            start = pl.multiple_of(j * tq, tq)
            for h in range(2):
                s_ref[slot, h] = _dot_nt(q_ref[h], k_vm[h, pl.ds(start, tq), :])

        def softmax(slot, masked):
            for h in range(2):
                for r in range(nrc):
                    rows = slice(r * ROW_CHUNK, (r + 1) * ROW_CHUNK)
                    sc = s_ref[slot, h, rows, :]
                    if masked:
                        sc = jnp.where(colk <= rowq + r * ROW_CHUNK, sc, -jnp.inf)
                    m_old = m_ref[h, rows, :]
                    m_new = jnp.maximum(m_old, jnp.max(sc, axis=1, keepdims=True))
                    p_ref[slot, h, rows, :] = jnp.exp2(sc - m_new).astype(BF16)
                    a_ref[slot, h, rows, :] = jnp.exp2(m_old - m_new)
                    m_ref[h, rows, :] = m_new

        def accumulate(j, slot):
            start = pl.multiple_of(j * tq, tq)
            for h in range(2):
                acc_ref[h] = acc_ref[h] * a_ref[slot, h] + _dot(p_ref[slot, h], v_vm[h, pl.ds(start, tq), :])

        scores(0, 0)

        @pl.when(i > 0)
        def _():
            scores(1, 1)
            softmax(0, False)

        def stage(j, slot):
            scores(j + 1, 1 - slot)
            softmax(slot, False)
            accumulate(j - 1, 1 - slot)

        def pair_body(n, carry):
            j = 1 + 2 * n
            stage(j, 1)
            stage(j + 1, 0)
            return carry

        rest = i - 1
        lax.fori_loop(0, rest // 2, pair_body, 0)

        @pl.when((rest > 0) & (lax.rem(rest, 2) == 1))
        def _():
            stage(i - 1, 1)

        @pl.when(lax.rem(i, 2) == 0)
        def _():
            softmax(0, True)

            @pl.when(i > 0)
            def _():
                accumulate(i - 1, 1)

            accumulate(i, 0)

        @pl.when(lax.rem(i, 2) == 1)
        def _():
            softmax(1, True)
            accumulate(i - 1, 0)
            accumulate(i, 1)

        lane = lax.broadcasted_iota(jnp.int32, (tq, LANES), 1)
        l_h = [jnp.sum(jnp.where(lane == _aug_lane(h), acc_ref[h], 0.0), axis=1, keepdims=True) for h in range(2)]
        o_ref[...] = jnp.where(lane < HEAD_DIM, acc_ref[0] / l_h[0], acc_ref[1] / l_h[1])
        lse = [m_ref[h] + jnp.log2(l_h[h]) for h in range(2)]
        lse_cols = jnp.where(lane == 0, lse[0], jnp.where(lane == 1, lse[1], 0.0))
        lse_ref[0] = lse_cols.T[:8, :]

    return pl.pallas_call(
        body, name="attn_fwd", grid=(HEAD_PAIRS, nq),
        in_specs=[pl.BlockSpec((2, tq, LANES), lambda p, i: (p, i, 0)), ANY, ANY],
        out_specs=[pl.BlockSpec((tq, LANES), lambda p, i: (i, p)),
                   pl.BlockSpec((1, 8, tq), lambda p, i: (p, 0, i))],
        out_shape=[SDS((t, D_ATTN), F32), SDS((HEAD_PAIRS, 8, t), F32)],
        scratch_shapes=[pltpu.VMEM((2, t, LANES), BF16), pltpu.VMEM((2, t, LANES), BF16),
                        pltpu.VMEM((2, 2, tq, tq), F32), pltpu.VMEM((2, 2, tq, tq), BF16),
                        pltpu.VMEM((2, tq, 1), F32), pltpu.VMEM((2, 2, tq, 1), F32), pltpu.VMEM((2, tq, LANES), F32)],
        compiler_params=_params("arbitrary", "arbitrary"),
    )(qa, ka, va)


def _attn_bwd(qt, ka, va, dot, lse_row, delta_row, *, tq):
    t = qt.shape[2]
    nq = t // tq
    nrc = tq // ROW_CHUNK

    def body(q_hbm, do_ref, k_ref, v_ref, lse_ref, dl_ref, dqt_ref, dk_ref, dv_ref, dck_ref, dcq_ref,
             q_vm, st_ref, dp_ref, pt_ref, ds_ref, dka_ref, dva_ref, vh_ref, kx_ref):
        p = pl.program_id(0)
        j = pl.program_id(1)

        @pl.when(j == 0)
        def _():
            pltpu.sync_copy(q_hbm.at[pl.ds(2 * p, 2)], q_vm)
            dqt_ref[...] = jnp.zeros_like(dqt_ref)
            dcq_ref[...] = jnp.zeros_like(dcq_ref)

        dka_ref[...] = jnp.zeros_like(dka_ref)
        dva_ref[...] = jnp.zeros_like(dva_ref)
        lane = lax.broadcasted_iota(jnp.int32, (tq, LANES), 1)
        in_head = (lane < HEAD_DIM, lane >= HEAD_DIM)
        for h in range(2):
            zero = jnp.zeros((tq, LANES), BF16)
            vh_ref[h] = jnp.where(in_head[h], v_ref[h], zero)
            kx_ref[h, :LANES, :] = jnp.where(in_head[h], k_ref[h], zero).astype(F32).T.astype(BF16)
            kx_ref[h, LANES:, :] = jnp.ones((KX_ROWS - LANES, tq), BF16)
        rowk = lax.broadcasted_iota(jnp.int32, (ROW_CHUNK, tq), 0)
        colq = lax.broadcasted_iota(jnp.int32, (ROW_CHUNK, tq), 1)

        def step(i, masked):
            cols = pl.ds(pl.multiple_of(i * tq, tq), tq)
            do_t = do_ref[0, :, cols]
            for h in range(2):
                st_ref[h] = _dot(k_ref[h], q_vm[h, :, cols])
                dp_ref[h] = _dot(vh_ref[h], do_t)
            for h in range(2):
                lse = lse_ref[0, h:h + 1, cols]
                delta = dl_ref[0, h:h + 1, cols]
                for r in range(nrc):
                    rows = slice(r * ROW_CHUNK, (r + 1) * ROW_CHUNK)
                    st = st_ref[h, rows, :]
                    if masked:
                        st = jnp.where(rowk + r * ROW_CHUNK <= colq, st, -jnp.inf)
                    pt = jnp.exp2(st - lse)
                    pt_ref[h, rows, :] = pt.astype(BF16)
                    ds_ref[h, rows, :] = (pt * (dp_ref[h, rows, :] - delta)).astype(BF16)
            dq_t = jnp.zeros((LANES, tq), F32)
            for h in range(2):
                dva_ref[h] += _dot_nt(do_t, pt_ref[h])
                dka_ref[h] += _dot_nt(q_vm[h, :, cols], ds_ref[h])
                ext = _dot(kx_ref[h], ds_ref[h])
                dq_t = dq_t + ext[:LANES, :]
                dcq_ref[0, h:h + 1, cols] += ext[LANES:LANES + 1, :]
            dqt_ref[0, :, cols] += dq_t

        def loop_body(i, carry):
            step(i, False)
            return carry

        step(j, True)
        lax.fori_loop(j + 1, nq, loop_body, 0)

        row = lax.broadcasted_iota(jnp.int32, (LANES, tq), 0)
        dk_ref[...] = (jnp.where(row < HEAD_DIM, dka_ref[0], dka_ref[1]) * LN2).T.astype(BF16)
        dv_ref[...] = jnp.where(row < HEAD_DIM, dva_ref[0], dva_ref[1]).T.astype(BF16)
        own = pl.ds(pl.multiple_of(j * tq, tq), tq)
        for h in range(2):
            at = _aug_lane(h) + 3
            dck_ref[0, h:h + 1, own] = -dka_ref[h, at:at + 1, :]

    rows = pl.BlockSpec((1, 2, t), lambda p, j: (p, 0, 0))
    tiles = pl.BlockSpec((1, LANES, t), lambda p, j: (p, 0, 0))
    return pl.pallas_call(
        body, name="attn_bwd", grid=(HEAD_PAIRS, nq),
        in_specs=[ANY, tiles,
                  pl.BlockSpec((2, tq, LANES), lambda p, j: (p, j, 0)),
                  pl.BlockSpec((2, tq, LANES), lambda p, j: (p, j, 0)), rows, rows],
        out_specs=[tiles,
                   pl.BlockSpec((tq, LANES), lambda p, j: (j, p)),
                   pl.BlockSpec((tq, LANES), lambda p, j: (j, p)), rows, rows],
        out_shape=[SDS((HEAD_PAIRS, LANES, t), F32), SDS((t, D_ATTN), BF16), SDS((t, D_ATTN), BF16),
                   SDS((HEAD_PAIRS, 2, t), F32), SDS((HEAD_PAIRS, 2, t), F32)],
        scratch_shapes=[pltpu.VMEM((2, LANES, t), BF16), pltpu.VMEM((2, tq, tq), F32),
                        pltpu.VMEM((2, tq, tq), F32), pltpu.VMEM((2, tq, tq), BF16), pltpu.VMEM((2, tq, tq), BF16),
                        pltpu.VMEM((2, LANES, tq), F32), pltpu.VMEM((2, LANES, tq), F32),
                        pltpu.VMEM((2, tq, LANES), BF16), pltpu.VMEM((2, KX_ROWS, tq), BF16)],
        compiler_params=_params("arbitrary", "arbitrary"),
    )(qt, dot, ka, va, lse_row, delta_row)

def _load_weights_once(pairs):
    @pl.when(pl.program_id(0) == 0)
    def _():
        for src, dst in pairs:
            pltpu.sync_copy(src, dst)


def _row_tile_call(body, name, nt, operands, *, in_specs, out_specs, out_shape, scratch_shapes, exchange, scatter):
    if not exchange:
        return pl.pallas_call(body, name=name, grid=(nt,), in_specs=in_specs, out_specs=out_specs,
                              out_shape=out_shape, scratch_shapes=scratch_shapes,
                              compiler_params=_params("arbitrary"))(*operands)
    flags = [scatter] * len(exchange)
    n = len(exchange)
    return pl.pallas_call(
        _fused_exchange(body, len(operands), len(out_shape), flags, nt),
        name=name + ("_scatter" if scatter else "_gather"), grid=(nt,),
        in_specs=list(in_specs) + [ANY] * n, out_specs=list(out_specs) + [ANY] * n,
        out_shape=list(out_shape) + _exchange_shapes(exchange, flags),
        scratch_shapes=list(scratch_shapes) + _exchange_scratch(n),
        compiler_params=_params("arbitrary"))(*operands, *exchange)


def _fwd_b(x, o, sgu, gout, gffn, w_out, w_gu, w_dn, *, tm, gather=()):
    t = x.shape[0]
    nt = t // tm

    def body(x_ref, o_ref, s_ref, gout_ref, gffn_ref, wout_hbm, wgu_hbm, wdn_hbm,
             x1_ref, x2_ref, gu_ref, wout, wgu, wdn):
        _load_weights_once(((wout_hbm, wout), (wgu_hbm, wgu), (wdn_hbm, wdn)))
        ov = o_ref[...]
        sv = s_ref[...]
        mo = ((ov * lax.rsqrt(_mean(ov * ov) + EPS)) * gout_ref[:, :D_ATTN]).astype(BF16)
        ms = ((sv * lax.rsqrt(_mean(sv * sv) + EPS)) * gout_ref[:, D_ATTN:]).astype(BF16)
        x1 = x_ref[...] + (_dot(mo, wout[:D_ATTN, :]) + _dot(ms, wout[D_ATTN:, :]))
        x1_ref[...] = x1
        xn2 = ((x1 * lax.rsqrt(_mean(x1 * x1) + EPS)) * gffn_ref[...]).astype(BF16)
        y = jnp.zeros((tm, D_MODEL), F32)
        for n in range(N_FF_CHUNKS):
            lo, hi = n * FF_CHUNK, (n + 1) * FF_CHUNK
            gate = _dot(xn2, wgu[:, lo:hi])
            up = _dot(xn2, wgu[:, D_FF + lo:D_FF + hi])
            gu_ref[:, lo:hi] = gate
            gu_ref[:, D_FF + lo:D_FF + hi] = up
            a = ((gate * _sigmoid(gate)) * up).astype(BF16)
            y = y + _dot(a, wdn[lo:hi, :])
        x2_ref[...] = x1 + y

    return _row_tile_call(
        body, "fwd_b", nt, (x, o, sgu, gout, gffn, w_out, w_gu, w_dn),
        in_specs=[_row_spec(tm, D_MODEL), _row_spec(tm, D_ATTN), _row_spec(tm, D_SGU),
                  _const_spec((1, D_MODEL)), _const_spec((1, D_MODEL)), ANY, ANY, ANY],
        out_specs=[_row_spec(tm, D_MODEL), _row_spec(tm, D_MODEL), _row_spec(tm, 2 * D_FF)],
        out_shape=[SDS((t, D_MODEL), F32), SDS((t, D_MODEL), F32), SDS((t, 2 * D_FF), F32)],
        scratch_shapes=[pltpu.VMEM((D_MODEL, D_MODEL), BF16), pltpu.VMEM((D_MODEL, 2 * D_FF), BF16),
                        pltpu.VMEM((D_FF, D_MODEL), BF16)],
        exchange=gather, scatter=False)


def _loss_bwd(x, tgt, gfin, *, tm):
    t = x.shape[0]
    nt = t // tm

    def body(x_ref, t_ref, g_ref, dx_ref, loss_ref, dg_ref):
        @pl.when(pl.program_id(0) == 0)
        def _():
            loss_ref[...] = jnp.zeros_like(loss_ref)
            dg_ref[...] = jnp.zeros_like(dg_ref)

        xt = x_ref[...]
        g = g_ref[...]
        r = lax.rsqrt(_mean(xt * xt) + EPS)
        xh = xt * r
        err = xh * g - t_ref[...]
        loss_ref[...] += 0.5 * jnp.sum(_mean(err * err), axis=0, keepdims=True)
        dy = err * (1.0 / D_MODEL)
        dg_ref[...] += jnp.sum(dy * xh, axis=0, keepdims=True)
        dyg = dy * g
        dx_ref[...] = r * (dyg - xh * _mean(dyg * xh))

    return pl.pallas_call(
        body, name="loss_bwd", grid=(nt,),
        in_specs=[_row_spec(tm, D_MODEL), _row_spec(tm, D_MODEL), _const_spec((1, D_MODEL))],
        out_specs=[_row_spec(tm, D_MODEL), _const_spec((1, 1)), _const_spec((1, D_MODEL))],
        out_shape=[SDS((t, D_MODEL), F32), SDS((1, 1), F32), SDS((1, D_MODEL), F32)],
        compiler_params=_params("arbitrary"),
    )(x, tgt, gfin)


def _bwd_b(dx2, x1, gu, o, sgu, gout, gffn, w_out, w_gu, w_dn, *, tm, scatter=()):
    t = dx2.shape[0]
    nt = t // tm

    def body(dx2_ref, x1_ref, gu_ref, o_ref, s_ref, gout_ref, gffn_ref, wout_hbm, wgu_hbm, wdn_hbm,
             dx1_ref, dx2b_ref, a_ref, dgu_ref, xn2_ref, mrg_ref, dx1b_ref, do_ref, dl_ref, dsgu_ref,
             dgffn_ref, dgout_ref, wout, wgu, wdn):
        _load_weights_once(((wout_hbm, wout), (wgu_hbm, wgu), (wdn_hbm, wdn)))

        @pl.when(pl.program_id(0) == 0)
        def _():
            dgffn_ref[...] = jnp.zeros_like(dgffn_ref)
            dgout_ref[...] = jnp.zeros_like(dgout_ref)

        dx2 = dx2_ref[...]
        dx2b = dx2.astype(BF16)
        dx2b_ref[...] = dx2b
        dxn2 = jnp.zeros((tm, D_MODEL), F32)
        for n in range(N_FF_CHUNKS):
            lo, hi = n * FF_CHUNK, (n + 1) * FF_CHUNK
            gate = gu_ref[:, lo:hi]
            up = gu_ref[:, D_FF + lo:D_FF + hi]
            sg = _sigmoid(gate)
            si = gate * sg
            a_ref[:, lo:hi] = (si * up).astype(BF16)
            d_a = _dot_nt(dx2b, wdn[lo:hi, :])
            dgb = ((d_a * up) * (sg * (1.0 + gate * (1.0 - sg)))).astype(BF16)
            dub = (d_a * si).astype(BF16)
            dgu_ref[:, lo:hi] = dgb
            dgu_ref[:, D_FF + lo:D_FF + hi] = dub
            dxn2 = dxn2 + (_dot_nt(dgb, wgu[:, lo:hi]) + _dot_nt(dub, wgu[:, D_FF + lo:D_FF + hi]))

        x1 = x1_ref[...]
        gffn = gffn_ref[...]
        r1 = lax.rsqrt(_mean(x1 * x1) + EPS)
        xh1 = x1 * r1
        xn2_ref[...] = (xh1 * gffn).astype(BF16)
        dgffn_ref[...] += jnp.sum(dxn2 * xh1, axis=0, keepdims=True)
        dyg = dxn2 * gffn
        dx1 = dx2 + r1 * (dyg - xh1 * _mean(dyg * xh1))
        dx1_ref[...] = dx1
        dx1b = dx1.astype(BF16)
        dx1b_ref[...] = dx1b

        ov = o_ref[...]
        sv = s_ref[...]
        go = gout_ref[:, :D_ATTN]
        gs = gout_ref[:, D_ATTN:]
        ro = lax.rsqrt(_mean(ov * ov) + EPS)
        rs = lax.rsqrt(_mean(sv * sv) + EPS)
        oh = ov * ro
        sh = sv * rs
        mrg_ref[:, :D_ATTN] = (oh * go).astype(BF16)
        mrg_ref[:, D_ATTN:] = (sh * gs).astype(BF16)
        dmo = _dot_nt(dx1b, wout[:D_ATTN, :])
        dms = _dot_nt(dx1b, wout[D_ATTN:, :])
        dgout_ref[:, :D_ATTN] += jnp.sum(dmo * oh, axis=0, keepdims=True)
        dgout_ref[:, D_ATTN:] += jnp.sum(dms * sh, axis=0, keepdims=True)
        dmog = dmo * go
        d_o = ro * (dmog - oh * _mean(dmog * oh))
        for pair in range(HEAD_PAIRS):
            do_ref[pair] = d_o[:, pair * LANES:(pair + 1) * LANES].T.astype(BF16)
        dl_ref[...] = _dot_exact(d_o * ov, _group_indicator()).T[:N_HEADS, :]
        dmsg = dms * gs
        dsgu_ref[...] = rs * (dmsg - sh * _mean(dmsg * sh))

    return _row_tile_call(
        body, "bwd_b", nt, (dx2, x1, gu, o, sgu, gout, gffn, w_out, w_gu, w_dn),
        in_specs=[_row_spec(tm, D_MODEL), _row_spec(tm, D_MODEL), _row_spec(tm, 2 * D_FF), _row_spec(tm, D_ATTN),
                  _row_spec(tm, D_SGU), _const_spec((1, D_MODEL)), _const_spec((1, D_MODEL)), ANY, ANY, ANY],
        out_specs=[_row_spec(tm, D_MODEL), _row_spec(tm, D_MODEL), _row_spec(tm, D_FF), _row_spec(tm, 2 * D_FF),
                   _row_spec(tm, D_MODEL), _row_spec(tm, D_MODEL), _row_spec(tm, D_MODEL),
                   pl.BlockSpec((HEAD_PAIRS, LANES, tm), lambda i: (0, 0, i)),
                   pl.BlockSpec((N_HEADS, tm), lambda i: (0, i)), _row_spec(tm, D_SGU),
                   _const_spec((1, D_MODEL)), _const_spec((1, D_MODEL))],
        out_shape=[SDS((t, D_MODEL), F32), SDS((t, D_MODEL), BF16), SDS((t, D_FF), BF16), SDS((t, 2 * D_FF), BF16),
                   SDS((t, D_MODEL), BF16), SDS((t, D_MODEL), BF16), SDS((t, D_MODEL), BF16),
                   SDS((HEAD_PAIRS, LANES, t), BF16), SDS((N_HEADS, t), F32), SDS((t, D_SGU), F32),
                   SDS((1, D_MODEL), F32), SDS((1, D_MODEL), F32)],
        scratch_shapes=[pltpu.VMEM((D_MODEL, D_MODEL), BF16), pltpu.VMEM((D_MODEL, 2 * D_FF), BF16),
                        pltpu.VMEM((D_FF, D_MODEL), BF16)],
        exchange=scatter, scatter=True)


def _bwd_a(dx1, x, z, fl, dsgu, dq, dk, dv, dc, gmix, wqkv, wf, wz, lng, lnb, wsm, wsm_t, bsf, mask, *, tm,
           scatter=()):
    t = x.shape[0]
    nt = t // tm
    nch = tm // SGU_CHUNK

    def body(dx1_ref, x_ref, z_ref, fl_ref, dsgu_ref, dq_ref, dk_ref, dv_ref, dc_ref, gmix_ref, wqkv_ref, wf_ref,
             wz_ref, lng_ref, lnb_ref, wsm_ref, wsmt_ref, bsf_ref, mask_ref,
             dx_ref, xn_ref, dqkv_ref, dflb_ref, dzb_ref, dgmix_ref, dbf_ref, dlng_ref, dlnb_ref, dws_ref, dbs_ref,
             carry_ref, dzvn_ref, dzu_ref, dbacc_ref):
        step = pl.program_id(0)

        @pl.when(step == 0)
        def _():
            carry_ref[...] = jnp.zeros_like(carry_ref)
            dbacc_ref[...] = jnp.zeros_like(dbacc_ref)
            for ref in (dgmix_ref, dbf_ref, dlng_ref, dlnb_ref, dws_ref):
                ref[...] = jnp.zeros_like(ref)

        z = z_ref[...]
        erf = lax.erf(z * INV_SQRT2)
        cdf = 0.5 * (1.0 + erf)
        zg = z * cdf
        zu = zg[:, :D_SGU]
        zv = zg[:, D_SGU:]
        xc = zv - _mean(zv)
        rln = lax.rsqrt(_mean(xc * xc) + EPS)
        zh = xc * rln
        lng = lng_ref[...]
        zvn = (zh * lng + lnb_ref[...]).astype(BF16)
        dsgu = dsgu_ref[...]
        lane_grp = lax.broadcasted_iota(jnp.int32, (SGU_CHUNK, D_SGU), 1) >> 6
        for ch in range(nch):
            rows = slice(ch * SGU_CHUNK, (ch + 1) * SGU_CHUNK)
            zc = zvn[rows, :]
            ds_c = dsgu[rows, :]
            mixed = _sgu_mix(wsm_ref, zc, lane_grp) + bsf_ref[...]
            dzu_ref[rows, :] = ds_c * mixed
            dmix = ds_c * zu[rows, :]
            dbacc_ref[...] += dmix
            dmb = dmix.astype(BF16)
            dzvn_ref[rows, :] = _sgu_mix(wsmt_ref, dmb, lane_grp)
            for g in range(N_GROUPS):
                dws_ref[g] += _dot_nt(jnp.where(lane_grp == g, dmb, jnp.zeros_like(dmb)), zc)
        dzvn = dzvn_ref[...]
        dlng_ref[...] += jnp.sum(dzvn * zh, axis=0, keepdims=True)
        dlnb_ref[...] += jnp.sum(dzvn, axis=0, keepdims=True)
        dzh = dzvn * lng
        dzv = rln * ((dzh - _mean(dzh)) - zh * _mean(dzh * zh))
        pdf = jnp.exp(-0.5 * (z * z)) * INV_SQRT_2PI
        dgelu = cdf + z * pdf
        dzb_ref[:, :D_SGU] = (dzu_ref[...] * dgelu[:, :D_SGU]).astype(BF16)
        dzb_ref[:, D_SGU:] = (dzv * dgelu[:, D_SGU:]).astype(BF16)

        dc = jnp.concatenate([dc_ref[...], jnp.zeros((LANES - N_HEADS, tm), F32)], axis=0).T
        row = lax.broadcasted_iota(jnp.int32, (tm, tm), 0)
        col = lax.broadcasted_iota(jnp.int32, (tm, tm), 1)
        dlogf = _dot_exact((col >= row).astype(F32), dc) + carry_ref[...]
        carry_ref[...] = dlogf[0:1, :]
        dfl = dlogf * _sigmoid(-fl_ref[...])
        dbf_ref[...] += jnp.sum(dfl, axis=0, keepdims=True)
        dflb = dfl.astype(BF16)
        dflb_ref[...] = dflb

        for pair in range(HEAD_PAIRS):
            dqkv_ref[:, pair * LANES:(pair + 1) * LANES] = (dq_ref[pair].T * QK_SCALE).astype(BF16)
        dqkv_ref[:, D_ATTN:2 * D_ATTN] = dk_ref[...]
        dqkv_ref[:, 2 * D_ATTN:] = dv_ref[...]
        dxn = _dot_nt(dqkv_ref[...], wqkv_ref[...]) + _dot_nt(dflb, wf_ref[...]) + _dot_nt(dzb_ref[...], wz_ref[...])

        xt = x_ref[...]
        gmix = gmix_ref[...]
        r = lax.rsqrt(_mean(xt * xt) + EPS)
        xh = xt * r
        xn_ref[...] = (xh * gmix).astype(BF16)
        dgmix_ref[...] += jnp.sum(dxn * xh, axis=0, keepdims=True)
        dyg = dxn * gmix
        dx_ref[...] = dx1_ref[...] + r * (dyg - xh * _mean(dyg * xh))

        @pl.when(step == nt - 1)
        def _():
            for g in range(N_GROUPS):
                dws_ref[g] = dws_ref[g] * mask_ref[...]
            dbs_ref[...] = _dot_exact(dbacc_ref[...], _group_indicator())

    rev = functools.partial(_rev_spec, nt=nt)
    return _row_tile_call(
        body, "bwd_a", nt, (dx1, x, z, fl, dsgu, dq, dk, dv, dc, gmix, wqkv, wf, wz, lng, lnb, wsm, wsm_t, bsf, mask),
        exchange=scatter, scatter=True,
        in_specs=[rev(tm, D_MODEL), rev(tm, D_MODEL), rev(tm, 2 * D_SGU), rev(tm, LANES), rev(tm, D_SGU),
                  pl.BlockSpec((HEAD_PAIRS, LANES, tm), lambda i: (0, 0, nt - 1 - i)), rev(tm, D_ATTN), rev(tm, D_ATTN),
                  pl.BlockSpec((N_HEADS, tm), lambda i: (0, nt - 1 - i)),
                  _const_spec((1, D_MODEL)), _const_spec((D_MODEL, 3 * D_ATTN)), _const_spec((D_MODEL, LANES)),
                  _const_spec((D_MODEL, 2 * D_SGU)), _const_spec((1, D_SGU)), _const_spec((1, D_SGU)),
                  _const_spec((N_GROUPS, SGU_CHUNK, SGU_CHUNK)), _const_spec((N_GROUPS, SGU_CHUNK, SGU_CHUNK)),
                  _const_spec((SGU_CHUNK, D_SGU)), _const_spec((SGU_CHUNK, SGU_CHUNK))],
        out_specs=[rev(tm, D_MODEL), rev(tm, D_MODEL), rev(tm, 3 * D_ATTN), rev(tm, LANES), rev(tm, 2 * D_SGU),
                   _const_spec((1, D_MODEL)), _const_spec((1, LANES)), _const_spec((1, D_SGU)), _const_spec((1, D_SGU)),
                   _const_spec((N_GROUPS, SGU_CHUNK, SGU_CHUNK)), _const_spec((SGU_CHUNK, LANES))],
        out_shape=[SDS((t, D_MODEL), F32), SDS((t, D_MODEL), BF16), SDS((t, 3 * D_ATTN), BF16), SDS((t, LANES), BF16),
                   SDS((t, 2 * D_SGU), BF16), SDS((1, D_MODEL), F32), SDS((1, LANES), F32), SDS((1, D_SGU), F32),
                   SDS((1, D_SGU), F32), SDS((N_GROUPS, SGU_CHUNK, SGU_CHUNK), F32), SDS((SGU_CHUNK, LANES), F32)],
        scratch_shapes=[pltpu.VMEM((1, LANES), F32), pltpu.VMEM((tm, D_SGU), F32), pltpu.VMEM((tm, D_SGU), F32),
                        pltpu.VMEM((SGU_CHUNK, D_SGU), F32)])


def _pick(n, cap):
    if n <= cap:
        return n
    best = LANES
    for cand in range(LANES, cap + 1, LANES):
        if n % cand == 0:
            best = cand
    return best


def _tn_matmul(a, b, *, bt):
    t, k1 = a.shape
    n = b.shape[1]
    bk = _pick(k1, 1408)
    bn = _pick(n, 1408)
    nsteps = t // bt

    def body(a_ref, b_ref, o_ref):
        @pl.when(pl.program_id(2) == 0)
        def _():
            o_ref[...] = jnp.zeros_like(o_ref)

        o_ref[...] += _dot_tn(a_ref[...], b_ref[...])

    return pl.pallas_call(
        body, name=f"wgrad_{k1}x{n}", grid=(k1 // bk, n // bn, nsteps),
        in_specs=[pl.BlockSpec((bt, bk), lambda i, j, s: (s, i)), pl.BlockSpec((bt, bn), lambda i, j, s: (s, j))],
        out_specs=pl.BlockSpec((bk, bn), lambda i, j, s: (i, j)),
        out_shape=SDS((k1, n), F32),
        compiler_params=_params("arbitrary", "arbitrary", "arbitrary"),
    )(a, b)


def _adamw(parts, w, m, v, *, name):
    rows, cols = w.shape
    br = _pick_rows(rows, cols)
    c1 = 1.0 - ADAM_B1 ** ADAM_STEP
    c2 = 1.0 - ADAM_B2 ** ADAM_STEP

    def body(p_ref, w_ref, m_ref, v_ref, g_ref, d_ref, nm_ref, nv_ref):
        g = p_ref[0].astype(F32)
        for j in range(1, N_DEV):
            g = g + p_ref[j].astype(F32)
        g_ref[...] = g
        nm = ADAM_B1 * m_ref[...] + (1.0 - ADAM_B1) * g
        nv = ADAM_B2 * v_ref[...] + (1.0 - ADAM_B2) * (g * g)
        nm_ref[...] = nm
        nv_ref[...] = nv
        d_ref[...] = -ADAM_LR * ((nm / c1) / (jnp.sqrt(nv / c2) + ADAM_EPS) + ADAM_WD * w_ref[...])

    spec = pl.BlockSpec((br, cols), lambda i: (i, 0))
    return pl.pallas_call(
        body, name=name, grid=(rows // br,),
        in_specs=[pl.BlockSpec((N_DEV, br, cols), lambda i: (0, i, 0)), spec, spec, spec],
        out_specs=[spec] * 4, out_shape=[SDS((rows, cols), F32)] * 4,
        compiler_params=_params("arbitrary"),
    )(parts, w, m, v)


def _pick_rows(rows, cols):
    target = max(8, (256 * 1024) // cols)
    best = 8
    for cand in range(8, min(rows, target) + 1, 8):
        if rows % cand == 0:
            best = cand
    return best


def _peer(k):
    x, y, c = lax.axis_index("x"), lax.axis_index("y"), lax.axis_index("c")
    px = 1 - x if k & 4 else x
    py = 1 - y if k & 2 else y
    pc = 1 - c if k & 1 else c
    return (px, py, pc), 4 * px + 2 * py + pc


def _exchange_scratch(n):
    return [pltpu.SemaphoreType.DMA((N_DEV - 1, n)), pltpu.SemaphoreType.DMA((N_DEV - 1, n)),
            pltpu.SemaphoreType.DMA((n,))]


def _exchange_copies(ins, outs, sems, scatter, landing):
    send_sems, recv_sems, local_sems = sems
    me = 4 * lax.axis_index("x") + 2 * lax.axis_index("y") + lax.axis_index("c")
    copies = [pltpu.make_async_copy(ins[a].at[me] if scatter[a] else ins[a], outs[a].at[me], local_sems.at[a])
              for a in range(len(ins))]
    for k in range(1, N_DEV):
        peer, pidx = _peer(k)
        for a in range(len(ins)):
            copies.append(pltpu.make_async_remote_copy(
                src_ref=ins[a].at[pidx] if scatter[a] else ins[a], dst_ref=outs[a].at[pidx if landing else me],
                send_sem=send_sems.at[k - 1, a], recv_sem=recv_sems.at[k - 1, a], device_id=peer, device_id_type=MESH))
    return copies


def _exchange_start(ins, outs, sems, scatter):
    for cp in _exchange_copies(ins, outs, sems, scatter, landing=False):
        cp.start()


def _exchange_wait(ins, outs, sems, scatter):
    for cp in _exchange_copies(ins, outs, sems, scatter, landing=True):
        cp.wait()


def _exchange_shapes(arrs, scatter):
    return [SDS(a.shape if sc else (N_DEV,) + a.shape, a.dtype) for a, sc in zip(arrs, scatter)]


def _exchange(arrs, scatter, *, name):
    n = len(arrs)

    def body(*refs):
        ins, outs, sems = refs[:n], refs[n:2 * n], refs[2 * n:]
        _exchange_start(ins, outs, sems, scatter)
        _exchange_wait(ins, outs, sems, scatter)

    return pl.pallas_call(
        body, name=name, in_specs=[ANY] * n, out_specs=[ANY] * n, out_shape=_exchange_shapes(arrs, scatter),
        scratch_shapes=_exchange_scratch(n),
    )(*arrs)


def _fused_exchange(body, n_in, n_out, scatter, nsteps):
    n = len(scatter)

    def wrapped(*refs):
        ins, ex_in = refs[:n_in], refs[n_in:n_in + n]
        outs, ex_out = refs[n_in + n:n_in + n + n_out], refs[n_in + n + n_out:n_in + 2 * n + n_out]
        scratch, sems = refs[n_in + 2 * n + n_out:-3], refs[-3:]

        @pl.when(pl.program_id(0) == 0)
        def _():
            _exchange_start(ex_in, ex_out, sems, scatter)

        body(*ins, *outs, *scratch)

        @pl.when(pl.program_id(0) == nsteps - 1)
        def _():
            _exchange_wait(ex_in, ex_out, sems, scatter)

    return wrapped


def _step(x, tgt, small, shards):
    t = x.shape[0]
    tm, tq = _tiles(t)
    r = jnp.arange(SGU_CHUNK, dtype=jnp.int32) // SGU_BLOCK
    mask = (r[None, :] <= r[:, None]).astype(F32)
    layer_shards = lambda l: [shards[n][l] for n in BIG]

    layers = []
    saved = []
    gathered = _exchange(layer_shards(0)[:1], [False], name="gather_weights")
    for l in range(DEPTH):
        w_in = _assemble(gathered[0], BIG[0])
        wqkv = w_in[:, :3 * D_ATTN]
        wf = jnp.pad(w_in[:, 3 * D_ATTN:3 * D_ATTN + N_HEADS], ((0, 0), (0, LANES - N_HEADS)))
        wz = w_in[:, 3 * D_ATTN + N_HEADS:]
        bf = jnp.pad(small["b_f"][l], (0, LANES - N_HEADS))[None, :]
        wsm = (small["w_s"][l] * mask[None]).astype(BF16)
        wsm_t = jnp.swapaxes(wsm, 1, 2)
        bsf = jnp.repeat(small["b_s"][l].T, GROUP_DIM, axis=1)
        lw = dict(wqkv=wqkv, wf=wf, wz=wz, bf=bf, wsm=wsm, wsm_t=wsm_t, bsf=bsf,
                  gmix=small["mix_norm_g"][l][None, :], lng=small["sgu_ln_g"][l][None, :],
                  lnb=small["sgu_ln_b"][l][None, :], gout=small["out_norm_g"][l][None, :],
                  gffn=small["ffn_norm_g"][l][None, :])
        layers.append(lw)
        q, q_t, k, v, fl, z, sgu, *late = _fwd_a(x, lw["gmix"], wqkv, wf, bf, wz, lw["lng"], lw["lnb"], wsm, bsf, tm=tm,
                                                 gather=layer_shards(0)[1:] if l == 0 else ())
        w_out, w_gu, w_dn = (_assemble(g, n) for g, n in zip(late if l == 0 else gathered[1:], BIG[1:]))
        lw.update(w_out=w_out, w_gu=w_gu, w_dn=w_dn)
        o, lse = _attn_fwd(q, k, v, tq=tq)
        x1, x2, gu, *gathered = _fwd_b(x, o, sgu, lw["gout"], lw["gffn"], w_out, w_gu, w_dn, tm=tm,
                                       gather=layer_shards(l + 1) if l + 1 < DEPTH else ())
        saved.append(dict(x=x, q_t=q_t, k=k, v=v, fl=fl, z=z, sgu=sgu, o=o, lse=lse[:, :2, :], x1=x1, gu=gu))
        x = x2

    dx, loss, dgfin = _loss_bwd(x, tgt, small["final_norm_g"][None, :], tm=tm)
    grads = {n: [None] * DEPTH for n in SMALL if n != "final_norm_g"}
    parts = [None] * DEPTH
    pending = ()
    for l in reversed(range(DEPTH)):
        lw, sv = layers[l], saved[l]
        (dx1, dx2b, a, dgu, xn2, mrg, dx1b, do_t, delta, dsgu, dgffn, dgout, *landed) = _bwd_b(
            dx, sv["x1"], sv["gu"], sv["o"], sv["sgu"], lw["gout"], lw["gffn"], lw["w_out"], lw["w_gu"], lw["w_dn"],
            tm=tm, scatter=pending)
        if pending:
            parts[l + 1] = landed
        big_grads = {"w_down": _tn_matmul(a, dx2b, bt=tq), "w_gate_up": _tn_matmul(xn2, dgu, bt=tq),
                     "w_out": _tn_matmul(mrg, dx1b, bt=tq)}
        dqt, dk, dv, dck, dcq = _attn_bwd(sv["q_t"], sv["k"], sv["v"], do_t, sv["lse"],
                                          delta.reshape(HEAD_PAIRS, 2, t), tq=tq)
        early = [_split(big_grads[n], n).astype(BF16) for n in BIG[1:]] if l == 0 else ()
        (dx, xn, dqkv, dflb, dzb, dgmix, dbf, dlng, dlnb, dws, dbs, *landed) = _bwd_a(
            dx1, sv["x"], sv["z"], sv["fl"], dsgu, dqt, dk, dv, (dck + dcq).reshape(N_HEADS, t), lw["gmix"],
            lw["wqkv"], lw["wf"], lw["wz"], lw["lng"], lw["lnb"], lw["wsm"], lw["wsm_t"], lw["bsf"], mask, tm=tm,
            scatter=early)
        big_grads["w_in"] = jnp.concatenate(
            [_tn_matmul(xn, dqkv, bt=tq), _tn_matmul(xn, dflb, bt=tq)[:, :N_HEADS], _tn_matmul(xn, dzb, bt=tq)], axis=1)
        pending = [_split(big_grads[n], n).astype(BF16) for n in (BIG[:1] if l == 0 else BIG)]
        grads["mix_norm_g"][l] = dgmix[0]
        grads["b_f"][l] = dbf[0, :N_HEADS]
        grads["sgu_ln_g"][l] = dlng[0]
        grads["sgu_ln_b"][l] = dlnb[0]
        grads["w_s"][l] = dws
        grads["b_s"][l] = dbs[:, :N_GROUPS].T
        grads["out_norm_g"][l] = dgout[0]
        grads["ffn_norm_g"][l] = dgffn[0]
    grads = {n: jnp.stack(g) for n, g in grads.items()}
    grads["final_norm_g"] = dgfin[0]
    first, small_parts = _exchange(pending + [_pack(grads)], [True, False], name="scatter_grads")
    parts[0] = [first] + landed
    big_parts = {}
    for a, n in enumerate(BIG):
        stacked = jnp.stack([parts[l][a] for l in range(DEPTH)], axis=1)
        big_parts[n] = stacked.reshape(N_DEV, -1, stacked.shape[-1])
    return loss[0, 0], dx, big_parts, small_parts


SMALL = ("mix_norm_g", "b_f", "sgu_ln_g", "sgu_ln_b", "w_s", "b_s", "out_norm_g", "ffn_norm_g", "final_norm_g")
BIG = ("w_in", "w_out", "w_gate_up", "w_down")
WEIGHTS = ("mix_norm_g", "w_in", "b_f", "sgu_ln_g", "sgu_ln_b", "w_s", "b_s", "out_norm_g", "w_out", "ffn_norm_g",
           "w_gate_up", "w_down", "final_norm_g")
SHARD_AXIS = {"w_in": 1, "w_out": 0, "w_gate_up": 1, "w_down": 0}


def _assemble(gathered, name):
    if SHARD_AXIS[name] == 0:
        return gathered.reshape(-1, gathered.shape[-1])
    return gathered.transpose(1, 0, 2).reshape(gathered.shape[1], -1)


def _split(full, name):
    rows, cols = full.shape
    if SHARD_AXIS[name] == 0:
        return full.reshape(N_DEV, rows // N_DEV, cols)
    return full.reshape(rows, N_DEV, cols // N_DEV).transpose(1, 0, 2)


def _pack(tree):
    flat = jnp.concatenate([tree[n].reshape(-1) for n in SMALL])
    pad = (-flat.shape[0]) % (8 * LANES)
    return jnp.pad(flat, (0, pad)).reshape(-1, LANES)


def _unpack(packed, like):
    flat = packed.reshape(-1)
    out, at = {}, 0
    for n in SMALL:
        size = like[n].size
        out[n] = flat[at:at + size].reshape(like[n].shape)
        at += size
    return out


def kernel(x, mix_norm_g, w_in, b_f, sgu_ln_g, sgu_ln_b, w_s, b_s, out_norm_g, w_out, ffn_norm_g, w_gate_up, w_down, final_norm_g, loss_target, m_mix_norm_g, m_w_in, m_b_f, m_sgu_ln_g, m_sgu_ln_b, m_w_s, m_b_s, m_out_norm_g, m_w_out, m_ffn_norm_g, m_w_gate_up, m_w_down, m_final_norm_g, v_mix_norm_g, v_w_in, v_b_f, v_sgu_ln_g, v_sgu_ln_b, v_w_s, v_b_s, v_out_norm_g, v_w_out, v_ffn_norm_g, v_w_gate_up, v_w_down, v_final_norm_g):
    w = dict(mix_norm_g=mix_norm_g, w_in=w_in, b_f=b_f, sgu_ln_g=sgu_ln_g, sgu_ln_b=sgu_ln_b, w_s=w_s, b_s=b_s,
             out_norm_g=out_norm_g, w_out=w_out, ffn_norm_g=ffn_norm_g, w_gate_up=w_gate_up, w_down=w_down,
             final_norm_g=final_norm_g)
    m = dict(mix_norm_g=m_mix_norm_g, w_in=m_w_in, b_f=m_b_f, sgu_ln_g=m_sgu_ln_g, sgu_ln_b=m_sgu_ln_b, w_s=m_w_s,
             b_s=m_b_s, out_norm_g=m_out_norm_g, w_out=m_w_out, ffn_norm_g=m_ffn_norm_g, w_gate_up=m_w_gate_up,
             w_down=m_w_down, final_norm_g=m_final_norm_g)
    v = dict(mix_norm_g=v_mix_norm_g, w_in=v_w_in, b_f=v_b_f, sgu_ln_g=v_sgu_ln_g, sgu_ln_b=v_sgu_ln_b, w_s=v_w_s,
             b_s=v_b_s, out_norm_g=v_out_norm_g, w_out=v_w_out, ffn_norm_g=v_ffn_norm_g, w_gate_up=v_w_gate_up,
             w_down=v_w_down, final_norm_g=v_final_norm_g)

    loss, dx, big_parts, small_parts = _step(x[0], loss_target[0], {n: w[n] for n in SMALL},
                                             {n: w[n].astype(BF16) for n in BIG})
    loss = lax.psum(loss, ("x", "y", "c"))

    g_out, d_out, m_out, v_out = {}, {}, {}, {}
    for n in BIG:
        shape = w[n].shape
        two_d = lambda a: a.reshape(-1, shape[-1])
        res = _adamw(big_parts[n], two_d(w[n]), two_d(m[n]), two_d(v[n]), name=f"adamw_{n}")
        g_out[n], d_out[n], m_out[n], v_out[n] = (r.reshape(shape) for r in res)
    res = _adamw(small_parts, _pack(w), _pack(m), _pack(v), name="adamw_small")
    for dst, packed in zip((g_out, d_out, m_out, v_out), res):
        dst.update(_unpack(packed, w))

    return (loss, dx[None], *[g_out[n] for n in WEIGHTS], *[d_out[n] for n in WEIGHTS],
            *[m_out[n] for n in WEIGHTS], *[v_out[n] for n in WEIGHTS])
```

```python
import functools
import math

import jax
import jax.numpy as jnp
from jax import lax
from jax.experimental import pallas as pl
from jax.experimental.pallas import tpu as pltpu

F32, BF16 = jnp.float32, jnp.bfloat16
HIGHEST = lax.Precision.HIGHEST
MESH = pl.DeviceIdType.MESH
ANY = pl.BlockSpec(memory_space=pl.ANY)
SDS = jax.ShapeDtypeStruct

N_DEV = 8
DEPTH = 4
D_MODEL = 1024
D_ATTN = 512
D_SGU = 512
N_HEADS = 8
HEAD_DIM = 64
HEAD_PAIRS = N_HEADS // 2
SGU_CHUNK = 128
SGU_BLOCK = 64
N_GROUPS = 8
GROUP_DIM = 64
D_FF = 2816
FF_CHUNK = 1408
N_FF_CHUNKS = D_FF // FF_CHUNK
D_IN = 3 * D_ATTN + N_HEADS + 2 * D_SGU
LANES = 128
EPS = 1e-6
QK_SCALE = HEAD_DIM ** -0.5
INV_SQRT2 = 1.0 / math.sqrt(2.0)
INV_SQRT_2PI = 1.0 / math.sqrt(2.0 * math.pi)
LOG2E = 1.0 / math.log(2.0)
LN2 = math.log(2.0)
ROW_CHUNK = 32
KX_ROWS = LANES + 16

ADAM_LR = 0.001
ADAM_B1 = 0.9
ADAM_B2 = 0.999
ADAM_EPS = 1e-08
ADAM_WD = 0.01
ADAM_STEP = 10

VMEM_LIMIT_BYTES = 56 * 1024 * 1024


def _params(*sem):
    return pltpu.CompilerParams(dimension_semantics=sem or None, vmem_limit_bytes=VMEM_LIMIT_BYTES)


def _dot(a, b):
    return jnp.dot(a, b, preferred_element_type=F32)


def _dot_nt(a, b):
    return lax.dot_general(a, b, (((1,), (1,)), ((), ())), preferred_element_type=F32)


def _dot_tn(a, b):
    return lax.dot_general(a, b, (((0,), (0,)), ((), ())), preferred_element_type=F32)


def _dot_exact(a, b):
    return jnp.dot(a, b, precision=HIGHEST, preferred_element_type=F32)


def _mean(v):
    return jnp.mean(v, axis=-1, keepdims=True)


def _sigmoid(v):
    return 1.0 / (1.0 + jnp.exp(-v))


def _row_spec(tm, n):
    return pl.BlockSpec((tm, n), lambda i: (i, 0))


def _rev_spec(tm, n, nt):
    return pl.BlockSpec((tm, n), lambda i: (nt - 1 - i, 0))


def _const_spec(shape):
    return pl.BlockSpec(shape, lambda i: (0,) * len(shape))


def _tiles(t):
    return min(256, t), min(512, t)


def _group_indicator():
    r = lax.broadcasted_iota(jnp.int32, (D_ATTN, LANES), 0)
    c = lax.broadcasted_iota(jnp.int32, (D_ATTN, LANES), 1)
    return ((r >> 6) == c).astype(F32)


def _sgu_mix(w_ref, zc, lane_grp):
    out = jnp.zeros((SGU_CHUNK, D_SGU), F32)
    for g in range(N_GROUPS):
        out = out + jnp.where(lane_grp == g, _dot(w_ref[g], zc), 0.0)
    return out


def _fwd_a(x, gmix, wqkv, wf, bf, wz, lng, lnb, wsm, bsf, *, tm, gather=()):
    t = x.shape[0]
    nt = t // tm
    nch = tm // SGU_CHUNK

    def body(x_ref, gmix_ref, wqkv_ref, wf_ref, bf_ref, wz_ref, lng_ref, lnb_ref, wsm_ref, bsf_ref,
             q_ref, k_ref, v_ref, fl_ref, z_ref, sgu_ref, carry_ref):
        @pl.when(pl.program_id(0) == 0)
        def _():
            carry_ref[...] = jnp.zeros_like(carry_ref)

        xt = x_ref[...]
        r = lax.rsqrt(_mean(xt * xt) + EPS)
        xn = ((xt * r) * gmix_ref[...]).astype(BF16)
        qkv = _dot(xn, wqkv_ref[...])

        fl = _dot(xn, wf_ref[...]) + bf_ref[...]
        fl_ref[...] = fl
        logf = jnp.minimum(fl, 0.0) - jnp.log1p(jnp.exp(-jnp.abs(fl)))
        row = lax.broadcasted_iota(jnp.int32, (tm, tm), 0)
        col = lax.broadcasted_iota(jnp.int32, (tm, tm), 1)
        c = _dot_exact((col <= row).astype(F32), logf) + carry_ref[...]
        carry_ref[...] = c[tm - 1:tm, :]

        c2 = c * LOG2E
        lane = lax.broadcasted_iota(jnp.int32, (tm, LANES), 1)
        for h in range(N_HEADS):
            pair, hh = divmod(h, 2)
            base = _aug_lane(hh)
            in_head = (lane >= hh * HEAD_DIM) & (lane < (hh + 1) * HEAD_DIM)
            col_h = jnp.sum(jnp.where(lane == h, c2, 0.0), axis=1, keepdims=True)
            hi = col_h.astype(BF16).astype(F32)
            mid = (col_h - hi).astype(BF16).astype(F32)
            lo = (col_h - hi) - mid
            split = jnp.where(lane == base, hi, jnp.where(lane == base + 1, mid, jnp.where(lane == base + 2, lo, 0.0)))
            split_k = jnp.where(lane == base + 3, hi, jnp.where(lane == base + 4, mid,
                                                                 jnp.where(lane == base + 5, lo, 0.0)))
            ones_q = ((lane >= base + 3) & (lane < base + 6)).astype(F32)
            ones_k = ((lane >= base) & (lane < base + 3)).astype(F32)
            blk = slice(pair * LANES, (pair + 1) * LANES)
            q_ref[h] = jnp.where(in_head, qkv[:, blk] * (QK_SCALE * LOG2E), split + ones_q).astype(BF16)
            k_ref[h] = jnp.where(in_head, qkv[:, D_ATTN:2 * D_ATTN][:, blk], ones_k - split_k).astype(BF16)
            v_ref[h] = jnp.where(in_head, qkv[:, 2 * D_ATTN:][:, blk], (lane == base).astype(F32)).astype(BF16)

        z = _dot(xn, wz_ref[...])
        z_ref[...] = z
        zg = 0.5 * z * (1.0 + lax.erf(z * INV_SQRT2))
        zu = zg[:, :D_SGU]
        zv = zg[:, D_SGU:]
        xc = zv - _mean(zv)
        zvn = ((xc * lax.rsqrt(_mean(xc * xc) + EPS)) * lng_ref[...] + lnb_ref[...]).astype(BF16)
        lane_grp = lax.broadcasted_iota(jnp.int32, (SGU_CHUNK, D_SGU), 1) >> 6
        for ch in range(nch):
            rows = slice(ch * SGU_CHUNK, (ch + 1) * SGU_CHUNK)
            mixed = _sgu_mix(wsm_ref, zvn[rows, :], lane_grp) + bsf_ref[...]
            sgu_ref[rows, :] = zu[rows, :] * mixed

    head_spec = pl.BlockSpec((N_HEADS, tm, LANES), lambda i: (0, i, 0))
    return _row_tile_call(
        body, "fwd_a", nt, (x, gmix, wqkv, wf, bf, wz, lng, lnb, wsm, bsf), exchange=gather, scatter=False,
        in_specs=[_row_spec(tm, D_MODEL), _const_spec((1, D_MODEL)), _const_spec((D_MODEL, 3 * D_ATTN)),
                  _const_spec((D_MODEL, LANES)), _const_spec((1, LANES)), _const_spec((D_MODEL, 2 * D_SGU)),
                  _const_spec((1, D_SGU)), _const_spec((1, D_SGU)), _const_spec((N_GROUPS, SGU_CHUNK, SGU_CHUNK)),
                  _const_spec((SGU_CHUNK, D_SGU))],
        out_specs=[head_spec, head_spec, head_spec, _row_spec(tm, LANES), _row_spec(tm, 2 * D_SGU),
                   _row_spec(tm, D_SGU)],
        out_shape=[SDS((N_HEADS, t, LANES), BF16)] * 3 + [SDS((t, LANES), F32), SDS((t, 2 * D_SGU), F32),
                                                          SDS((t, D_SGU), F32)],
        scratch_shapes=[pltpu.VMEM((1, LANES), F32)])


def _aug_lane(hh):
    return (1 - hh) * HEAD_DIM


def _attn_fwd(qa, ka, va, *, tq):
    t = qa.shape[1]
    nq = t // tq
    nrc = tq // ROW_CHUNK

    def body(q_ref, k_hbm, v_hbm, o_ref, lse_ref, k_vm, v_vm, s_ref, p_ref, m_ref, a_ref, acc_ref):
        p = pl.program_id(0)
        i = pl.program_id(1)

        @pl.when(i == 0)
        def _():
            pltpu.sync_copy(k_hbm.at[pl.ds(2 * p, 2)], k_vm)
            pltpu.sync_copy(v_hbm.at[pl.ds(2 * p, 2)], v_vm)

        m_ref[...] = jnp.full(m_ref.shape, -jnp.inf, F32)
        acc_ref[...] = jnp.zeros_like(acc_ref)
        rowq = lax.broadcasted_iota(jnp.int32, (ROW_CHUNK, tq), 0)
        colk = lax.broadcasted_iota(jnp.int32, (ROW_CHUNK, tq), 1)

        def scores(j, slot):
            start = pl.multiple_of(j * tq, tq)
            for h in range(2):
                s_ref[slot, h] = _dot_nt(q_ref[h], k_vm[h, pl.ds(start, tq), :])

        def softmax(slot, masked):
            for h in range(2):
                for r in range(nrc):
                    rows = slice(r * ROW_CHUNK, (r + 1) * ROW_CHUNK)
                    sc = s_ref[slot, h, rows, :]
                    if masked:
                        sc = jnp.where(colk <= rowq + r * ROW_CHUNK, sc, -jnp.inf)
                    m_old = m_ref[h, rows, :]
                    m_new = jnp.maximum(m_old, jnp.max(sc, axis=1, keepdims=True))
                    p_ref[slot, h, rows, :] = jnp.exp2(sc - m_new).astype(BF16)
                    a_ref[slot, h, rows, :] = jnp.exp2(m_old - m_new)
                    m_ref[h, rows, :] = m_new

        def accumulate(j, slot):
            start = pl.multiple_of(j * tq, tq)
            for h in range(2):
                acc_ref[h] = acc_ref[h] * a_ref[slot, h] + _dot(p_ref[slot, h], v_vm[h, pl.ds(start, tq), :])

        scores(0, 0)

        @pl.when(i > 0)
        def _():
            scores(1, 1)
            softmax(0, False)

        def stage(j, slot):
            scores(j + 1, 1 - slot)
            softmax(slot, False)
            accumulate(j - 1, 1 - slot)

        def pair_body(n, carry):
            j = 1 + 2 * n
            stage(j, 1)
            stage(j + 1, 0)
            return carry

        rest = i - 1
        lax.fori_loop(0, rest // 2, pair_body, 0)

        @pl.when((rest > 0) & (lax.rem(rest, 2) == 1))
        def _():
            stage(i - 1, 1)

        @pl.when(lax.rem(i, 2) == 0)
        def _():
            softmax(0, True)

            @pl.when(i > 0)
            def _():
                accumulate(i - 1, 1)

            accumulate(i, 0)

        @pl.when(lax.rem(i, 2) == 1)
        def _():
            softmax(1, True)
            accumulate(i - 1, 0)
            accumulate(i, 1)

        lane = lax.broadcasted_iota(jnp.int32, (tq, LANES), 1)
        l_h = [jnp.sum(jnp.where(lane == _aug_lane(h), acc_ref[h], 0.0), axis=1, keepdims=True) for h in range(2)]
        o_ref[...] = jnp.where(lane < HEAD_DIM, acc_ref[0] / l_h[0], acc_ref[1] / l_h[1])
        lse = [m_ref[h] + jnp.log2(l_h[h]) for h in range(2)]
        lse_cols = jnp.where(lane == 0, lse[0], jnp.where(lane == 1, lse[1], 0.0))
        lse_ref[0] = lse_cols.T[:8, :]

    return pl.pallas_call(
        body, name="attn_fwd", grid=(HEAD_PAIRS, nq),
        in_specs=[pl.BlockSpec((2, tq, LANES), lambda p, i: (p, i, 0)), ANY, ANY],
        out_specs=[pl.BlockSpec((tq, LANES), lambda p, i: (i, p)),
                   pl.BlockSpec((1, 8, tq), lambda p, i: (p, 0, i))],
        out_shape=[SDS((t, D_ATTN), F32), SDS((HEAD_PAIRS, 8, t), F32)],
        scratch_shapes=[pltpu.VMEM((2, t, LANES), BF16), pltpu.VMEM((2, t, LANES), BF16),
                        pltpu.VMEM((2, 2, tq, tq), F32), pltpu.VMEM((2, 2, tq, tq), BF16),
                        pltpu.VMEM((2, tq, 1), F32), pltpu.VMEM((2, 2, tq, 1), F32), pltpu.VMEM((2, tq, LANES), F32)],
        compiler_params=_params("arbitrary", "arbitrary"),
    )(qa, ka, va)


def _attn_bwd(qa, ka, va, do, lse_row, delta_row, *, tq):
    t = qa.shape[1]
    nq = t // tq
    nrc = tq // ROW_CHUNK

    def body(q_hbm, do_ref, k_ref, v_ref, lse_ref, dl_ref, dqt_ref, dk_ref, dv_ref, dck_ref, dcq_ref,
             q_vm, st_ref, dp_ref, pt_ref, ds_ref, dka_ref, dva_ref, vh_ref, kx_ref):
        p = pl.program_id(0)
        j = pl.program_id(1)

        @pl.when(j == 0)
        def _():
            pltpu.sync_copy(q_hbm.at[pl.ds(2 * p, 2)], q_vm)
            dqt_ref[...] = jnp.zeros_like(dqt_ref)
            dcq_ref[...] = jnp.zeros_like(dcq_ref)

        dka_ref[...] = jnp.zeros_like(dka_ref)
        dva_ref[...] = jnp.zeros_like(dva_ref)
        lane = lax.broadcasted_iota(jnp.int32, (tq, LANES), 1)
        in_head = (lane < HEAD_DIM, lane >= HEAD_DIM)
        for h in range(2):
            zero = jnp.zeros((tq, LANES), BF16)
            vh_ref[h] = jnp.where(in_head[h], v_ref[h], zero)
            kx_ref[h, :LANES, :] = jnp.where(in_head[h], k_ref[h], zero).astype(F32).T.astype(BF16)
            kx_ref[h, LANES:, :] = jnp.ones((KX_ROWS - LANES, tq), BF16)
        rowk = lax.broadcasted_iota(jnp.int32, (ROW_CHUNK, tq), 0)
        colq = lax.broadcasted_iota(jnp.int32, (ROW_CHUNK, tq), 1)

        def step(i, masked):
            start = pl.multiple_of(i * tq, tq)
            cols = pl.ds(start, tq)
            do2 = do_ref[pl.ds(start, tq), :]
            q_h = [q_vm[h, pl.ds(start, tq), :] for h in range(2)]
            for h in range(2):
                st_ref[h] = _dot_nt(k_ref[h], q_h[h])
                dp_ref[h] = _dot_nt(vh_ref[h], do2)
            for h in range(2):
                lse = lse_ref[0, h:h + 1, cols]
                delta = dl_ref[0, h:h + 1, cols]
                for r in range(nrc):
                    rows = slice(r * ROW_CHUNK, (r + 1) * ROW_CHUNK)
                    st = st_ref[h, rows, :]
                    if masked:
                        st = jnp.where(rowk + r * ROW_CHUNK <= colq, st, -jnp.inf)
                    pt = jnp.exp2(st - lse)
                    pt_ref[h, rows, :] = pt.astype(BF16)
                    ds_ref[h, rows, :] = (pt * (dp_ref[h, rows, :] - delta)).astype(BF16)
            dq_t = jnp.zeros((LANES, tq), F32)
            for h in range(2):
                dva_ref[h] += _dot(pt_ref[h], do2)
                dka_ref[h] += _dot(ds_ref[h], q_h[h])
                ext = _dot(kx_ref[h], ds_ref[h])
                dq_t = dq_t + ext[:LANES, :]
                dcq_ref[0, h:h + 1, cols] += ext[LANES:LANES + 1, :]
            dqt_ref[0, :, cols] += dq_t

        def loop_body(i, carry):
            step(i, False)
            return carry

        step(j, True)
        lax.fori_loop(j + 1, nq, loop_body, 0)

        dk_ref[...] = (jnp.where(in_head[0], dka_ref[0], dka_ref[1]) * LN2).astype(BF16)
        dv_ref[...] = jnp.where(in_head[0], dva_ref[0], dva_ref[1]).astype(BF16)
        own = pl.ds(pl.multiple_of(j * tq, tq), tq)
        for h in range(2):
            at = _aug_lane(h) + 3
            dck_ref[0, h:h + 1, own] = -dka_ref[h].T[at:at + 1, :]

    rows = pl.BlockSpec((1, 2, t), lambda p, j: (p, 0, 0))
    tiles = pl.BlockSpec((1, LANES, t), lambda p, j: (p, 0, 0))
    return pl.pallas_call(
        body, name="attn_bwd", grid=(HEAD_PAIRS, nq),
        in_specs=[ANY, pl.BlockSpec((t, LANES), lambda p, j: (0, p)),
                  pl.BlockSpec((2, tq, LANES), lambda p, j: (p, j, 0)),
                  pl.BlockSpec((2, tq, LANES), lambda p, j: (p, j, 0)), rows, rows],
        out_specs=[tiles,
                   pl.BlockSpec((tq, LANES), lambda p, j: (j, p)),
                   pl.BlockSpec((tq, LANES), lambda p, j: (j, p)), rows, rows],
        out_shape=[SDS((HEAD_PAIRS, LANES, t), F32), SDS((t, D_ATTN), BF16), SDS((t, D_ATTN), BF16),
                   SDS((HEAD_PAIRS, 2, t), F32), SDS((HEAD_PAIRS, 2, t), F32)],
        scratch_shapes=[pltpu.VMEM((2, t, LANES), BF16), pltpu.VMEM((2, tq, tq), F32),
                        pltpu.VMEM((2, tq, tq), F32), pltpu.VMEM((2, tq, tq), BF16), pltpu.VMEM((2, tq, tq), BF16),
                        pltpu.VMEM((2, tq, LANES), F32), pltpu.VMEM((2, tq, LANES), F32),
                        pltpu.VMEM((2, tq, LANES), BF16), pltpu.VMEM((2, KX_ROWS, tq), BF16)],
        compiler_params=_params("arbitrary", "arbitrary"),
    )(qa, do, ka, va, lse_row, delta_row)

def _load_weights_once(pairs):
    @pl.when(pl.program_id(0) == 0)
    def _():
        for src, dst in pairs:
            pltpu.sync_copy(src, dst)


def _row_tile_call(body, name, nt, operands, *, in_specs, out_specs, out_shape, scratch_shapes, exchange, scatter):
    if not exchange:
        return pl.pallas_call(body, name=name, grid=(nt,), in_specs=in_specs, out_specs=out_specs,
                              out_shape=out_shape, scratch_shapes=scratch_shapes,
                              compiler_params=_params("arbitrary"))(*operands)
    flags = [scatter] * len(exchange)
    n = len(exchange)
    return pl.pallas_call(
        _fused_exchange(body, len(operands), len(out_shape), flags, nt),
        name=name + ("_scatter" if scatter else "_gather"), grid=(nt,),
        in_specs=list(in_specs) + [ANY] * n, out_specs=list(out_specs) + [ANY] * n,
        out_shape=list(out_shape) + _exchange_shapes(exchange, flags),
        scratch_shapes=list(scratch_shapes) + _exchange_scratch(n),
        compiler_params=_params("arbitrary"))(*operands, *exchange)


def _fwd_b(x, o, sgu, gout, gffn, w_out, w_gu, w_dn, *, tm, gather=()):
    t = x.shape[0]
    nt = t // tm

    def body(x_ref, o_ref, s_ref, gout_ref, gffn_ref, wout_hbm, wgu_hbm, wdn_hbm,
             x1_ref, x2_ref, gu_ref, wout, wgu, wdn):
        _load_weights_once(((wout_hbm, wout), (wgu_hbm, wgu), (wdn_hbm, wdn)))
        ov = o_ref[...]
        sv = s_ref[...]
        mo = ((ov * lax.rsqrt(_mean(ov * ov) + EPS)) * gout_ref[:, :D_ATTN]).astype(BF16)
        ms = ((sv * lax.rsqrt(_mean(sv * sv) + EPS)) * gout_ref[:, D_ATTN:]).astype(BF16)
        x1 = x_ref[...] + (_dot(mo, wout[:D_ATTN, :]) + _dot(ms, wout[D_ATTN:, :]))
        x1_ref[...] = x1
        xn2 = ((x1 * lax.rsqrt(_mean(x1 * x1) + EPS)) * gffn_ref[...]).astype(BF16)
        y = jnp.zeros((tm, D_MODEL), F32)
        for n in range(N_FF_CHUNKS):
            lo, hi = n * FF_CHUNK, (n + 1) * FF_CHUNK
            gate = _dot(xn2, wgu[:, lo:hi])
            up = _dot(xn2, wgu[:, D_FF + lo:D_FF + hi])
            gu_ref[:, lo:hi] = gate
            gu_ref[:, D_FF + lo:D_FF + hi] = up
            a = ((gate * _sigmoid(gate)) * up).astype(BF16)
            y = y + _dot(a, wdn[lo:hi, :])
        x2_ref[...] = x1 + y

    return _row_tile_call(
        body, "fwd_b", nt, (x, o, sgu, gout, gffn, w_out, w_gu, w_dn),
        in_specs=[_row_spec(tm, D_MODEL), _row_spec(tm, D_ATTN), _row_spec(tm, D_SGU),
                  _const_spec((1, D_MODEL)), _const_spec((1, D_MODEL)), ANY, ANY, ANY],
        out_specs=[_row_spec(tm, D_MODEL), _row_spec(tm, D_MODEL), _row_spec(tm, 2 * D_FF)],
        out_shape=[SDS((t, D_MODEL), F32), SDS((t, D_MODEL), F32), SDS((t, 2 * D_FF), F32)],
        scratch_shapes=[pltpu.VMEM((D_MODEL, D_MODEL), BF16), pltpu.VMEM((D_MODEL, 2 * D_FF), BF16),
                        pltpu.VMEM((D_FF, D_MODEL), BF16)],
        exchange=gather, scatter=False)


def _loss_bwd(x, tgt, gfin, *, tm):
    t = x.shape[0]
    nt = t // tm

    def body(x_ref, t_ref, g_ref, dx_ref, loss_ref, dg_ref):
        @pl.when(pl.program_id(0) == 0)
        def _():
            loss_ref[...] = jnp.zeros_like(loss_ref)
            dg_ref[...] = jnp.zeros_like(dg_ref)

        xt = x_ref[...]
        g = g_ref[...]
        r = lax.rsqrt(_mean(xt * xt) + EPS)
        xh = xt * r
        err = xh * g - t_ref[...]
        loss_ref[...] += 0.5 * jnp.sum(_mean(err * err), axis=0, keepdims=True)
        dy = err * (1.0 / D_MODEL)
        dg_ref[...] += jnp.sum(dy * xh, axis=0, keepdims=True)
        dyg = dy * g
        dx_ref[...] = r * (dyg - xh * _mean(dyg * xh))

    return pl.pallas_call(
        body, name="loss_bwd", grid=(nt,),
        in_specs=[_row_spec(tm, D_MODEL), _row_spec(tm, D_MODEL), _const_spec((1, D_MODEL))],
        out_specs=[_row_spec(tm, D_MODEL), _const_spec((1, 1)), _const_spec((1, D_MODEL))],
        out_shape=[SDS((t, D_MODEL), F32), SDS((1, 1), F32), SDS((1, D_MODEL), F32)],
        compiler_params=_params("arbitrary"),
    )(x, tgt, gfin)


def _bwd_b(dx2, x1, gu, o, sgu, gout, gffn, w_out, w_gu, w_dn, *, tm, scatter=()):
    t = dx2.shape[0]
    nt = t // tm

    def body(dx2_ref, x1_ref, gu_ref, o_ref, s_ref, gout_ref, gffn_ref, wout_hbm, wgu_hbm, wdn_hbm,
             dx1_ref, dx2b_ref, a_ref, dgu_ref, xn2_ref, mrg_ref, dx1b_ref, do_ref, dl_ref, dsgu_ref,
             dgffn_ref, dgout_ref, wout, wgu, wdn):
        _load_weights_once(((wout_hbm, wout), (wgu_hbm, wgu), (wdn_hbm, wdn)))

        @pl.when(pl.program_id(0) == 0)
        def _():
            dgffn_ref[...] = jnp.zeros_like(dgffn_ref)
            dgout_ref[...] = jnp.zeros_like(dgout_ref)

        dx2 = dx2_ref[...]
        dx2b = dx2.astype(BF16)
        dx2b_ref[...] = dx2b
        dxn2 = jnp.zeros((tm, D_MODEL), F32)
        for n in range(N_FF_CHUNKS):
            lo, hi = n * FF_CHUNK, (n + 1) * FF_CHUNK
            gate = gu_ref[:, lo:hi]
            up = gu_ref[:, D_FF + lo:D_FF + hi]
            sg = _sigmoid(gate)
            si = gate * sg
            a_ref[:, lo:hi] = (si * up).astype(BF16)
            d_a = _dot_nt(dx2b, wdn[lo:hi, :])
            dgb = ((d_a * up) * (sg * (1.0 + gate * (1.0 - sg)))).astype(BF16)
            dub = (d_a * si).astype(BF16)
            dgu_ref[:, lo:hi] = dgb
            dgu_ref[:, D_FF + lo:D_FF + hi] = dub
            dxn2 = dxn2 + (_dot_nt(dgb, wgu[:, lo:hi]) + _dot_nt(dub, wgu[:, D_FF + lo:D_FF + hi]))

        x1 = x1_ref[...]
        gffn = gffn_ref[...]
        r1 = lax.rsqrt(_mean(x1 * x1) + EPS)
        xh1 = x1 * r1
        xn2_ref[...] = (xh1 * gffn).astype(BF16)
        dgffn_ref[...] += jnp.sum(dxn2 * xh1, axis=0, keepdims=True)
        dyg = dxn2 * gffn
        dx1 = dx2 + r1 * (dyg - xh1 * _mean(dyg * xh1))
        dx1_ref[...] = dx1
        dx1b = dx1.astype(BF16)
        dx1b_ref[...] = dx1b

        ov = o_ref[...]
        sv = s_ref[...]
        go = gout_ref[:, :D_ATTN]
        gs = gout_ref[:, D_ATTN:]
        ro = lax.rsqrt(_mean(ov * ov) + EPS)
        rs = lax.rsqrt(_mean(sv * sv) + EPS)
        oh = ov * ro
        sh = sv * rs
        mrg_ref[:, :D_ATTN] = (oh * go).astype(BF16)
        mrg_ref[:, D_ATTN:] = (sh * gs).astype(BF16)
        dmo = _dot_nt(dx1b, wout[:D_ATTN, :])
        dms = _dot_nt(dx1b, wout[D_ATTN:, :])
        dgout_ref[:, :D_ATTN] += jnp.sum(dmo * oh, axis=0, keepdims=True)
        dgout_ref[:, D_ATTN:] += jnp.sum(dms * sh, axis=0, keepdims=True)
        dmog = dmo * go
        d_o = ro * (dmog - oh * _mean(dmog * oh))
        do_ref[...] = d_o.astype(BF16)
        dl_ref[...] = _dot_exact(d_o * ov, _group_indicator()).T[:N_HEADS, :]
        dmsg = dms * gs
        dsgu_ref[...] = rs * (dmsg - sh * _mean(dmsg * sh))

    return _row_tile_call(
        body, "bwd_b", nt, (dx2, x1, gu, o, sgu, gout, gffn, w_out, w_gu, w_dn),
        in_specs=[_row_spec(tm, D_MODEL), _row_spec(tm, D_MODEL), _row_spec(tm, 2 * D_FF), _row_spec(tm, D_ATTN),
                  _row_spec(tm, D_SGU), _const_spec((1, D_MODEL)), _const_spec((1, D_MODEL)), ANY, ANY, ANY],
        out_specs=[_row_spec(tm, D_MODEL), _row_spec(tm, D_MODEL), _row_spec(tm, D_FF), _row_spec(tm, 2 * D_FF),
                   _row_spec(tm, D_MODEL), _row_spec(tm, D_MODEL), _row_spec(tm, D_MODEL), _row_spec(tm, D_ATTN),
                   pl.BlockSpec((N_HEADS, tm), lambda i: (0, i)), _row_spec(tm, D_SGU),
                   _const_spec((1, D_MODEL)), _const_spec((1, D_MODEL))],
        out_shape=[SDS((t, D_MODEL), F32), SDS((t, D_MODEL), BF16), SDS((t, D_FF), BF16), SDS((t, 2 * D_FF), BF16),
                   SDS((t, D_MODEL), BF16), SDS((t, D_MODEL), BF16), SDS((t, D_MODEL), BF16),
                   SDS((t, D_ATTN), BF16), SDS((N_HEADS, t), F32), SDS((t, D_SGU), F32),
                   SDS((1, D_MODEL), F32), SDS((1, D_MODEL), F32)],
        scratch_shapes=[pltpu.VMEM((D_MODEL, D_MODEL), BF16), pltpu.VMEM((D_MODEL, 2 * D_FF), BF16),
                        pltpu.VMEM((D_FF, D_MODEL), BF16)],
        exchange=scatter, scatter=True)


def _bwd_a(dx1, x, z, fl, dsgu, dq, dk, dv, dc, gmix, wqkv, wf, wz, lng, lnb, wsm, wsm_t, bsf, mask, *, tm,
           scatter=()):
    t = x.shape[0]
    nt = t // tm
    nch = tm // SGU_CHUNK

    def body(dx1_ref, x_ref, z_ref, fl_ref, dsgu_ref, dq_ref, dk_ref, dv_ref, dc_ref, gmix_ref, wqkv_ref, wf_ref,
             wz_ref, lng_ref, lnb_ref, wsm_ref, wsmt_ref, bsf_ref, mask_ref,
             dx_ref, xn_ref, dqkv_ref, dflb_ref, dzb_ref, dgmix_ref, dbf_ref, dlng_ref, dlnb_ref, dws_ref, dbs_ref,
             carry_ref, dzvn_ref, dzu_ref, dbacc_ref):
        step = pl.program_id(0)

        @pl.when(step == 0)
        def _():
            carry_ref[...] = jnp.zeros_like(carry_ref)
            dbacc_ref[...] = jnp.zeros_like(dbacc_ref)
            for ref in (dgmix_ref, dbf_ref, dlng_ref, dlnb_ref, dws_ref):
                ref[...] = jnp.zeros_like(ref)

        z = z_ref[...]
        erf = lax.erf(z * INV_SQRT2)
        cdf = 0.5 * (1.0 + erf)
        zg = z * cdf
        zu = zg[:, :D_SGU]
        zv = zg[:, D_SGU:]
        xc = zv - _mean(zv)
        rln = lax.rsqrt(_mean(xc * xc) + EPS)
        zh = xc * rln
        lng = lng_ref[...]
        zvn = (zh * lng + lnb_ref[...]).astype(BF16)
        dsgu = dsgu_ref[...]
        lane_grp = lax.broadcasted_iota(jnp.int32, (SGU_CHUNK, D_SGU), 1) >> 6
        for ch in range(nch):
            rows = slice(ch * SGU_CHUNK, (ch + 1) * SGU_CHUNK)
            zc = zvn[rows, :]
            ds_c = dsgu[rows, :]
            mixed = _sgu_mix(wsm_ref, zc, lane_grp) + bsf_ref[...]
            dzu_ref[rows, :] = ds_c * mixed
            dmix = ds_c * zu[rows, :]
            dbacc_ref[...] += dmix
            dmb = dmix.astype(BF16)
            dzvn_ref[rows, :] = _sgu_mix(wsmt_ref, dmb, lane_grp)
            for g in range(N_GROUPS):
                dws_ref[g] += _dot_nt(jnp.where(lane_grp == g, dmb, jnp.zeros_like(dmb)), zc)
        dzvn = dzvn_ref[...]
        dlng_ref[...] += jnp.sum(dzvn * zh, axis=0, keepdims=True)
        dlnb_ref[...] += jnp.sum(dzvn, axis=0, keepdims=True)
        dzh = dzvn * lng
        dzv = rln * ((dzh - _mean(dzh)) - zh * _mean(dzh * zh))
        pdf = jnp.exp(-0.5 * (z * z)) * INV_SQRT_2PI
        dgelu = cdf + z * pdf
        dzb_ref[:, :D_SGU] = (dzu_ref[...] * dgelu[:, :D_SGU]).astype(BF16)
        dzb_ref[:, D_SGU:] = (dzv * dgelu[:, D_SGU:]).astype(BF16)

        dc = jnp.concatenate([dc_ref[...], jnp.zeros((LANES - N_HEADS, tm), F32)], axis=0).T
        row = lax.broadcasted_iota(jnp.int32, (tm, tm), 0)
        col = lax.broadcasted_iota(jnp.int32, (tm, tm), 1)
        dlogf = _dot_exact((col >= row).astype(F32), dc) + carry_ref[...]
        carry_ref[...] = dlogf[0:1, :]
        dfl = dlogf * _sigmoid(-fl_ref[...])
        dbf_ref[...] += jnp.sum(dfl, axis=0, keepdims=True)
        dflb = dfl.astype(BF16)
        dflb_ref[...] = dflb

        for pair in range(HEAD_PAIRS):
            dqkv_ref[:, pair * LANES:(pair + 1) * LANES] = (dq_ref[pair].T * QK_SCALE).astype(BF16)
        dqkv_ref[:, D_ATTN:2 * D_ATTN] = dk_ref[...]
        dqkv_ref[:, 2 * D_ATTN:] = dv_ref[...]
        dxn = _dot_nt(dqkv_ref[...], wqkv_ref[...]) + _dot_nt(dflb, wf_ref[...]) + _dot_nt(dzb_ref[...], wz_ref[...])

        xt = x_ref[...]
        gmix = gmix_ref[...]
        r = lax.rsqrt(_mean(xt * xt) + EPS)
        xh = xt * r
        xn_ref[...] = (xh * gmix).astype(BF16)
        dgmix_ref[...] += jnp.sum(dxn * xh, axis=0, keepdims=True)
        dyg = dxn * gmix
        dx_ref[...] = dx1_ref[...] + r * (dyg - xh * _mean(dyg * xh))

        @pl.when(step == nt - 1)
        def _():
            for g in range(N_GROUPS):
                dws_ref[g] = dws_ref[g] * mask_ref[...]
            dbs_ref[...] = _dot_exact(dbacc_ref[...], _group_indicator())

    rev = functools.partial(_rev_spec, nt=nt)
    return _row_tile_call(
        body, "bwd_a", nt, (dx1, x, z, fl, dsgu, dq, dk, dv, dc, gmix, wqkv, wf, wz, lng, lnb, wsm, wsm_t, bsf, mask),
        exchange=scatter, scatter=True,
        in_specs=[rev(tm, D_MODEL), rev(tm, D_MODEL), rev(tm, 2 * D_SGU), rev(tm, LANES), rev(tm, D_SGU),
                  pl.BlockSpec((HEAD_PAIRS, LANES, tm), lambda i: (0, 0, nt - 1 - i)), rev(tm, D_ATTN), rev(tm, D_ATTN),
                  pl.BlockSpec((N_HEADS, tm), lambda i: (0, nt - 1 - i)),
                  _const_spec((1, D_MODEL)), _const_spec((D_MODEL, 3 * D_ATTN)), _const_spec((D_MODEL, LANES)),
                  _const_spec((D_MODEL, 2 * D_SGU)), _const_spec((1, D_SGU)), _const_spec((1, D_SGU)),
                  _const_spec((N_GROUPS, SGU_CHUNK, SGU_CHUNK)), _const_spec((N_GROUPS, SGU_CHUNK, SGU_CHUNK)),
                  _const_spec((SGU_CHUNK, D_SGU)), _const_spec((SGU_CHUNK, SGU_CHUNK))],
        out_specs=[rev(tm, D_MODEL), rev(tm, D_MODEL), rev(tm, 3 * D_ATTN), rev(tm, LANES), rev(tm, 2 * D_SGU),
                   _const_spec((1, D_MODEL)), _const_spec((1, LANES)), _const_spec((1, D_SGU)), _const_spec((1, D_SGU)),
                   _const_spec((N_GROUPS, SGU_CHUNK, SGU_CHUNK)), _const_spec((SGU_CHUNK, LANES))],
        out_shape=[SDS((t, D_MODEL), F32), SDS((t, D_MODEL), BF16), SDS((t, 3 * D_ATTN), BF16), SDS((t, LANES), BF16),
                   SDS((t, 2 * D_SGU), BF16), SDS((1, D_MODEL), F32), SDS((1, LANES), F32), SDS((1, D_SGU), F32),
                   SDS((1, D_SGU), F32), SDS((N_GROUPS, SGU_CHUNK, SGU_CHUNK), F32), SDS((SGU_CHUNK, LANES), F32)],
        scratch_shapes=[pltpu.VMEM((1, LANES), F32), pltpu.VMEM((tm, D_SGU), F32), pltpu.VMEM((tm, D_SGU), F32),
                        pltpu.VMEM((SGU_CHUNK, D_SGU), F32)])


def _pick(n, cap):
    if n <= cap:
        return n
    best = LANES
    for cand in range(LANES, cap + 1, LANES):
        if n % cand == 0:
            best = cand
    return best


def _tn_matmul(a, b, *, bt):
    t, k1 = a.shape
    n = b.shape[1]
    bk = _pick(k1, 1408)
    bn = _pick(n, 1408)
    nsteps = t // bt

    def body(a_ref, b_ref, o_ref):
        @pl.when(pl.program_id(2) == 0)
        def _():
            o_ref[...] = jnp.zeros_like(o_ref)

        o_ref[...] += _dot_tn(a_ref[...], b_ref[...])

    return pl.pallas_call(
        body, name=f"wgrad_{k1}x{n}", grid=(k1 // bk, n // bn, nsteps),
        in_specs=[pl.BlockSpec((bt, bk), lambda i, j, s: (s, i)), pl.BlockSpec((bt, bn), lambda i, j, s: (s, j))],
        out_specs=pl.BlockSpec((bk, bn), lambda i, j, s: (i, j)),
        out_shape=SDS((k1, n), F32),
        compiler_params=_params("arbitrary", "arbitrary", "arbitrary"),
    )(a, b)


def _adamw(parts, w, m, v, *, name):
    rows, cols = w.shape
    br = _pick_rows(rows, cols)
    c1 = 1.0 - ADAM_B1 ** ADAM_STEP
    c2 = 1.0 - ADAM_B2 ** ADAM_STEP

    def body(p_ref, w_ref, m_ref, v_ref, g_ref, d_ref, nm_ref, nv_ref):
        g = p_ref[0].astype(F32)
        for j in range(1, N_DEV):
            g = g + p_ref[j].astype(F32)
        g_ref[...] = g
        nm = ADAM_B1 * m_ref[...] + (1.0 - ADAM_B1) * g
        nv = ADAM_B2 * v_ref[...] + (1.0 - ADAM_B2) * (g * g)
        nm_ref[...] = nm
        nv_ref[...] = nv
        d_ref[...] = -ADAM_LR * ((nm / c1) / (jnp.sqrt(nv / c2) + ADAM_EPS) + ADAM_WD * w_ref[...])

    spec = pl.BlockSpec((br, cols), lambda i: (i, 0))
    return pl.pallas_call(
        body, name=name, grid=(rows // br,),
        in_specs=[pl.BlockSpec((N_DEV, br, cols), lambda i: (0, i, 0)), spec, spec, spec],
        out_specs=[spec] * 4, out_shape=[SDS((rows, cols), F32)] * 4,
        compiler_params=_params("arbitrary"),
    )(parts, w, m, v)


def _pick_rows(rows, cols):
    target = max(8, (256 * 1024) // cols)
    best = 8
    for cand in range(8, min(rows, target) + 1, 8):
        if rows % cand == 0:
            best = cand
    return best


def _peer(k):
    x, y, c = lax.axis_index("x"), lax.axis_index("y"), lax.axis_index("c")
    px = 1 - x if k & 4 else x
    py = 1 - y if k & 2 else y
    pc = 1 - c if k & 1 else c
    return (px, py, pc), 4 * px + 2 * py + pc


def _exchange_scratch(n):
    return [pltpu.SemaphoreType.DMA((N_DEV - 1, n)), pltpu.SemaphoreType.DMA((N_DEV - 1, n)),
            pltpu.SemaphoreType.DMA((n,))]


def _exchange_copies(ins, outs, sems, scatter, landing):
    send_sems, recv_sems, local_sems = sems
    me = 4 * lax.axis_index("x") + 2 * lax.axis_index("y") + lax.axis_index("c")
    copies = [pltpu.make_async_copy(ins[a].at[me] if scatter[a] else ins[a], outs[a].at[me], local_sems.at[a])
              for a in range(len(ins))]
    for k in range(1, N_DEV):
        peer, pidx = _peer(k)
        for a in range(len(ins)):
            copies.append(pltpu.make_async_remote_copy(
                src_ref=ins[a].at[pidx] if scatter[a] else ins[a], dst_ref=outs[a].at[pidx if landing else me],
                send_sem=send_sems.at[k - 1, a], recv_sem=recv_sems.at[k - 1, a], device_id=peer, device_id_type=MESH))
    return copies


def _exchange_start(ins, outs, sems, scatter):
    for cp in _exchange_copies(ins, outs, sems, scatter, landing=False):
        cp.start()


def _exchange_wait(ins, outs, sems, scatter):
    for cp in _exchange_copies(ins, outs, sems, scatter, landing=True):
        cp.wait()


def _exchange_shapes(arrs, scatter):
    return [SDS(a.shape if sc else (N_DEV,) + a.shape, a.dtype) for a, sc in zip(arrs, scatter)]


def _exchange(arrs, scatter, *, name):
    n = len(arrs)

    def body(*refs):
        ins, outs, sems = refs[:n], refs[n:2 * n], refs[2 * n:]
        _exchange_start(ins, outs, sems, scatter)
        _exchange_wait(ins, outs, sems, scatter)

    return pl.pallas_call(
        body, name=name, in_specs=[ANY] * n, out_specs=[ANY] * n, out_shape=_exchange_shapes(arrs, scatter),
        scratch_shapes=_exchange_scratch(n),
    )(*arrs)


def _fused_exchange(body, n_in, n_out, scatter, nsteps):
    n = len(scatter)

    def wrapped(*refs):
        ins, ex_in = refs[:n_in], refs[n_in:n_in + n]
        outs, ex_out = refs[n_in + n:n_in + n + n_out], refs[n_in + n + n_out:n_in + 2 * n + n_out]
        scratch, sems = refs[n_in + 2 * n + n_out:-3], refs[-3:]

        @pl.when(pl.program_id(0) == 0)
        def _():
            _exchange_start(ex_in, ex_out, sems, scatter)

        body(*ins, *outs, *scratch)

        @pl.when(pl.program_id(0) == nsteps - 1)
        def _():
            _exchange_wait(ex_in, ex_out, sems, scatter)

    return wrapped


def _step(x, tgt, small, shards):
    t = x.shape[0]
    tm, tq = _tiles(t)
    r = jnp.arange(SGU_CHUNK, dtype=jnp.int32) // SGU_BLOCK
    mask = (r[None, :] <= r[:, None]).astype(F32)
    layer_shards = lambda l: [shards[n][l] for n in BIG]

    layers = []
    saved = []
    gathered = _exchange(layer_shards(0)[:1], [False], name="gather_weights")
    for l in range(DEPTH):
        w_in = _assemble(gathered[0], BIG[0])
        wqkv = w_in[:, :3 * D_ATTN]
        wf = jnp.pad(w_in[:, 3 * D_ATTN:3 * D_ATTN + N_HEADS], ((0, 0), (0, LANES - N_HEADS)))
        wz = w_in[:, 3 * D_ATTN + N_HEADS:]
        bf = jnp.pad(small["b_f"][l], (0, LANES - N_HEADS))[None, :]
        wsm = (small["w_s"][l] * mask[None]).astype(BF16)
        wsm_t = jnp.swapaxes(wsm, 1, 2)
        bsf = jnp.repeat(small["b_s"][l].T, GROUP_DIM, axis=1)
        lw = dict(wqkv=wqkv, wf=wf, wz=wz, bf=bf, wsm=wsm, wsm_t=wsm_t, bsf=bsf,
                  gmix=small["mix_norm_g"][l][None, :], lng=small["sgu_ln_g"][l][None, :],
                  lnb=small["sgu_ln_b"][l][None, :], gout=small["out_norm_g"][l][None, :],
                  gffn=small["ffn_norm_g"][l][None, :])
        layers.append(lw)
        q, k, v, fl, z, sgu, *late = _fwd_a(x, lw["gmix"], wqkv, wf, bf, wz, lw["lng"], lw["lnb"], wsm, bsf, tm=tm,
                                                 gather=layer_shards(0)[1:] if l == 0 else ())
        w_out, w_gu, w_dn = (_assemble(g, n) for g, n in zip(late if l == 0 else gathered[1:], BIG[1:]))
        lw.update(w_out=w_out, w_gu=w_gu, w_dn=w_dn)
        o, lse = _attn_fwd(q, k, v, tq=tq)
        x1, x2, gu, *gathered = _fwd_b(x, o, sgu, lw["gout"], lw["gffn"], w_out, w_gu, w_dn, tm=tm,
                                       gather=layer_shards(l + 1) if l + 1 < DEPTH else ())
        saved.append(dict(x=x, q=q, k=k, v=v, fl=fl, z=z, sgu=sgu, o=o, lse=lse[:, :2, :], x1=x1, gu=gu))
        x = x2

    dx, loss, dgfin = _loss_bwd(x, tgt, small["final_norm_g"][None, :], tm=tm)
    grads = {n: [None] * DEPTH for n in SMALL if n != "final_norm_g"}
    parts = [None] * DEPTH
    pending = ()
    for l in reversed(range(DEPTH)):
        lw, sv = layers[l], saved[l]
        (dx1, dx2b, a, dgu, xn2, mrg, dx1b, do, delta, dsgu, dgffn, dgout, *landed) = _bwd_b(
            dx, sv["x1"], sv["gu"], sv["o"], sv["sgu"], lw["gout"], lw["gffn"], lw["w_out"], lw["w_gu"], lw["w_dn"],
            tm=tm, scatter=pending)
        if pending:
            parts[l + 1] = landed
        big_grads = {"w_down": _tn_matmul(a, dx2b, bt=tq), "w_gate_up": _tn_matmul(xn2, dgu, bt=tq),
                     "w_out": _tn_matmul(mrg, dx1b, bt=tq)}
        dqt, dk, dv, dck, dcq = _attn_bwd(sv["q"], sv["k"], sv["v"], do, sv["lse"],
                                          delta.reshape(HEAD_PAIRS, 2, t), tq=tq)
        early = [_split(big_grads[n], n).astype(BF16) for n in BIG[1:]] if l == 0 else ()
        (dx, xn, dqkv, dflb, dzb, dgmix, dbf, dlng, dlnb, dws, dbs, *landed) = _bwd_a(
            dx1, sv["x"], sv["z"], sv["fl"], dsgu, dqt, dk, dv, (dck + dcq).reshape(N_HEADS, t), lw["gmix"],
            lw["wqkv"], lw["wf"], lw["wz"], lw["lng"], lw["lnb"], lw["wsm"], lw["wsm_t"], lw["bsf"], mask, tm=tm,
            scatter=early)
        big_grads["w_in"] = jnp.concatenate(
            [_tn_matmul(xn, dqkv, bt=tq), _tn_matmul(xn, dflb, bt=tq)[:, :N_HEADS], _tn_matmul(xn, dzb, bt=tq)], axis=1)
        pending = [_split(big_grads[n], n).astype(BF16) for n in (BIG[:1] if l == 0 else BIG)]
        grads["mix_norm_g"][l] = dgmix[0]
        grads["b_f"][l] = dbf[0, :N_HEADS]
        grads["sgu_ln_g"][l] = dlng[0]
        grads["sgu_ln_b"][l] = dlnb[0]
        grads["w_s"][l] = dws
        grads["b_s"][l] = dbs[:, :N_GROUPS].T
        grads["out_norm_g"][l] = dgout[0]
        grads["ffn_norm_g"][l] = dgffn[0]
    grads = {n: jnp.stack(g) for n, g in grads.items()}
    grads["final_norm_g"] = dgfin[0]
    first, small_parts = _exchange(pending + [_pack(grads)], [True, False], name="scatter_grads")
    parts[0] = [first] + landed
    big_parts = {}
    for a, n in enumerate(BIG):
        stacked = jnp.stack([parts[l][a] for l in range(DEPTH)], axis=1)
        big_parts[n] = stacked.reshape(N_DEV, -1, stacked.shape[-1])
    return loss[0, 0], dx, big_parts, small_parts


SMALL = ("mix_norm_g", "b_f", "sgu_ln_g", "sgu_ln_b", "w_s", "b_s", "out_norm_g", "ffn_norm_g", "final_norm_g")
BIG = ("w_in", "w_out", "w_gate_up", "w_down")
WEIGHTS = ("mix_norm_g", "w_in", "b_f", "sgu_ln_g", "sgu_ln_b", "w_s", "b_s", "out_norm_g", "w_out", "ffn_norm_g",
           "w_gate_up", "w_down", "final_norm_g")
SHARD_AXIS = {"w_in": 1, "w_out": 0, "w_gate_up": 1, "w_down": 0}


def _assemble(gathered, name):
    if SHARD_AXIS[name] == 0:
        return gathered.reshape(-1, gathered.shape[-1])
    return gathered.transpose(1, 0, 2).reshape(gathered.shape[1], -1)


def _split(full, name):
    rows, cols = full.shape
    if SHARD_AXIS[name] == 0:
        return full.reshape(N_DEV, rows // N_DEV, cols)
    return full.reshape(rows, N_DEV, cols // N_DEV).transpose(1, 0, 2)


def _pack(tree):
    flat = jnp.concatenate([tree[n].reshape(-1) for n in SMALL])
    pad = (-flat.shape[0]) % (8 * LANES)
    return jnp.pad(flat, (0, pad)).reshape(-1, LANES)


def _unpack(packed, like):
    flat = packed.reshape(-1)
    out, at = {}, 0
    for n in SMALL:
        size = like[n].size
        out[n] = flat[at:at + size].reshape(like[n].shape)
        at += size
    return out


def kernel(x, mix_norm_g, w_in, b_f, sgu_ln_g, sgu_ln_b, w_s, b_s, out_norm_g, w_out, ffn_norm_g, w_gate_up, w_down, final_norm_g, loss_target, m_mix_norm_g, m_w_in, m_b_f, m_sgu_ln_g, m_sgu_ln_b, m_w_s, m_b_s, m_out_norm_g, m_w_out, m_ffn_norm_g, m_w_gate_up, m_w_down, m_final_norm_g, v_mix_norm_g, v_w_in, v_b_f, v_sgu_ln_g, v_sgu_ln_b, v_w_s, v_b_s, v_out_norm_g, v_w_out, v_ffn_norm_g, v_w_gate_up, v_w_down, v_final_norm_g):
    w = dict(mix_norm_g=mix_norm_g, w_in=w_in, b_f=b_f, sgu_ln_g=sgu_ln_g, sgu_ln_b=sgu_ln_b, w_s=w_s, b_s=b_s,
             out_norm_g=out_norm_g, w_out=w_out, ffn_norm_g=ffn_norm_g, w_gate_up=w_gate_up, w_down=w_down,
             final_norm_g=final_norm_g)
    m = dict(mix_norm_g=m_mix_norm_g, w_in=m_w_in, b_f=m_b_f, sgu_ln_g=m_sgu_ln_g, sgu_ln_b=m_sgu_ln_b, w_s=m_w_s,
             b_s=m_b_s, out_norm_g=m_out_norm_g, w_out=m_w_out, ffn_norm_g=m_ffn_norm_g, w_gate_up=m_w_gate_up,
             w_down=m_w_down, final_norm_g=m_final_norm_g)
    v = dict(mix_norm_g=v_mix_norm_g, w_in=v_w_in, b_f=v_b_f, sgu_ln_g=v_sgu_ln_g, sgu_ln_b=v_sgu_ln_b, w_s=v_w_s,
             b_s=v_b_s, out_norm_g=v_out_norm_g, w_out=v_w_out, ffn_norm_g=v_ffn_norm_g, w_gate_up=v_w_gate_up,
             w_down=v_w_down, final_norm_g=v_final_norm_g)

    loss, dx, big_parts, small_parts = _step(x[0], loss_target[0], {n: w[n] for n in SMALL},
                                             {n: w[n].astype(BF16) for n in BIG})
    loss = lax.psum(loss, ("x", "y", "c"))

    g_out, d_out, m_out, v_out = {}, {}, {}, {}
    for n in BIG:
        shape = w[n].shape
        two_d = lambda a: a.reshape(-1, shape[-1])
        res = _adamw(big_parts[n], two_d(w[n]), two_d(m[n]), two_d(v[n]), name=f"adamw_{n}")
        g_out[n], d_out[n], m_out[n], v_out[n] = (r.reshape(shape) for r in res)
    res = _adamw(small_parts, _pack(w), _pack(m), _pack(v), name="adamw_small")
    for dst, packed in zip((g_out, d_out, m_out, v_out), res):
        dst.update(_unpack(packed, w))

    return (loss, dx[None], *[g_out[n] for n in WEIGHTS], *[d_out[n] for n in WEIGHTS],
            *[m_out[n] for n in WEIGHTS], *[v_out[n] for n in WEIGHTS])
```

```python
import functools
import math

import jax
import jax.numpy as jnp
from jax import lax
from jax.experimental import pallas as pl
from jax.experimental.pallas import tpu as pltpu

F32, BF16 = jnp.float32, jnp.bfloat16
HIGHEST = lax.Precision.HIGHEST
MESH = pl.DeviceIdType.MESH
ANY = pl.BlockSpec(memory_space=pl.ANY)
SDS = jax.ShapeDtypeStruct

N_DEV = 8
DEPTH = 4
D_MODEL = 1024
D_ATTN = 512
D_SGU = 512
N_HEADS = 8
HEAD_DIM = 64
HEAD_PAIRS = N_HEADS // 2
SGU_CHUNK = 128
SGU_BLOCK = 64
N_GROUPS = 8
GROUP_DIM = 64
D_FF = 2816
FF_CHUNK = 1408
N_FF_CHUNKS = D_FF // FF_CHUNK
D_IN = 3 * D_ATTN + N_HEADS + 2 * D_SGU
LANES = 128
EPS = 1e-6
QK_SCALE = HEAD_DIM ** -0.5
INV_SQRT2 = 1.0 / math.sqrt(2.0)
INV_SQRT_2PI = 1.0 / math.sqrt(2.0 * math.pi)
LOG2E = 1.0 / math.log(2.0)
LN2 = math.log(2.0)
ROW_CHUNK = 32
KX_ROWS = LANES + 16

ADAM_LR = 0.001
ADAM_B1 = 0.9
ADAM_B2 = 0.999
ADAM_EPS = 1e-08
ADAM_WD = 0.01
ADAM_STEP = 10

VMEM_LIMIT_BYTES = 56 * 1024 * 1024


def _params(*sem):
    return pltpu.CompilerParams(dimension_semantics=sem or None, vmem_limit_bytes=VMEM_LIMIT_BYTES)


def _dot(a, b):
    return jnp.dot(a, b, preferred_element_type=F32)


def _dot_nt(a, b):
    return lax.dot_general(a, b, (((1,), (1,)), ((), ())), preferred_element_type=F32)


def _dot_tn(a, b):
    return lax.dot_general(a, b, (((0,), (0,)), ((), ())), preferred_element_type=F32)


def _dot_exact(a, b):
    return jnp.dot(a, b, precision=HIGHEST, preferred_element_type=F32)


def _mean(v):
    return jnp.mean(v, axis=-1, keepdims=True)


def _sigmoid(v):
    return 1.0 / (1.0 + jnp.exp(-v))


def _row_spec(tm, n):
    return pl.BlockSpec((tm, n), lambda i: (i, 0))


def _rev_spec(tm, n, nt):
    return pl.BlockSpec((tm, n), lambda i: (nt - 1 - i, 0))


def _const_spec(shape):
    return pl.BlockSpec(shape, lambda i: (0,) * len(shape))


def _tiles(t):
    return min(256, t), min(512, t)


def _group_indicator():
    r = lax.broadcasted_iota(jnp.int32, (D_ATTN, LANES), 0)
    c = lax.broadcasted_iota(jnp.int32, (D_ATTN, LANES), 1)
    return ((r >> 6) == c).astype(F32)


def _sgu_mix(w_ref, zc, lane_grp):
    out = jnp.zeros((SGU_CHUNK, D_SGU), F32)
    for g in range(N_GROUPS):
        out = out + jnp.where(lane_grp == g, _dot(w_ref[g], zc), 0.0)
    return out


def _fwd_a(x, gmix, wqkv, wf, bf, wz, lng, lnb, wsm, bsf, *, tm, gather=()):
    t = x.shape[0]
    nt = t // tm
    nch = tm // SGU_CHUNK

    def body(x_ref, gmix_ref, wqkv_ref, wf_ref, bf_ref, wz_ref, lng_ref, lnb_ref, wsm_ref, bsf_ref,
             q_ref, k_ref, v_ref, fl_ref, z_ref, sgu_ref, carry_ref):
        @pl.when(pl.program_id(0) == 0)
        def _():
            carry_ref[...] = jnp.zeros_like(carry_ref)

        xt = x_ref[...]
        r = lax.rsqrt(_mean(xt * xt) + EPS)
        xn = ((xt * r) * gmix_ref[...]).astype(BF16)
        qkv = _dot(xn, wqkv_ref[...])

        fl = _dot(xn, wf_ref[...]) + bf_ref[...]
        fl_ref[...] = fl
        logf = jnp.minimum(fl, 0.0) - jnp.log1p(jnp.exp(-jnp.abs(fl)))
        row = lax.broadcasted_iota(jnp.int32, (tm, tm), 0)
        col = lax.broadcasted_iota(jnp.int32, (tm, tm), 1)
        c = _dot_exact((col <= row).astype(F32), logf) + carry_ref[...]
        carry_ref[...] = c[tm - 1:tm, :]

        c2 = c * LOG2E
        lane = lax.broadcasted_iota(jnp.int32, (tm, LANES), 1)
        for h in range(N_HEADS):
            pair, hh = divmod(h, 2)
            base = _aug_lane(hh)
            in_head = (lane >= hh * HEAD_DIM) & (lane < (hh + 1) * HEAD_DIM)
            col_h = jnp.sum(jnp.where(lane == h, c2, 0.0), axis=1, keepdims=True)
            hi = col_h.astype(BF16).astype(F32)
            mid = (col_h - hi).astype(BF16).astype(F32)
            lo = (col_h - hi) - mid
            split = jnp.where(lane == base, hi, jnp.where(lane == base + 1, mid, jnp.where(lane == base + 2, lo, 0.0)))
            split_k = jnp.where(lane == base + 3, hi, jnp.where(lane == base + 4, mid,
                                                                 jnp.where(lane == base + 5, lo, 0.0)))
            ones_q = ((lane >= base + 3) & (lane < base + 6)).astype(F32)
            ones_k = ((lane >= base) & (lane < base + 3)).astype(F32)
            blk = slice(pair * LANES, (pair + 1) * LANES)
            q_ref[h] = jnp.where(in_head, qkv[:, blk] * (QK_SCALE * LOG2E), split + ones_q).astype(BF16)
            k_ref[h] = jnp.where(in_head, qkv[:, D_ATTN:2 * D_ATTN][:, blk], ones_k - split_k).astype(BF16)
            v_ref[h] = jnp.where(in_head, qkv[:, 2 * D_ATTN:][:, blk], (lane == base).astype(F32)).astype(BF16)

        z = _dot(xn, wz_ref[...])
        z_ref[...] = z
        zg = 0.5 * z * (1.0 + lax.erf(z * INV_SQRT2))
        zu = zg[:, :D_SGU]
        zv = zg[:, D_SGU:]
        xc = zv - _mean(zv)
        zvn = ((xc * lax.rsqrt(_mean(xc * xc) + EPS)) * lng_ref[...] + lnb_ref[...]).astype(BF16)
        lane_grp = lax.broadcasted_iota(jnp.int32, (SGU_CHUNK, D_SGU), 1) >> 6
        for ch in range(nch):
            rows = slice(ch * SGU_CHUNK, (ch + 1) * SGU_CHUNK)
            mixed = _sgu_mix(wsm_ref, zvn[rows, :], lane_grp) + bsf_ref[...]
            sgu_ref[rows, :] = zu[rows, :] * mixed

    head_spec = pl.BlockSpec((N_HEADS, tm, LANES), lambda i: (0, i, 0))
    return _row_tile_call(
        body, "fwd_a", nt, (x, gmix, wqkv, wf, bf, wz, lng, lnb, wsm, bsf), exchange=gather, scatter=False,
        in_specs=[_row_spec(tm, D_MODEL), _const_spec((1, D_MODEL)), _const_spec((D_MODEL, 3 * D_ATTN)),
                  _const_spec((D_MODEL, LANES)), _const_spec((1, LANES)), _const_spec((D_MODEL, 2 * D_SGU)),
                  _const_spec((1, D_SGU)), _const_spec((1, D_SGU)), _const_spec((N_GROUPS, SGU_CHUNK, SGU_CHUNK)),
                  _const_spec((SGU_CHUNK, D_SGU))],
        out_specs=[head_spec, head_spec, head_spec, _row_spec(tm, LANES), _row_spec(tm, 2 * D_SGU),
                   _row_spec(tm, D_SGU)],
        out_shape=[SDS((N_HEADS, t, LANES), BF16)] * 3 + [SDS((t, LANES), F32), SDS((t, 2 * D_SGU), F32),
                                                          SDS((t, D_SGU), F32)],
        scratch_shapes=[pltpu.VMEM((1, LANES), F32)])


def _aug_lane(hh):
    return (1 - hh) * HEAD_DIM


def _attn_fwd(qa, ka, va, *, tq):
    t = qa.shape[1]
    nq = t // tq
    nrc = tq // ROW_CHUNK

    def body(q_ref, k_hbm, v_hbm, o_ref, lse_ref, k_vm, v_vm, s_ref, p_ref, m_ref, a_ref, acc_ref):
        p = pl.program_id(0)
        i = pl.program_id(1)

        @pl.when(i == 0)
        def _():
            pltpu.sync_copy(k_hbm.at[pl.ds(2 * p, 2)], k_vm)
            pltpu.sync_copy(v_hbm.at[pl.ds(2 * p, 2)], v_vm)

        m_ref[...] = jnp.full(m_ref.shape, -jnp.inf, F32)
        acc_ref[...] = jnp.zeros_like(acc_ref)
        rowq = lax.broadcasted_iota(jnp.int32, (ROW_CHUNK, tq), 0)
        colk = lax.broadcasted_iota(jnp.int32, (ROW_CHUNK, tq), 1)

        def scores(j, slot):
            start = pl.multiple_of(j * tq, tq)
            for h in range(2):
                s_ref[slot, h] = _dot_nt(q_ref[h], k_vm[h, pl.ds(start, tq), :])

        def softmax(slot, masked):
            for h in range(2):
                for r in range(nrc):
                    rows = slice(r * ROW_CHUNK, (r + 1) * ROW_CHUNK)
                    sc = s_ref[slot, h, rows, :]
                    if masked:
                        sc = jnp.where(colk <= rowq + r * ROW_CHUNK, sc, -jnp.inf)
                    m_old = m_ref[h, rows, :]
                    m_new = jnp.maximum(m_old, jnp.max(sc, axis=1, keepdims=True))
                    p_ref[slot, h, rows, :] = jnp.exp2(sc - m_new).astype(BF16)
                    a_ref[slot, h, rows, :] = jnp.exp2(m_old - m_new)
                    m_ref[h, rows, :] = m_new

        def accumulate(j, slot):
            start = pl.multiple_of(j * tq, tq)
            for h in range(2):
                acc_ref[h] = acc_ref[h] * a_ref[slot, h] + _dot(p_ref[slot, h], v_vm[h, pl.ds(start, tq), :])

        scores(0, 0)

        @pl.when(i > 0)
        def _():
            scores(1, 1)
            softmax(0, False)

        def stage(j, slot):
            scores(j + 1, 1 - slot)
            softmax(slot, False)
            accumulate(j - 1, 1 - slot)

        def pair_body(n, carry):
            j = 1 + 2 * n
            stage(j, 1)
            stage(j + 1, 0)
            return carry

        rest = i - 1
        lax.fori_loop(0, rest // 2, pair_body, 0)

        @pl.when((rest > 0) & (lax.rem(rest, 2) == 1))
        def _():
            stage(i - 1, 1)

        @pl.when(lax.rem(i, 2) == 0)
        def _():
            softmax(0, True)

            @pl.when(i > 0)
            def _():
                accumulate(i - 1, 1)

            accumulate(i, 0)

        @pl.when(lax.rem(i, 2) == 1)
        def _():
            softmax(1, True)
            accumulate(i - 1, 0)
            accumulate(i, 1)

        lane = lax.broadcasted_iota(jnp.int32, (tq, LANES), 1)
        l_h = [jnp.sum(jnp.where(lane == _aug_lane(h), acc_ref[h], 0.0), axis=1, keepdims=True) for h in range(2)]
        o_ref[...] = jnp.where(lane < HEAD_DIM, acc_ref[0] / l_h[0], acc_ref[1] / l_h[1])
        lse = [m_ref[h] + jnp.log2(l_h[h]) for h in range(2)]
        lse_cols = jnp.where(lane == 0, lse[0], jnp.where(lane == 1, lse[1], 0.0))
        lse_ref[0] = lse_cols.T[:8, :]

    return pl.pallas_call(
        body, name="attn_fwd", grid=(HEAD_PAIRS, nq),
        in_specs=[pl.BlockSpec((2, tq, LANES), lambda p, i: (p, i, 0)), ANY, ANY],
        out_specs=[pl.BlockSpec((tq, LANES), lambda p, i: (i, p)),
                   pl.BlockSpec((1, 8, tq), lambda p, i: (p, 0, i))],
        out_shape=[SDS((t, D_ATTN), F32), SDS((HEAD_PAIRS, 8, t), F32)],
        scratch_shapes=[pltpu.VMEM((2, t, LANES), BF16), pltpu.VMEM((2, t, LANES), BF16),
                        pltpu.VMEM((2, 2, tq, tq), F32), pltpu.VMEM((2, 2, tq, tq), BF16),
                        pltpu.VMEM((2, tq, 1), F32), pltpu.VMEM((2, 2, tq, 1), F32), pltpu.VMEM((2, tq, LANES), F32)],
        compiler_params=_params("arbitrary", "arbitrary"),
    )(qa, ka, va)


def _attn_bwd(qa, ka, va, do, lse_row, delta_row, *, tq):
    t = qa.shape[1]
    nq = t // tq
    nrc = tq // ROW_CHUNK

    def body(q_hbm, do_ref, k_ref, v_ref, lse_ref, dl_ref, dqt_ref, dk_ref, dv_ref, dck_ref, dcq_ref,
             q_vm, st_ref, dp_ref, pt_ref, ds_ref, dka_ref, dva_ref, vh_ref, kx_ref):
        p = pl.program_id(0)
        j = pl.program_id(1)

        @pl.when(j == 0)
        def _():
            pltpu.sync_copy(q_hbm.at[pl.ds(2 * p, 2)], q_vm)
            dqt_ref[...] = jnp.zeros_like(dqt_ref)
            dcq_ref[...] = jnp.zeros_like(dcq_ref)

        dka_ref[...] = jnp.zeros_like(dka_ref)
        dva_ref[...] = jnp.zeros_like(dva_ref)
        lane = lax.broadcasted_iota(jnp.int32, (tq, LANES), 1)
        in_head = (lane < HEAD_DIM, lane >= HEAD_DIM)
        for h in range(2):
            zero = jnp.zeros((tq, LANES), BF16)
            vh_ref[h] = jnp.where(in_head[h], v_ref[h], zero)
            kx_ref[h, :LANES, :] = jnp.where(in_head[h], k_ref[h], zero).astype(F32).T.astype(BF16)
            kx_ref[h, LANES:, :] = jnp.ones((KX_ROWS - LANES, tq), BF16)
        rowk = lax.broadcasted_iota(jnp.int32, (ROW_CHUNK, tq), 0)
        colq = lax.broadcasted_iota(jnp.int32, (ROW_CHUNK, tq), 1)

        def step(i, masked, slot):
            start = pl.multiple_of(i * tq, tq)
            cols = pl.ds(start, tq)
            do2 = do_ref[pl.ds(start, tq), :]
            q_h = [q_vm[h, pl.ds(start, tq), :] for h in range(2)]
            for h in range(2):
                st_ref[slot, h] = _dot_nt(k_ref[h], q_h[h])
                dp_ref[slot, h] = _dot_nt(vh_ref[h], do2)
            for h in range(2):
                lse = lse_ref[0, h:h + 1, cols]
                delta = dl_ref[0, h:h + 1, cols]
                for r in range(nrc):
                    rows = slice(r * ROW_CHUNK, (r + 1) * ROW_CHUNK)
                    st = st_ref[slot, h, rows, :]
                    if masked:
                        st = jnp.where(rowk + r * ROW_CHUNK <= colq, st, -jnp.inf)
                    pt = jnp.exp2(st - lse)
                    pt_ref[slot, h, rows, :] = pt.astype(BF16)
                    ds_ref[slot, h, rows, :] = (pt * (dp_ref[slot, h, rows, :] - delta)).astype(BF16)
            dq_t = jnp.zeros((LANES, tq), F32)
            for h in range(2):
                dva_ref[h] += _dot(pt_ref[slot, h], do2)
                dka_ref[h] += _dot(ds_ref[slot, h], q_h[h])
                ext = _dot(kx_ref[h], ds_ref[slot, h])
                dq_t = dq_t + ext[:LANES, :]
                dcq_ref[0, h:h + 1, cols] += ext[LANES:LANES + 1, :]
            dqt_ref[0, :, cols] += dq_t

        def pair_body(n, carry):
            i = j + 1 + 2 * n
            step(i, False, 0)
            step(i + 1, False, 1)
            return carry

        step(j, True, 0)
        after = nq - 1 - j
        lax.fori_loop(0, after // 2, pair_body, 0)

        @pl.when(lax.rem(after, 2) == 1)
        def _():
            step(nq - 1, False, 0)

        dk_ref[...] = (jnp.where(in_head[0], dka_ref[0], dka_ref[1]) * LN2).astype(BF16)
        dv_ref[...] = jnp.where(in_head[0], dva_ref[0], dva_ref[1]).astype(BF16)
        own = pl.ds(pl.multiple_of(j * tq, tq), tq)
        for h in range(2):
            at = _aug_lane(h) + 3
            dck_ref[0, h:h + 1, own] = -dka_ref[h].T[at:at + 1, :]

    rows = pl.BlockSpec((1, 2, t), lambda p, j: (p, 0, 0))
    tiles = pl.BlockSpec((1, LANES, t), lambda p, j: (p, 0, 0))
    return pl.pallas_call(
        body, name="attn_bwd", grid=(HEAD_PAIRS, nq),
        in_specs=[ANY, pl.BlockSpec((t, LANES), lambda p, j: (0, p)),
                  pl.BlockSpec((2, tq, LANES), lambda p, j: (p, j, 0)),
                  pl.BlockSpec((2, tq, LANES), lambda p, j: (p, j, 0)), rows, rows],
        out_specs=[tiles,
                   pl.BlockSpec((tq, LANES), lambda p, j: (j, p)),
                   pl.BlockSpec((tq, LANES), lambda p, j: (j, p)), rows, rows],
        out_shape=[SDS((HEAD_PAIRS, LANES, t), F32), SDS((t, D_ATTN), BF16), SDS((t, D_ATTN), BF16),
                   SDS((HEAD_PAIRS, 2, t), F32), SDS((HEAD_PAIRS, 2, t), F32)],
        scratch_shapes=[pltpu.VMEM((2, t, LANES), BF16), pltpu.VMEM((2, 2, tq, tq), F32),
                        pltpu.VMEM((2, 2, tq, tq), F32), pltpu.VMEM((2, 2, tq, tq), BF16),
                        pltpu.VMEM((2, 2, tq, tq), BF16),
                        pltpu.VMEM((2, tq, LANES), F32), pltpu.VMEM((2, tq, LANES), F32),
                        pltpu.VMEM((2, tq, LANES), BF16), pltpu.VMEM((2, KX_ROWS, tq), BF16)],
        compiler_params=_params("arbitrary", "arbitrary"),
    )(qa, do, ka, va, lse_row, delta_row)

def _load_weights_once(pairs):
    @pl.when(pl.program_id(0) == 0)
    def _():
        for src, dst in pairs:
            pltpu.sync_copy(src, dst)


def _row_tile_call(body, name, nt, operands, *, in_specs, out_specs, out_shape, scratch_shapes, exchange, scatter):
    if not exchange:
        return pl.pallas_call(body, name=name, grid=(nt,), in_specs=in_specs, out_specs=out_specs,
                              out_shape=out_shape, scratch_shapes=scratch_shapes,
                              compiler_params=_params("arbitrary"))(*operands)
    flags = [scatter] * len(exchange)
    n = len(exchange)
    return pl.pallas_call(
        _fused_exchange(body, len(operands), len(out_shape), flags, nt),
        name=name + ("_scatter" if scatter else "_gather"), grid=(nt,),
        in_specs=list(in_specs) + [ANY] * n, out_specs=list(out_specs) + [ANY] * n,
        out_shape=list(out_shape) + _exchange_shapes(exchange, flags),
        scratch_shapes=list(scratch_shapes) + _exchange_scratch(n),
        compiler_params=_params("arbitrary"))(*operands, *exchange)


def _fwd_b(x, o, sgu, gout, gffn, w_out, w_gu, w_dn, *, tm, gather=()):
    t = x.shape[0]
    nt = t // tm

    def body(x_ref, o_ref, s_ref, gout_ref, gffn_ref, wout_hbm, wgu_hbm, wdn_hbm,
             x1_ref, x2_ref, gu_ref, wout, wgu, wdn):
        _load_weights_once(((wout_hbm, wout), (wgu_hbm, wgu), (wdn_hbm, wdn)))
        ov = o_ref[...]
        sv = s_ref[...]
        mo = ((ov * lax.rsqrt(_mean(ov * ov) + EPS)) * gout_ref[:, :D_ATTN]).astype(BF16)
        ms = ((sv * lax.rsqrt(_mean(sv * sv) + EPS)) * gout_ref[:, D_ATTN:]).astype(BF16)
        x1 = x_ref[...] + (_dot(mo, wout[:D_ATTN, :]) + _dot(ms, wout[D_ATTN:, :]))
        x1_ref[...] = x1
        xn2 = ((x1 * lax.rsqrt(_mean(x1 * x1) + EPS)) * gffn_ref[...]).astype(BF16)
        y = jnp.zeros((tm, D_MODEL), F32)
        for n in range(N_FF_CHUNKS):
            lo, hi = n * FF_CHUNK, (n + 1) * FF_CHUNK
            gate = _dot(xn2, wgu[:, lo:hi])
            up = _dot(xn2, wgu[:, D_FF + lo:D_FF + hi])
            gu_ref[:, lo:hi] = gate
            gu_ref[:, D_FF + lo:D_FF + hi] = up
            a = ((gate * _sigmoid(gate)) * up).astype(BF16)
            y = y + _dot(a, wdn[lo:hi, :])
        x2_ref[...] = x1 + y

    return _row_tile_call(
        body, "fwd_b", nt, (x, o, sgu, gout, gffn, w_out, w_gu, w_dn),
        in_specs=[_row_spec(tm, D_MODEL), _row_spec(tm, D_ATTN), _row_spec(tm, D_SGU),
                  _const_spec((1, D_MODEL)), _const_spec((1, D_MODEL)), ANY, ANY, ANY],
        out_specs=[_row_spec(tm, D_MODEL), _row_spec(tm, D_MODEL), _row_spec(tm, 2 * D_FF)],
        out_shape=[SDS((t, D_MODEL), F32), SDS((t, D_MODEL), F32), SDS((t, 2 * D_FF), F32)],
        scratch_shapes=[pltpu.VMEM((D_MODEL, D_MODEL), BF16), pltpu.VMEM((D_MODEL, 2 * D_FF), BF16),
                        pltpu.VMEM((D_FF, D_MODEL), BF16)],
        exchange=gather, scatter=False)


def _loss_bwd(x, tgt, gfin, *, tm):
    t = x.shape[0]
    nt = t // tm

    def body(x_ref, t_ref, g_ref, dx_ref, loss_ref, dg_ref):
        @pl.when(pl.program_id(0) == 0)
        def _():
            loss_ref[...] = jnp.zeros_like(loss_ref)
            dg_ref[...] = jnp.zeros_like(dg_ref)

        xt = x_ref[...]
        g = g_ref[...]
        r = lax.rsqrt(_mean(xt * xt) + EPS)
        xh = xt * r
        err = xh * g - t_ref[...]
        loss_ref[...] += 0.5 * jnp.sum(_mean(err * err), axis=0, keepdims=True)
        dy = err * (1.0 / D_MODEL)
        dg_ref[...] += jnp.sum(dy * xh, axis=0, keepdims=True)
        dyg = dy * g
        dx_ref[...] = r * (dyg - xh * _mean(dyg * xh))

    return pl.pallas_call(
        body, name="loss_bwd", grid=(nt,),
        in_specs=[_row_spec(tm, D_MODEL), _row_spec(tm, D_MODEL), _const_spec((1, D_MODEL))],
        out_specs=[_row_spec(tm, D_MODEL), _const_spec((1, 1)), _const_spec((1, D_MODEL))],
        out_shape=[SDS((t, D_MODEL), F32), SDS((1, 1), F32), SDS((1, D_MODEL), F32)],
        compiler_params=_params("arbitrary"),
    )(x, tgt, gfin)


def _bwd_b(dx2, x1, gu, o, sgu, gout, gffn, w_out, w_gu, w_dn, *, tm, scatter=()):
    t = dx2.shape[0]
    nt = t // tm

    def body(dx2_ref, x1_ref, gu_ref, o_ref, s_ref, gout_ref, gffn_ref, wout_hbm, wgu_hbm, wdn_hbm,
             dx1_ref, dx2b_ref, a_ref, dgu_ref, xn2_ref, mrg_ref, dx1b_ref, do_ref, dl_ref, dsgu_ref,
             dgffn_ref, dgout_ref, wout, wgu, wdn):
        _load_weights_once(((wout_hbm, wout), (wgu_hbm, wgu), (wdn_hbm, wdn)))

        @pl.when(pl.program_id(0) == 0)
        def _():
            dgffn_ref[...] = jnp.zeros_like(dgffn_ref)
            dgout_ref[...] = jnp.zeros_like(dgout_ref)

        dx2 = dx2_ref[...]
        dx2b = dx2.astype(BF16)
        dx2b_ref[...] = dx2b
        dxn2 = jnp.zeros((tm, D_MODEL), F32)
        for n in range(N_FF_CHUNKS):
            lo, hi = n * FF_CHUNK, (n + 1) * FF_CHUNK
            gate = gu_ref[:, lo:hi]
            up = gu_ref[:, D_FF + lo:D_FF + hi]
            sg = _sigmoid(gate)
            si = gate * sg
            a_ref[:, lo:hi] = (si * up).astype(BF16)
            d_a = _dot_nt(dx2b, wdn[lo:hi, :])
            dgb = ((d_a * up) * (sg * (1.0 + gate * (1.0 - sg)))).astype(BF16)
            dub = (d_a * si).astype(BF16)
            dgu_ref[:, lo:hi] = dgb
            dgu_ref[:, D_FF + lo:D_FF + hi] = dub
            dxn2 = dxn2 + (_dot_nt(dgb, wgu[:, lo:hi]) + _dot_nt(dub, wgu[:, D_FF + lo:D_FF + hi]))

        x1 = x1_ref[...]
        gffn = gffn_ref[...]
        r1 = lax.rsqrt(_mean(x1 * x1) + EPS)
        xh1 = x1 * r1
        xn2_ref[...] = (xh1 * gffn).astype(BF16)
        dgffn_ref[...] += jnp.sum(dxn2 * xh1, axis=0, keepdims=True)
        dyg = dxn2 * gffn
        dx1 = dx2 + r1 * (dyg - xh1 * _mean(dyg * xh1))
        dx1_ref[...] = dx1
        dx1b = dx1.astype(BF16)
        dx1b_ref[...] = dx1b

        ov = o_ref[...]
        sv = s_ref[...]
        go = gout_ref[:, :D_ATTN]
        gs = gout_ref[:, D_ATTN:]
        ro = lax.rsqrt(_mean(ov * ov) + EPS)
        rs = lax.rsqrt(_mean(sv * sv) + EPS)
        oh = ov * ro
        sh = sv * rs
        mrg_ref[:, :D_ATTN] = (oh * go).astype(BF16)
        mrg_ref[:, D_ATTN:] = (sh * gs).astype(BF16)
        dmo = _dot_nt(dx1b, wout[:D_ATTN, :])
        dms = _dot_nt(dx1b, wout[D_ATTN:, :])
        dgout_ref[:, :D_ATTN] += jnp.sum(dmo * oh, axis=0, keepdims=True)
        dgout_ref[:, D_ATTN:] += jnp.sum(dms * sh, axis=0, keepdims=True)
        dmog = dmo * go
        d_o = ro * (dmog - oh * _mean(dmog * oh))
        do_ref[...] = d_o.astype(BF16)
        dl_ref[...] = _dot_exact(d_o * ov, _group_indicator()).T[:N_HEADS, :]
        dmsg = dms * gs
        dsgu_ref[...] = rs * (dmsg - sh * _mean(dmsg * sh))

    return _row_tile_call(
        body, "bwd_b", nt, (dx2, x1, gu, o, sgu, gout, gffn, w_out, w_gu, w_dn),
        in_specs=[_row_spec(tm, D_MODEL), _row_spec(tm, D_MODEL), _row_spec(tm, 2 * D_FF), _row_spec(tm, D_ATTN),
                  _row_spec(tm, D_SGU), _const_spec((1, D_MODEL)), _const_spec((1, D_MODEL)), ANY, ANY, ANY],
        out_specs=[_row_spec(tm, D_MODEL), _row_spec(tm, D_MODEL), _row_spec(tm, D_FF), _row_spec(tm, 2 * D_FF),
                   _row_spec(tm, D_MODEL), _row_spec(tm, D_MODEL), _row_spec(tm, D_MODEL), _row_spec(tm, D_ATTN),
                   pl.BlockSpec((N_HEADS, tm), lambda i: (0, i)), _row_spec(tm, D_SGU),
                   _const_spec((1, D_MODEL)), _const_spec((1, D_MODEL))],
        out_shape=[SDS((t, D_MODEL), F32), SDS((t, D_MODEL), BF16), SDS((t, D_FF), BF16), SDS((t, 2 * D_FF), BF16),
                   SDS((t, D_MODEL), BF16), SDS((t, D_MODEL), BF16), SDS((t, D_MODEL), BF16),
                   SDS((t, D_ATTN), BF16), SDS((N_HEADS, t), F32), SDS((t, D_SGU), F32),
                   SDS((1, D_MODEL), F32), SDS((1, D_MODEL), F32)],
        scratch_shapes=[pltpu.VMEM((D_MODEL, D_MODEL), BF16), pltpu.VMEM((D_MODEL, 2 * D_FF), BF16),
                        pltpu.VMEM((D_FF, D_MODEL), BF16)],
        exchange=scatter, scatter=True)


def _bwd_a(dx1, x, z, fl, dsgu, dq, dk, dv, dc, gmix, wqkv, wf, wz, lng, lnb, wsm, wsm_t, bsf, mask, *, tm,
           scatter=()):
    t = x.shape[0]
    nt = t // tm
    nch = tm // SGU_CHUNK

    def body(dx1_ref, x_ref, z_ref, fl_ref, dsgu_ref, dq_ref, dk_ref, dv_ref, dc_ref, gmix_ref, wqkv_ref, wf_ref,
             wz_ref, lng_ref, lnb_ref, wsm_ref, wsmt_ref, bsf_ref, mask_ref,
             dx_ref, xn_ref, dqkv_ref, dflb_ref, dzb_ref, dgmix_ref, dbf_ref, dlng_ref, dlnb_ref, dws_ref, dbs_ref,
             carry_ref, dzvn_ref, dzu_ref, dbacc_ref):
        step = pl.program_id(0)

        @pl.when(step == 0)
        def _():
            carry_ref[...] = jnp.zeros_like(carry_ref)
            dbacc_ref[...] = jnp.zeros_like(dbacc_ref)
            for ref in (dgmix_ref, dbf_ref, dlng_ref, dlnb_ref, dws_ref):
                ref[...] = jnp.zeros_like(ref)

        z = z_ref[...]
        erf = lax.erf(z * INV_SQRT2)
        cdf = 0.5 * (1.0 + erf)
        zg = z * cdf
        zu = zg[:, :D_SGU]
        zv = zg[:, D_SGU:]
        xc = zv - _mean(zv)
        rln = lax.rsqrt(_mean(xc * xc) + EPS)
        zh = xc * rln
        lng = lng_ref[...]
        zvn = (zh * lng + lnb_ref[...]).astype(BF16)
        dsgu = dsgu_ref[...]
        lane_grp = lax.broadcasted_iota(jnp.int32, (SGU_CHUNK, D_SGU), 1) >> 6
        for ch in range(nch):
            rows = slice(ch * SGU_CHUNK, (ch + 1) * SGU_CHUNK)
            zc = zvn[rows, :]
            ds_c = dsgu[rows, :]
            mixed = _sgu_mix(wsm_ref, zc, lane_grp) + bsf_ref[...]
            dzu_ref[rows, :] = ds_c * mixed
            dmix = ds_c * zu[rows, :]
            dbacc_ref[...] += dmix
            dmb = dmix.astype(BF16)
            dzvn_ref[rows, :] = _sgu_mix(wsmt_ref, dmb, lane_grp)
            for g in range(N_GROUPS):
                dws_ref[g] += _dot_nt(jnp.where(lane_grp == g, dmb, jnp.zeros_like(dmb)), zc)
        dzvn = dzvn_ref[...]
        dlng_ref[...] += jnp.sum(dzvn * zh, axis=0, keepdims=True)
        dlnb_ref[...] += jnp.sum(dzvn, axis=0, keepdims=True)
        dzh = dzvn * lng
        dzv = rln * ((dzh - _mean(dzh)) - zh * _mean(dzh * zh))
        pdf = jnp.exp(-0.5 * (z * z)) * INV_SQRT_2PI
        dgelu = cdf + z * pdf
        dzb_ref[:, :D_SGU] = (dzu_ref[...] * dgelu[:, :D_SGU]).astype(BF16)
        dzb_ref[:, D_SGU:] = (dzv * dgelu[:, D_SGU:]).astype(BF16)

        dc = jnp.concatenate([dc_ref[...], jnp.zeros((LANES - N_HEADS, tm), F32)], axis=0).T
        row = lax.broadcasted_iota(jnp.int32, (tm, tm), 0)
        col = lax.broadcasted_iota(jnp.int32, (tm, tm), 1)
        dlogf = _dot_exact((col >= row).astype(F32), dc) + carry_ref[...]
        carry_ref[...] = dlogf[0:1, :]
        dfl = dlogf * _sigmoid(-fl_ref[...])
        dbf_ref[...] += jnp.sum(dfl, axis=0, keepdims=True)
        dflb = dfl.astype(BF16)
        dflb_ref[...] = dflb

        for pair in range(HEAD_PAIRS):
            dqkv_ref[:, pair * LANES:(pair + 1) * LANES] = (dq_ref[pair].T * QK_SCALE).astype(BF16)
        dqkv_ref[:, D_ATTN:2 * D_ATTN] = dk_ref[...]
        dqkv_ref[:, 2 * D_ATTN:] = dv_ref[...]
        dxn = _dot_nt(dqkv_ref[...], wqkv_ref[...]) + _dot_nt(dflb, wf_ref[...]) + _dot_nt(dzb_ref[...], wz_ref[...])

        xt = x_ref[...]
        gmix = gmix_ref[...]
        r = lax.rsqrt(_mean(xt * xt) + EPS)
        xh = xt * r
        xn_ref[...] = (xh * gmix).astype(BF16)
        dgmix_ref[...] += jnp.sum(dxn * xh, axis=0, keepdims=True)
        dyg = dxn * gmix
        dx_ref[...] = dx1_ref[...] + r * (dyg - xh * _mean(dyg * xh))

        @pl.when(step == nt - 1)
        def _():
            for g in range(N_GROUPS):
                dws_ref[g] = dws_ref[g] * mask_ref[...]
            dbs_ref[...] = _dot_exact(dbacc_ref[...], _group_indicator())

    rev = functools.partial(_rev_spec, nt=nt)
    return _row_tile_call(
        body, "bwd_a", nt, (dx1, x, z, fl, dsgu, dq, dk, dv, dc, gmix, wqkv, wf, wz, lng, lnb, wsm, wsm_t, bsf, mask),
        exchange=scatter, scatter=True,
        in_specs=[rev(tm, D_MODEL), rev(tm, D_MODEL), rev(tm, 2 * D_SGU), rev(tm, LANES), rev(tm, D_SGU),
                  pl.BlockSpec((HEAD_PAIRS, LANES, tm), lambda i: (0, 0, nt - 1 - i)), rev(tm, D_ATTN), rev(tm, D_ATTN),
                  pl.BlockSpec((N_HEADS, tm), lambda i: (0, nt - 1 - i)),
                  _const_spec((1, D_MODEL)), _const_spec((D_MODEL, 3 * D_ATTN)), _const_spec((D_MODEL, LANES)),
                  _const_spec((D_MODEL, 2 * D_SGU)), _const_spec((1, D_SGU)), _const_spec((1, D_SGU)),
                  _const_spec((N_GROUPS, SGU_CHUNK, SGU_CHUNK)), _const_spec((N_GROUPS, SGU_CHUNK, SGU_CHUNK)),
                  _const_spec((SGU_CHUNK, D_SGU)), _const_spec((SGU_CHUNK, SGU_CHUNK))],
        out_specs=[rev(tm, D_MODEL), rev(tm, D_MODEL), rev(tm, 3 * D_ATTN), rev(tm, LANES), rev(tm, 2 * D_SGU),
                   _const_spec((1, D_MODEL)), _const_spec((1, LANES)), _const_spec((1, D_SGU)), _const_spec((1, D_SGU)),
                   _const_spec((N_GROUPS, SGU_CHUNK, SGU_CHUNK)), _const_spec((SGU_CHUNK, LANES))],
        out_shape=[SDS((t, D_MODEL), F32), SDS((t, D_MODEL), BF16), SDS((t, 3 * D_ATTN), BF16), SDS((t, LANES), BF16),
                   SDS((t, 2 * D_SGU), BF16), SDS((1, D_MODEL), F32), SDS((1, LANES), F32), SDS((1, D_SGU), F32),
                   SDS((1, D_SGU), F32), SDS((N_GROUPS, SGU_CHUNK, SGU_CHUNK), F32), SDS((SGU_CHUNK, LANES), F32)],
        scratch_shapes=[pltpu.VMEM((1, LANES), F32), pltpu.VMEM((tm, D_SGU), F32), pltpu.VMEM((tm, D_SGU), F32),
                        pltpu.VMEM((SGU_CHUNK, D_SGU), F32)])


def _pick(n, cap):
    if n <= cap:
        return n
    best = LANES
    for cand in range(LANES, cap + 1, LANES):
        if n % cand == 0:
            best = cand
    return best


def _tn_matmul(a, b, *, bt):
    t, k1 = a.shape
    n = b.shape[1]
    bk = _pick(k1, 1408)
    bn = _pick(n, 1408)
    nsteps = t // bt

    def body(a_ref, b_ref, o_ref):
        @pl.when(pl.program_id(2) == 0)
        def _():
            o_ref[...] = jnp.zeros_like(o_ref)

        o_ref[...] += _dot_tn(a_ref[...], b_ref[...])

    return pl.pallas_call(
        body, name=f"wgrad_{k1}x{n}", grid=(k1 // bk, n // bn, nsteps),
        in_specs=[pl.BlockSpec((bt, bk), lambda i, j, s: (s, i)), pl.BlockSpec((bt, bn), lambda i, j, s: (s, j))],
        out_specs=pl.BlockSpec((bk, bn), lambda i, j, s: (i, j)),
        out_shape=SDS((k1, n), F32),
        compiler_params=_params("arbitrary", "arbitrary", "arbitrary"),
    )(a, b)


def _adamw(parts, w, m, v, *, name):
    rows, cols = w.shape
    br = _pick_rows(rows, cols)
    c1 = 1.0 - ADAM_B1 ** ADAM_STEP
    c2 = 1.0 - ADAM_B2 ** ADAM_STEP

    def body(p_ref, w_ref, m_ref, v_ref, g_ref, d_ref, nm_ref, nv_ref):
        g = p_ref[0].astype(F32)
        for j in range(1, N_DEV):
            g = g + p_ref[j].astype(F32)
        g_ref[...] = g
        nm = ADAM_B1 * m_ref[...] + (1.0 - ADAM_B1) * g
        nv = ADAM_B2 * v_ref[...] + (1.0 - ADAM_B2) * (g * g)
        nm_ref[...] = nm
        nv_ref[...] = nv
        d_ref[...] = -ADAM_LR * ((nm / c1) / (jnp.sqrt(nv / c2) + ADAM_EPS) + ADAM_WD * w_ref[...])

    spec = pl.BlockSpec((br, cols), lambda i: (i, 0))
    return pl.pallas_call(
        body, name=name, grid=(rows // br,),
        in_specs=[pl.BlockSpec((N_DEV, br, cols), lambda i: (0, i, 0)), spec, spec, spec],
        out_specs=[spec] * 4, out_shape=[SDS((rows, cols), F32)] * 4,
        compiler_params=_params("arbitrary"),
    )(parts, w, m, v)


def _pick_rows(rows, cols):
    target = max(8, (256 * 1024) // cols)
    best = 8
    for cand in range(8, min(rows, target) + 1, 8):
        if rows % cand == 0:
            best = cand
    return best


def _peer(k):
    x, y, c = lax.axis_index("x"), lax.axis_index("y"), lax.axis_index("c")
    px = 1 - x if k & 4 else x
    py = 1 - y if k & 2 else y
    pc = 1 - c if k & 1 else c
    return (px, py, pc), 4 * px + 2 * py + pc


def _exchange_scratch(n):
    return [pltpu.SemaphoreType.DMA((N_DEV - 1, n)), pltpu.SemaphoreType.DMA((N_DEV - 1, n)),
            pltpu.SemaphoreType.DMA((n,))]


def _exchange_copies(ins, outs, sems, scatter, landing):
    send_sems, recv_sems, local_sems = sems
    me = 4 * lax.axis_index("x") + 2 * lax.axis_index("y") + lax.axis_index("c")
    copies = [pltpu.make_async_copy(ins[a].at[me] if scatter[a] else ins[a], outs[a].at[me], local_sems.at[a])
              for a in range(len(ins))]
    for k in range(1, N_DEV):
        peer, pidx = _peer(k)
        for a in range(len(ins)):
            copies.append(pltpu.make_async_remote_copy(
                src_ref=ins[a].at[pidx] if scatter[a] else ins[a], dst_ref=outs[a].at[pidx if landing else me],
                send_sem=send_sems.at[k - 1, a], recv_sem=recv_sems.at[k - 1, a], device_id=peer, device_id_type=MESH))
    return copies


def _exchange_start(ins, outs, sems, scatter):
    for cp in _exchange_copies(ins, outs, sems, scatter, landing=False):
        cp.start()


def _exchange_wait(ins, outs, sems, scatter):
    for cp in _exchange_copies(ins, outs, sems, scatter, landing=True):
        cp.wait()


def _exchange_shapes(arrs, scatter):
    return [SDS(a.shape if sc else (N_DEV,) + a.shape, a.dtype) for a, sc in zip(arrs, scatter)]


def _exchange(arrs, scatter, *, name):
    n = len(arrs)

    def body(*refs):
        ins, outs, sems = refs[:n], refs[n:2 * n], refs[2 * n:]
        _exchange_start(ins, outs, sems, scatter)
        _exchange_wait(ins, outs, sems, scatter)

    return pl.pallas_call(
        body, name=name, in_specs=[ANY] * n, out_specs=[ANY] * n, out_shape=_exchange_shapes(arrs, scatter),
        scratch_shapes=_exchange_scratch(n),
    )(*arrs)


def _fused_exchange(body, n_in, n_out, scatter, nsteps):
    n = len(scatter)

    def wrapped(*refs):
        ins, ex_in = refs[:n_in], refs[n_in:n_in + n]
        outs, ex_out = refs[n_in + n:n_in + n + n_out], refs[n_in + n + n_out:n_in + 2 * n + n_out]
        scratch, sems = refs[n_in + 2 * n + n_out:-3], refs[-3:]

        @pl.when(pl.program_id(0) == 0)
        def _():
            _exchange_start(ex_in, ex_out, sems, scatter)

        body(*ins, *outs, *scratch)

        @pl.when(pl.program_id(0) == nsteps - 1)
        def _():
            _exchange_wait(ex_in, ex_out, sems, scatter)

    return wrapped


def _step(x, tgt, small, shards):
    t = x.shape[0]
    tm, tq = _tiles(t)
    r = jnp.arange(SGU_CHUNK, dtype=jnp.int32) // SGU_BLOCK
    mask = (r[None, :] <= r[:, None]).astype(F32)
    layer_shards = lambda l: [shards[n][l] for n in BIG]

    layers = []
    saved = []
    gathered = _exchange(layer_shards(0)[:1], [False], name="gather_weights")
    for l in range(DEPTH):
        w_in = _assemble(gathered[0], BIG[0])
        wqkv = w_in[:, :3 * D_ATTN]
        wf = jnp.pad(w_in[:, 3 * D_ATTN:3 * D_ATTN + N_HEADS], ((0, 0), (0, LANES - N_HEADS)))
        wz = w_in[:, 3 * D_ATTN + N_HEADS:]
        bf = jnp.pad(small["b_f"][l], (0, LANES - N_HEADS))[None, :]
        wsm = (small["w_s"][l] * mask[None]).astype(BF16)
        wsm_t = jnp.swapaxes(wsm, 1, 2)
        bsf = jnp.repeat(small["b_s"][l].T, GROUP_DIM, axis=1)
        lw = dict(wqkv=wqkv, wf=wf, wz=wz, bf=bf, wsm=wsm, wsm_t=wsm_t, bsf=bsf,
                  gmix=small["mix_norm_g"][l][None, :], lng=small["sgu_ln_g"][l][None, :],
                  lnb=small["sgu_ln_b"][l][None, :], gout=small["out_norm_g"][l][None, :],
                  gffn=small["ffn_norm_g"][l][None, :])
        layers.append(lw)
        q, k, v, fl, z, sgu, *late = _fwd_a(x, lw["gmix"], wqkv, wf, bf, wz, lw["lng"], lw["lnb"], wsm, bsf, tm=tm,
                                                 gather=layer_shards(0)[1:] if l == 0 else ())
        w_out, w_gu, w_dn = (_assemble(g, n) for g, n in zip(late if l == 0 else gathered[1:], BIG[1:]))
        lw.update(w_out=w_out, w_gu=w_gu, w_dn=w_dn)
        o, lse = _attn_fwd(q, k, v, tq=tq)
        x1, x2, gu, *gathered = _fwd_b(x, o, sgu, lw["gout"], lw["gffn"], w_out, w_gu, w_dn, tm=tm,
                                       gather=layer_shards(l + 1) if l + 1 < DEPTH else ())
        saved.append(dict(x=x, q=q, k=k, v=v, fl=fl, z=z, sgu=sgu, o=o, lse=lse[:, :2, :], x1=x1, gu=gu))
        x = x2

    dx, loss, dgfin = _loss_bwd(x, tgt, small["final_norm_g"][None, :], tm=tm)
    grads = {n: [None] * DEPTH for n in SMALL if n != "final_norm_g"}
    parts = [None] * DEPTH
    pending = ()
    for l in reversed(range(DEPTH)):
        lw, sv = layers[l], saved[l]
        (dx1, dx2b, a, dgu, xn2, mrg, dx1b, do, delta, dsgu, dgffn, dgout, *landed) = _bwd_b(
            dx, sv["x1"], sv["gu"], sv["o"], sv["sgu"], lw["gout"], lw["gffn"], lw["w_out"], lw["w_gu"], lw["w_dn"],
            tm=tm, scatter=pending)
        if pending:
            parts[l + 1] = landed
        big_grads = {"w_down": _tn_matmul(a, dx2b, bt=tq), "w_gate_up": _tn_matmul(xn2, dgu, bt=tq),
                     "w_out": _tn_matmul(mrg, dx1b, bt=tq)}
        dqt, dk, dv, dck, dcq = _attn_bwd(sv["q"], sv["k"], sv["v"], do, sv["lse"],
                                          delta.reshape(HEAD_PAIRS, 2, t), tq=tq)
        early = [_split(big_grads[n], n).astype(BF16) for n in BIG[1:]] if l == 0 else ()
        (dx, xn, dqkv, dflb, dzb, dgmix, dbf, dlng, dlnb, dws, dbs, *landed) = _bwd_a(
            dx1, sv["x"], sv["z"], sv["fl"], dsgu, dqt, dk, dv, (dck + dcq).reshape(N_HEADS, t), lw["gmix"],
            lw["wqkv"], lw["wf"], lw["wz"], lw["lng"], lw["lnb"], lw["wsm"], lw["wsm_t"], lw["bsf"], mask, tm=tm,
            scatter=early)
        big_grads["w_in"] = jnp.concatenate(
            [_tn_matmul(xn, dqkv, bt=tq), _tn_matmul(xn, dflb, bt=tq)[:, :N_HEADS], _tn_matmul(xn, dzb, bt=tq)], axis=1)
        pending = [_split(big_grads[n], n).astype(BF16) for n in (BIG[:1] if l == 0 else BIG)]
        grads["mix_norm_g"][l] = dgmix[0]
        grads["b_f"][l] = dbf[0, :N_HEADS]
        grads["sgu_ln_g"][l] = dlng[0]
        grads["sgu_ln_b"][l] = dlnb[0]
        grads["w_s"][l] = dws
        grads["b_s"][l] = dbs[:, :N_GROUPS].T
        grads["out_norm_g"][l] = dgout[0]
        grads["ffn_norm_g"][l] = dgffn[0]
    grads = {n: jnp.stack(g) for n, g in grads.items()}
    grads["final_norm_g"] = dgfin[0]
    first, small_parts = _exchange(pending + [_pack(grads)], [True, False], name="scatter_grads")
    parts[0] = [first] + landed
    big_parts = {}
    for a, n in enumerate(BIG):
        stacked = jnp.stack([parts[l][a] for l in range(DEPTH)], axis=1)
        big_parts[n] = stacked.reshape(N_DEV, -1, stacked.shape[-1])
    return loss[0, 0], dx, big_parts, small_parts


SMALL = ("mix_norm_g", "b_f", "sgu_ln_g", "sgu_ln_b", "w_s", "b_s", "out_norm_g", "ffn_norm_g", "final_norm_g")
BIG = ("w_in", "w_out", "w_gate_up", "w_down")
WEIGHTS = ("mix_norm_g", "w_in", "b_f", "sgu_ln_g", "sgu_ln_b", "w_s", "b_s", "out_norm_g", "w_out", "ffn_norm_g",
           "w_gate_up", "w_down", "final_norm_g")
SHARD_AXIS = {"w_in": 1, "w_out": 0, "w_gate_up": 1, "w_down": 0}


def _assemble(gathered, name):
    if SHARD_AXIS[name] == 0:
        return gathered.reshape(-1, gathered.shape[-1])
    return gathered.transpose(1, 0, 2).reshape(gathered.shape[1], -1)


def _split(full, name):
    rows, cols = full.shape
    if SHARD_AXIS[name] == 0:
        return full.reshape(N_DEV, rows // N_DEV, cols)
    return full.reshape(rows, N_DEV, cols // N_DEV).transpose(1, 0, 2)


def _pack(tree):
    flat = jnp.concatenate([tree[n].reshape(-1) for n in SMALL])
    pad = (-flat.shape[0]) % (8 * LANES)
    return jnp.pad(flat, (0, pad)).reshape(-1, LANES)


def _unpack(packed, like):
    flat = packed.reshape(-1)
    out, at = {}, 0
    for n in SMALL:
        size = like[n].size
        out[n] = flat[at:at + size].reshape(like[n].shape)
        at += size
    return out


def kernel(x, mix_norm_g, w_in, b_f, sgu_ln_g, sgu_ln_b, w_s, b_s, out_norm_g, w_out, ffn_norm_g, w_gate_up, w_down, final_norm_g, loss_target, m_mix_norm_g, m_w_in, m_b_f, m_sgu_ln_g, m_sgu_ln_b, m_w_s, m_b_s, m_out_norm_g, m_w_out, m_ffn_norm_g, m_w_gate_up, m_w_down, m_final_norm_g, v_mix_norm_g, v_w_in, v_b_f, v_sgu_ln_g, v_sgu_ln_b, v_w_s, v_b_s, v_out_norm_g, v_w_out, v_ffn_norm_g, v_w_gate_up, v_w_down, v_final_norm_g):
    w = dict(mix_norm_g=mix_norm_g, w_in=w_in, b_f=b_f, sgu_ln_g=sgu_ln_g, sgu_ln_b=sgu_ln_b, w_s=w_s, b_s=b_s,
             out_norm_g=out_norm_g, w_out=w_out, ffn_norm_g=ffn_norm_g, w_gate_up=w_gate_up, w_down=w_down,
             final_norm_g=final_norm_g)
    m = dict(mix_norm_g=m_mix_norm_g, w_in=m_w_in, b_f=m_b_f, sgu_ln_g=m_sgu_ln_g, sgu_ln_b=m_sgu_ln_b, w_s=m_w_s,
             b_s=m_b_s, out_norm_g=m_out_norm_g, w_out=m_w_out, ffn_norm_g=m_ffn_norm_g, w_gate_up=m_w_gate_up,
             w_down=m_w_down, final_norm_g=m_final_norm_g)
    v = dict(mix_norm_g=v_mix_norm_g, w_in=v_w_in, b_f=v_b_f, sgu_ln_g=v_sgu_ln_g, sgu_ln_b=v_sgu_ln_b, w_s=v_w_s,
             b_s=v_b_s, out_norm_g=v_out_norm_g, w_out=v_w_out, ffn_norm_g=v_ffn_norm_g, w_gate_up=v_w_gate_up,
             w_down=v_w_down, final_norm_g=v_final_norm_g)

    loss, dx, big_parts, small_parts = _step(x[0], loss_target[0], {n: w[n] for n in SMALL},
                                             {n: w[n].astype(BF16) for n in BIG})
    loss = lax.psum(loss, ("x", "y", "c"))

    g_out, d_out, m_out, v_out = {}, {}, {}, {}
    for n in BIG:
        shape = w[n].shape
        two_d = lambda a: a.reshape(-1, shape[-1])
        res = _adamw(big_parts[n], two_d(w[n]), two_d(m[n]), two_d(v[n]), name=f"adamw_{n}")
        g_out[n], d_out[n], m_out[n], v_out[n] = (r.reshape(shape) for r in res)
    res = _adamw(small_parts, _pack(w), _pack(m), _pack(v), name="adamw_small")
    for dst, packed in zip((g_out, d_out, m_out, v_out), res):
        dst.update(_unpack(packed, w))

    return (loss, dx[None], *[g_out[n] for n in WEIGHTS], *[d_out[n] for n in WEIGHTS],
            *[m_out[n] for n in WEIGHTS], *[v_out[n] for n in WEIGHTS])
```

```python
import functools
import math

import jax
import jax.numpy as jnp
from jax import lax
from jax.experimental import pallas as pl
from jax.experimental.pallas import tpu as pltpu

F32, BF16 = jnp.float32, jnp.bfloat16
HIGHEST = lax.Precision.HIGHEST
MESH = pl.DeviceIdType.MESH
ANY = pl.BlockSpec(memory_space=pl.ANY)
SDS = jax.ShapeDtypeStruct

N_DEV = 8
DEPTH = 4
D_MODEL = 1024
D_ATTN = 512
D_SGU = 512
N_HEADS = 8
HEAD_DIM = 64
HEAD_PAIRS = N_HEADS // 2
SGU_CHUNK = 128
SGU_BLOCK = 64
N_GROUPS = 8
GROUP_DIM = 64
D_FF = 2816
FF_CHUNK = 1408
N_FF_CHUNKS = D_FF // FF_CHUNK
D_IN = 3 * D_ATTN + N_HEADS + 2 * D_SGU
LANES = 128
D_IN_PADDED = 3 * D_ATTN + LANES + 2 * D_SGU
EPS = 1e-6
QK_SCALE = HEAD_DIM ** -0.5
INV_SQRT2 = 1.0 / math.sqrt(2.0)
INV_SQRT_2PI = 1.0 / math.sqrt(2.0 * math.pi)
LOG2E = 1.0 / math.log(2.0)
LN2 = math.log(2.0)
ROW_CHUNK = 32
KX_ROWS = LANES + 16

ADAM_LR = 0.001
ADAM_B1 = 0.9
ADAM_B2 = 0.999
ADAM_EPS = 1e-08
ADAM_WD = 0.01
ADAM_STEP = 10

VMEM_LIMIT_BYTES = 56 * 1024 * 1024


def _params(*sem):
    return pltpu.CompilerParams(dimension_semantics=sem or None, vmem_limit_bytes=VMEM_LIMIT_BYTES)


def _dot(a, b):
    return jnp.dot(a, b, preferred_element_type=F32)


def _dot_nt(a, b):
    return lax.dot_general(a, b, (((1,), (1,)), ((), ())), preferred_element_type=F32)


def _dot_tn(a, b):
    return lax.dot_general(a, b, (((0,), (0,)), ((), ())), preferred_element_type=F32)


def _dot_exact(a, b):
    return jnp.dot(a, b, precision=HIGHEST, preferred_element_type=F32)


def _mean(v):
    return jnp.mean(v, axis=-1, keepdims=True)


def _sigmoid(v):
    return 1.0 / (1.0 + jnp.exp(-v))


def _row_spec(tm, n):
    return pl.BlockSpec((tm, n), lambda i: (i, 0))


def _rev_spec(tm, n, nt):
    return pl.BlockSpec((tm, n), lambda i: (nt - 1 - i, 0))


def _const_spec(shape):
    return pl.BlockSpec(shape, lambda i: (0,) * len(shape))


def _tiles(t):
    return min(256, t), min(512, t), min(2048, t)


def _group_indicator():
    r = lax.broadcasted_iota(jnp.int32, (D_ATTN, LANES), 0)
    c = lax.broadcasted_iota(jnp.int32, (D_ATTN, LANES), 1)
    return ((r >> 6) == c).astype(F32)


def _sgu_mix(w_ref, zc, lane_grp):
    out = jnp.zeros((SGU_CHUNK, D_SGU), F32)
    for g in range(N_GROUPS):
        out = out + jnp.where(lane_grp == g, _dot(w_ref[g], zc), 0.0)
    return out


def _fwd_a(x, gmix, w_in, bf, lng, lnb, wsm, bsf, *, tm, gather=()):
    t = x.shape[0]
    nt = t // tm
    nch = tm // SGU_CHUNK

    def body(x_ref, gmix_ref, w_ref, bf_ref, lng_ref, lnb_ref, wsm_ref, bsf_ref,
             q_ref, k_ref, v_ref, fl_ref, z_ref, sgu_ref, carry_ref):
        @pl.when(pl.program_id(0) == 0)
        def _():
            carry_ref[...] = jnp.zeros_like(carry_ref)

        xt = x_ref[...]
        r = lax.rsqrt(_mean(xt * xt) + EPS)
        xn = ((xt * r) * gmix_ref[...]).astype(BF16)
        proj = _dot(xn, w_ref[...])
        qkv = proj[:, :3 * D_ATTN]

        fl = proj[:, 3 * D_ATTN:3 * D_ATTN + LANES] + bf_ref[...]
        fl_ref[...] = fl
        logf = jnp.minimum(fl, 0.0) - jnp.log1p(jnp.exp(-jnp.abs(fl)))
        row = lax.broadcasted_iota(jnp.int32, (tm, tm), 0)
        col = lax.broadcasted_iota(jnp.int32, (tm, tm), 1)
        c = _dot_exact((col <= row).astype(F32), logf) + carry_ref[...]
        carry_ref[...] = c[tm - 1:tm, :]

        c2 = c * LOG2E
        lane = lax.broadcasted_iota(jnp.int32, (tm, LANES), 1)
        for h in range(N_HEADS):
            pair, hh = divmod(h, 2)
            base = _aug_lane(hh)
            in_head = (lane >= hh * HEAD_DIM) & (lane < (hh + 1) * HEAD_DIM)
            col_h = jnp.sum(jnp.where(lane == h, c2, 0.0), axis=1, keepdims=True)
            hi = col_h.astype(BF16).astype(F32)
            mid = (col_h - hi).astype(BF16).astype(F32)
            lo = (col_h - hi) - mid
            split = jnp.where(lane == base, hi, jnp.where(lane == base + 1, mid, jnp.where(lane == base + 2, lo, 0.0)))
            split_k = jnp.where(lane == base + 3, hi, jnp.where(lane == base + 4, mid,
                                                                 jnp.where(lane == base + 5, lo, 0.0)))
            ones_q = ((lane >= base + 3) & (lane < base + 6)).astype(F32)
            ones_k = ((lane >= base) & (lane < base + 3)).astype(F32)
            blk = slice(pair * LANES, (pair + 1) * LANES)
            q_ref[h] = jnp.where(in_head, qkv[:, blk] * (QK_SCALE * LOG2E), split + ones_q).astype(BF16)
            k_ref[h] = jnp.where(in_head, qkv[:, D_ATTN:2 * D_ATTN][:, blk], ones_k - split_k).astype(BF16)
            v_ref[h] = jnp.where(in_head, qkv[:, 2 * D_ATTN:][:, blk], (lane == base).astype(F32)).astype(BF16)

        z = proj[:, 3 * D_ATTN + LANES:]
        z_ref[...] = z
        zg = 0.5 * z * (1.0 + lax.erf(z * INV_SQRT2))
        zu = zg[:, :D_SGU]
        zv = zg[:, D_SGU:]
        xc = zv - _mean(zv)
        zvn = ((xc * lax.rsqrt(_mean(xc * xc) + EPS)) * lng_ref[...] + lnb_ref[...]).astype(BF16)
        lane_grp = lax.broadcasted_iota(jnp.int32, (SGU_CHUNK, D_SGU), 1) >> 6
        for ch in range(nch):
            rows = slice(ch * SGU_CHUNK, (ch + 1) * SGU_CHUNK)
            mixed = _sgu_mix(wsm_ref, zvn[rows, :], lane_grp) + bsf_ref[...]
            sgu_ref[rows, :] = zu[rows, :] * mixed

    head_spec = pl.BlockSpec((N_HEADS, tm, LANES), lambda i: (0, i, 0))
    return _row_tile_call(
        body, "fwd_a", nt, (x, gmix, w_in, bf, lng, lnb, wsm, bsf), exchange=gather, scatter=False,
        in_specs=[_row_spec(tm, D_MODEL), _const_spec((1, D_MODEL)), _const_spec((D_MODEL, D_IN_PADDED)),
                  _const_spec((1, LANES)), _const_spec((1, D_SGU)), _const_spec((1, D_SGU)),
                  _const_spec((N_GROUPS, SGU_CHUNK, SGU_CHUNK)), _const_spec((SGU_CHUNK, D_SGU))],
        out_specs=[head_spec, head_spec, head_spec, _row_spec(tm, LANES), _row_spec(tm, 2 * D_SGU),
                   _row_spec(tm, D_SGU)],
        out_shape=[SDS((N_HEADS, t, LANES), BF16)] * 3 + [SDS((t, LANES), F32), SDS((t, 2 * D_SGU), F32),
                                                          SDS((t, D_SGU), F32)],
        scratch_shapes=[pltpu.VMEM((1, LANES), F32)])


def _aug_lane(hh):
    return (1 - hh) * HEAD_DIM


def _attn_fwd(qa, ka, va, *, tq):
    t = qa.shape[1]
    nq = t // tq
    nrc = tq // ROW_CHUNK

    def body(q_ref, k_hbm, v_hbm, o_ref, lse_ref, k_vm, v_vm, s_ref, p_ref, m_ref, a_ref, acc_ref):
        p = pl.program_id(0)
        i = pl.program_id(1)

        @pl.when(i == 0)
        def _():
            pltpu.sync_copy(k_hbm.at[pl.ds(2 * p, 2)], k_vm)
            pltpu.sync_copy(v_hbm.at[pl.ds(2 * p, 2)], v_vm)

        m_ref[...] = jnp.full(m_ref.shape, -jnp.inf, F32)
        acc_ref[...] = jnp.zeros_like(acc_ref)
        rowq = lax.broadcasted_iota(jnp.int32, (ROW_CHUNK, tq), 0)
        colk = lax.broadcasted_iota(jnp.int32, (ROW_CHUNK, tq), 1)

        def scores(j, slot):
            start = pl.multiple_of(j * tq, tq)
            for h in range(2):
                s_ref[slot, h] = _dot_nt(q_ref[h], k_vm[h, pl.ds(start, tq), :])

        def softmax(slot, masked):
            for h in range(2):
                for r in range(nrc):
                    rows = slice(r * ROW_CHUNK, (r + 1) * ROW_CHUNK)
                    sc = s_ref[slot, h, rows, :]
                    if masked:
                        sc = jnp.where(colk <= rowq + r * ROW_CHUNK, sc, -jnp.inf)
                    m_old = m_ref[h, rows, :]
                    m_new = jnp.maximum(m_old, jnp.max(sc, axis=1, keepdims=True))
                    p_ref[slot, h, rows, :] = jnp.exp2(sc - m_new).astype(BF16)
                    a_ref[slot, h, rows, :] = jnp.exp2(m_old - m_new)
                    m_ref[h, rows, :] = m_new

        def accumulate(j, slot):
            start = pl.multiple_of(j * tq, tq)
            for h in range(2):
                acc_ref[h] = acc_ref[h] * a_ref[slot, h] + _dot(p_ref[slot, h], v_vm[h, pl.ds(start, tq), :])

        scores(0, 0)

        @pl.when(i > 0)
        def _():
            scores(1, 1)
            softmax(0, False)

        def stage(j, slot):
            scores(j + 1, 1 - slot)
            softmax(slot, False)
            accumulate(j - 1, 1 - slot)

        def pair_body(n, carry):
            j = 1 + 2 * n
            stage(j, 1)
            stage(j + 1, 0)
            return carry

        rest = i - 1
        lax.fori_loop(0, rest // 2, pair_body, 0)

        @pl.when((rest > 0) & (lax.rem(rest, 2) == 1))
        def _():
            stage(i - 1, 1)

        @pl.when(lax.rem(i, 2) == 0)
        def _():
            softmax(0, True)

            @pl.when(i > 0)
            def _():
                accumulate(i - 1, 1)

            accumulate(i, 0)

        @pl.when(lax.rem(i, 2) == 1)
        def _():
            softmax(1, True)
            accumulate(i - 1, 0)
            accumulate(i, 1)

        lane = lax.broadcasted_iota(jnp.int32, (tq, LANES), 1)
        l_h = [jnp.sum(jnp.where(lane == _aug_lane(h), acc_ref[h], 0.0), axis=1, keepdims=True) for h in range(2)]
        o_ref[...] = jnp.where(lane < HEAD_DIM, acc_ref[0] / l_h[0], acc_ref[1] / l_h[1])
        lse = [m_ref[h] + jnp.log2(l_h[h]) for h in range(2)]
        lse_cols = jnp.where(lane == 0, lse[0], jnp.where(lane == 1, lse[1], 0.0))
        lse_ref[0] = lse_cols.T[:8, :]

    return pl.pallas_call(
        body, name="attn_fwd", grid=(HEAD_PAIRS, nq),
        in_specs=[pl.BlockSpec((2, tq, LANES), lambda p, i: (p, i, 0)), ANY, ANY],
        out_specs=[pl.BlockSpec((tq, LANES), lambda p, i: (i, p)),
                   pl.BlockSpec((1, 8, tq), lambda p, i: (p, 0, i))],
        out_shape=[SDS((t, D_ATTN), F32), SDS((HEAD_PAIRS, 8, t), F32)],
        scratch_shapes=[pltpu.VMEM((2, t, LANES), BF16), pltpu.VMEM((2, t, LANES), BF16),
                        pltpu.VMEM((2, 2, tq, tq), F32), pltpu.VMEM((2, 2, tq, tq), BF16),
                        pltpu.VMEM((2, tq, 1), F32), pltpu.VMEM((2, 2, tq, 1), F32), pltpu.VMEM((2, tq, LANES), F32)],
        compiler_params=_params("arbitrary", "arbitrary"),
    )(qa, ka, va)


def _attn_bwd(qa, ka, va, do, lse_row, delta_row, *, tq):
    t = qa.shape[1]
    nq = t // tq
    nrc = tq // ROW_CHUNK

    def body(q_hbm, do_ref, k_ref, v_ref, lse_ref, dl_ref, dqt_ref, dk_ref, dv_ref, dck_ref, dcq_ref,
             q_vm, st_ref, dp_ref, pt_ref, ds_ref, dka_ref, dva_ref, vh_ref, kx_ref):
        p = pl.program_id(0)
        j = pl.program_id(1)

        @pl.when(j == 0)
        def _():
            pltpu.sync_copy(q_hbm.at[pl.ds(2 * p, 2)], q_vm)
            dqt_ref[...] = jnp.zeros_like(dqt_ref)
            dcq_ref[...] = jnp.zeros_like(dcq_ref)

        dka_ref[...] = jnp.zeros_like(dka_ref)
        dva_ref[...] = jnp.zeros_like(dva_ref)
        lane = lax.broadcasted_iota(jnp.int32, (tq, LANES), 1)
        in_head = (lane < HEAD_DIM, lane >= HEAD_DIM)
        for h in range(2):
            zero = jnp.zeros((tq, LANES), BF16)
            vh_ref[h] = jnp.where(in_head[h], v_ref[h], zero)
            kx_ref[h, :LANES, :] = jnp.where(in_head[h], k_ref[h], zero).astype(F32).T.astype(BF16)
            kx_ref[h, LANES:, :] = jnp.ones((KX_ROWS - LANES, tq), BF16)
        rowk = lax.broadcasted_iota(jnp.int32, (ROW_CHUNK, tq), 0)
        colq = lax.broadcasted_iota(jnp.int32, (ROW_CHUNK, tq), 1)

        def step(i, masked, slot):
            start = pl.multiple_of(i * tq, tq)
            cols = pl.ds(start, tq)
            do2 = do_ref[pl.ds(start, tq), :]
            q_h = [q_vm[h, pl.ds(start, tq), :] for h in range(2)]
            for h in range(2):
                st_ref[slot, h] = _dot_nt(k_ref[h], q_h[h])
                dp_ref[slot, h] = _dot_nt(vh_ref[h], do2)
            for h in range(2):
                lse = lse_ref[0, h:h + 1, cols]
                delta = dl_ref[0, h:h + 1, cols]
                for r in range(nrc):
                    rows = slice(r * ROW_CHUNK, (r + 1) * ROW_CHUNK)
                    st = st_ref[slot, h, rows, :]
                    if masked:
                        st = jnp.where(rowk + r * ROW_CHUNK <= colq, st, -jnp.inf)
                    pt = jnp.exp2(st - lse)
                    pt_ref[slot, h, rows, :] = pt.astype(BF16)
                    ds_ref[slot, h, rows, :] = (pt * (dp_ref[slot, h, rows, :] - delta)).astype(BF16)
            dq_t = jnp.zeros((LANES, tq), F32)
            for h in range(2):
                dva_ref[h] += _dot(pt_ref[slot, h], do2)
                dka_ref[h] += _dot(ds_ref[slot, h], q_h[h])
                ext = _dot(kx_ref[h], ds_ref[slot, h])
                dq_t = dq_t + ext[:LANES, :]
                dcq_ref[0, h:h + 1, cols] += ext[LANES:LANES + 1, :]
            dqt_ref[0, :, cols] += dq_t

        def pair_body(n, carry):
            i = j + 1 + 2 * n
            step(i, False, 0)
            step(i + 1, False, 1)
            return carry

        step(j, True, 0)
        after = nq - 1 - j
        lax.fori_loop(0, after // 2, pair_body, 0)

        @pl.when(lax.rem(after, 2) == 1)
        def _():
            step(nq - 1, False, 0)

        dk_ref[...] = (jnp.where(in_head[0], dka_ref[0], dka_ref[1]) * LN2).astype(BF16)
        dv_ref[...] = jnp.where(in_head[0], dva_ref[0], dva_ref[1]).astype(BF16)
        own = pl.ds(pl.multiple_of(j * tq, tq), tq)
        for h in range(2):
            at = _aug_lane(h) + 3
            dck_ref[0, h:h + 1, own] = -dka_ref[h].T[at:at + 1, :]

    rows = pl.BlockSpec((1, 2, t), lambda p, j: (p, 0, 0))
    tiles = pl.BlockSpec((1, LANES, t), lambda p, j: (p, 0, 0))
    return pl.pallas_call(
        body, name="attn_bwd", grid=(HEAD_PAIRS, nq),
        in_specs=[ANY, pl.BlockSpec((t, LANES), lambda p, j: (0, p)),
                  pl.BlockSpec((2, tq, LANES), lambda p, j: (p, j, 0)),
                  pl.BlockSpec((2, tq, LANES), lambda p, j: (p, j, 0)), rows, rows],
        out_specs=[tiles,
                   pl.BlockSpec((tq, LANES), lambda p, j: (j, p)),
                   pl.BlockSpec((tq, LANES), lambda p, j: (j, p)), rows, rows],
        out_shape=[SDS((HEAD_PAIRS, LANES, t), F32), SDS((t, D_ATTN), BF16), SDS((t, D_ATTN), BF16),
                   SDS((HEAD_PAIRS, 2, t), F32), SDS((HEAD_PAIRS, 2, t), F32)],
        scratch_shapes=[pltpu.VMEM((2, t, LANES), BF16), pltpu.VMEM((2, 2, tq, tq), F32),
                        pltpu.VMEM((2, 2, tq, tq), F32), pltpu.VMEM((2, 2, tq, tq), BF16),
                        pltpu.VMEM((2, 2, tq, tq), BF16),
                        pltpu.VMEM((2, tq, LANES), F32), pltpu.VMEM((2, tq, LANES), F32),
                        pltpu.VMEM((2, tq, LANES), BF16), pltpu.VMEM((2, KX_ROWS, tq), BF16)],
        compiler_params=_params("arbitrary", "arbitrary"),
    )(qa, do, ka, va, lse_row, delta_row)

def _load_weights_once(pairs):
    @pl.when(pl.program_id(0) == 0)
    def _():
        for src, dst in pairs:
            pltpu.sync_copy(src, dst)


def _row_tile_call(body, name, nt, operands, *, in_specs, out_specs, out_shape, scratch_shapes, exchange, scatter):
    if not exchange:
        return pl.pallas_call(body, name=name, grid=(nt,), in_specs=in_specs, out_specs=out_specs,
                              out_shape=out_shape, scratch_shapes=scratch_shapes,
                              compiler_params=_params("arbitrary"))(*operands)
    flags = [scatter] * len(exchange)
    n = len(exchange)
    return pl.pallas_call(
        _fused_exchange(body, len(operands), len(out_shape), flags, nt),
        name=name + ("_scatter" if scatter else "_gather"), grid=(nt,),
        in_specs=list(in_specs) + [ANY] * n, out_specs=list(out_specs) + [ANY] * n,
        out_shape=list(out_shape) + _exchange_shapes(exchange, flags),
        scratch_shapes=list(scratch_shapes) + _exchange_scratch(n),
        compiler_params=_params("arbitrary"))(*operands, *exchange)


def _fwd_b(x, o, sgu, gout, gffn, w_out, w_gu, w_dn, *, tm, gather=()):
    t = x.shape[0]
    nt = t // tm

    def body(x_ref, o_ref, s_ref, gout_ref, gffn_ref, wout_hbm, wgu_hbm, wdn_hbm,
             x1_ref, x2_ref, gu_ref, wout, wgu, wdn):
        _load_weights_once(((wout_hbm, wout), (wgu_hbm, wgu), (wdn_hbm, wdn)))
        ov = o_ref[...]
        sv = s_ref[...]
        mo = ((ov * lax.rsqrt(_mean(ov * ov) + EPS)) * gout_ref[:, :D_ATTN]).astype(BF16)
        ms = ((sv * lax.rsqrt(_mean(sv * sv) + EPS)) * gout_ref[:, D_ATTN:]).astype(BF16)
        x1 = x_ref[...] + (_dot(mo, wout[:D_ATTN, :]) + _dot(ms, wout[D_ATTN:, :]))
        x1_ref[...] = x1
        xn2 = ((x1 * lax.rsqrt(_mean(x1 * x1) + EPS)) * gffn_ref[...]).astype(BF16)
        y = jnp.zeros((tm, D_MODEL), F32)
        for n in range(N_FF_CHUNKS):
            lo, hi = n * FF_CHUNK, (n + 1) * FF_CHUNK
            gate = _dot(xn2, wgu[:, lo:hi])
            up = _dot(xn2, wgu[:, D_FF + lo:D_FF + hi])
            gu_ref[:, lo:hi] = gate
            gu_ref[:, D_FF + lo:D_FF + hi] = up
            a = ((gate * _sigmoid(gate)) * up).astype(BF16)
            y = y + _dot(a, wdn[lo:hi, :])
        x2_ref[...] = x1 + y

    return _row_tile_call(
        body, "fwd_b", nt, (x, o, sgu, gout, gffn, w_out, w_gu, w_dn),
        in_specs=[_row_spec(tm, D_MODEL), _row_spec(tm, D_ATTN), _row_spec(tm, D_SGU),
                  _const_spec((1, D_MODEL)), _const_spec((1, D_MODEL)), ANY, ANY, ANY],
        out_specs=[_row_spec(tm, D_MODEL), _row_spec(tm, D_MODEL), _row_spec(tm, 2 * D_FF)],
        out_shape=[SDS((t, D_MODEL), F32), SDS((t, D_MODEL), F32), SDS((t, 2 * D_FF), F32)],
        scratch_shapes=[pltpu.VMEM((D_MODEL, D_MODEL), BF16), pltpu.VMEM((D_MODEL, 2 * D_FF), BF16),
                        pltpu.VMEM((D_FF, D_MODEL), BF16)],
        exchange=gather, scatter=False)


def _loss_bwd(x, tgt, gfin, *, tm):
    t = x.shape[0]
    nt = t // tm

    def body(x_ref, t_ref, g_ref, dx_ref, loss_ref, dg_ref):
        @pl.when(pl.program_id(0) == 0)
        def _():
            loss_ref[...] = jnp.zeros_like(loss_ref)
            dg_ref[...] = jnp.zeros_like(dg_ref)

        xt = x_ref[...]
        g = g_ref[...]
        r = lax.rsqrt(_mean(xt * xt) + EPS)
        xh = xt * r
        err = xh * g - t_ref[...]
        loss_ref[...] += 0.5 * jnp.sum(_mean(err * err), axis=0, keepdims=True)
        dy = err * (1.0 / D_MODEL)
        dg_ref[...] += jnp.sum(dy * xh, axis=0, keepdims=True)
        dyg = dy * g
        dx_ref[...] = r * (dyg - xh * _mean(dyg * xh))

    return pl.pallas_call(
        body, name="loss_bwd", grid=(nt,),
        in_specs=[_row_spec(tm, D_MODEL), _row_spec(tm, D_MODEL), _const_spec((1, D_MODEL))],
        out_specs=[_row_spec(tm, D_MODEL), _const_spec((1, 1)), _const_spec((1, D_MODEL))],
        out_shape=[SDS((t, D_MODEL), F32), SDS((1, 1), F32), SDS((1, D_MODEL), F32)],
        compiler_params=_params("arbitrary"),
    )(x, tgt, gfin)


def _bwd_b(dx2, x1, gu, o, sgu, gout, gffn, w_out, w_gu, w_dn, *, tm, scatter=()):
    t = dx2.shape[0]
    nt = t // tm

    def body(dx2_ref, x1_ref, gu_ref, o_ref, s_ref, gout_ref, gffn_ref, wout_hbm, wgu_hbm, wdn_hbm,
             dx1_ref, dx2b_ref, a_ref, dgu_ref, xn2_ref, mrg_ref, dx1b_ref, do_ref, dl_ref, dsgu_ref,
             dgffn_ref, dgout_ref, wout, wgu, wdn):
        _load_weights_once(((wout_hbm, wout), (wgu_hbm, wgu), (wdn_hbm, wdn)))

        @pl.when(pl.program_id(0) == 0)
        def _():
            dgffn_ref[...] = jnp.zeros_like(dgffn_ref)
            dgout_ref[...] = jnp.zeros_like(dgout_ref)

        dx2 = dx2_ref[...]
        dx2b = dx2.astype(BF16)
        dx2b_ref[...] = dx2b
        dxn2 = jnp.zeros((tm, D_MODEL), F32)
        for n in range(N_FF_CHUNKS):
            lo, hi = n * FF_CHUNK, (n + 1) * FF_CHUNK
            gate = gu_ref[:, lo:hi]
            up = gu_ref[:, D_FF + lo:D_FF + hi]
            sg = _sigmoid(gate)
            si = gate * sg
            a_ref[:, lo:hi] = (si * up).astype(BF16)
            d_a = _dot_nt(dx2b, wdn[lo:hi, :])
            dgb = ((d_a * up) * (sg * (1.0 + gate * (1.0 - sg)))).astype(BF16)
            dub = (d_a * si).astype(BF16)
            dgu_ref[:, lo:hi] = dgb
            dgu_ref[:, D_FF + lo:D_FF + hi] = dub
            dxn2 = dxn2 + (_dot_nt(dgb, wgu[:, lo:hi]) + _dot_nt(dub, wgu[:, D_FF + lo:D_FF + hi]))

        x1 = x1_ref[...]
        gffn = gffn_ref[...]
        r1 = lax.rsqrt(_mean(x1 * x1) + EPS)
        xh1 = x1 * r1
        xn2_ref[...] = (xh1 * gffn).astype(BF16)
        dgffn_ref[...] += jnp.sum(dxn2 * xh1, axis=0, keepdims=True)
        dyg = dxn2 * gffn
        dx1 = dx2 + r1 * (dyg - xh1 * _mean(dyg * xh1))
        dx1_ref[...] = dx1
        dx1b = dx1.astype(BF16)
        dx1b_ref[...] = dx1b

        ov = o_ref[...]
        sv = s_ref[...]
        go = gout_ref[:, :D_ATTN]
        gs = gout_ref[:, D_ATTN:]
        ro = lax.rsqrt(_mean(ov * ov) + EPS)
        rs = lax.rsqrt(_mean(sv * sv) + EPS)
        oh = ov * ro
        sh = sv * rs
        mrg_ref[:, :D_ATTN] = (oh * go).astype(BF16)
        mrg_ref[:, D_ATTN:] = (sh * gs).astype(BF16)
        dmo = _dot_nt(dx1b, wout[:D_ATTN, :])
        dms = _dot_nt(dx1b, wout[D_ATTN:, :])
        dgout_ref[:, :D_ATTN] += jnp.sum(dmo * oh, axis=0, keepdims=True)
        dgout_ref[:, D_ATTN:] += jnp.sum(dms * sh, axis=0, keepdims=True)
        dmog = dmo * go
        d_o = ro * (dmog - oh * _mean(dmog * oh))
        do_ref[...] = d_o.astype(BF16)
        dl_ref[...] = _dot_exact(d_o * ov, _group_indicator()).T[:N_HEADS, :]
        dmsg = dms * gs
        dsgu_ref[...] = rs * (dmsg - sh * _mean(dmsg * sh))

    return _row_tile_call(
        body, "bwd_b", nt, (dx2, x1, gu, o, sgu, gout, gffn, w_out, w_gu, w_dn),
        in_specs=[_row_spec(tm, D_MODEL), _row_spec(tm, D_MODEL), _row_spec(tm, 2 * D_FF), _row_spec(tm, D_ATTN),
                  _row_spec(tm, D_SGU), _const_spec((1, D_MODEL)), _const_spec((1, D_MODEL)), ANY, ANY, ANY],
        out_specs=[_row_spec(tm, D_MODEL), _row_spec(tm, D_MODEL), _row_spec(tm, D_FF), _row_spec(tm, 2 * D_FF),
                   _row_spec(tm, D_MODEL), _row_spec(tm, D_MODEL), _row_spec(tm, D_MODEL), _row_spec(tm, D_ATTN),
                   pl.BlockSpec((N_HEADS, tm), lambda i: (0, i)), _row_spec(tm, D_SGU),
                   _const_spec((1, D_MODEL)), _const_spec((1, D_MODEL))],
        out_shape=[SDS((t, D_MODEL), F32), SDS((t, D_MODEL), BF16), SDS((t, D_FF), BF16), SDS((t, 2 * D_FF), BF16),
                   SDS((t, D_MODEL), BF16), SDS((t, D_MODEL), BF16), SDS((t, D_MODEL), BF16),
                   SDS((t, D_ATTN), BF16), SDS((N_HEADS, t), F32), SDS((t, D_SGU), F32),
                   SDS((1, D_MODEL), F32), SDS((1, D_MODEL), F32)],
        scratch_shapes=[pltpu.VMEM((D_MODEL, D_MODEL), BF16), pltpu.VMEM((D_MODEL, 2 * D_FF), BF16),
                        pltpu.VMEM((D_FF, D_MODEL), BF16)],
        exchange=scatter, scatter=True)


def _bwd_a(dx1, x, z, fl, dsgu, dq, dk, dv, dc, gmix, w_in, lng, lnb, wsm, wsm_t, bsf, mask, *, tm, scatter=()):
    t = x.shape[0]
    nt = t // tm
    nch = tm // SGU_CHUNK

    def body(dx1_ref, x_ref, z_ref, fl_ref, dsgu_ref, dq_ref, dk_ref, dv_ref, dc_ref, gmix_ref, w_ref,
             lng_ref, lnb_ref, wsm_ref, wsmt_ref, bsf_ref, mask_ref,
             dx_ref, xn_ref, dh_ref, dgmix_ref, dbf_ref, dlng_ref, dlnb_ref, dws_ref, dbs_ref,
             carry_ref, dzvn_ref, dzu_ref, dbacc_ref):
        step = pl.program_id(0)

        @pl.when(step == 0)
        def _():
            carry_ref[...] = jnp.zeros_like(carry_ref)
            dbacc_ref[...] = jnp.zeros_like(dbacc_ref)
            for ref in (dgmix_ref, dbf_ref, dlng_ref, dlnb_ref, dws_ref):
                ref[...] = jnp.zeros_like(ref)

        z = z_ref[...]
        erf = lax.erf(z * INV_SQRT2)
        cdf = 0.5 * (1.0 + erf)
        zg = z * cdf
        zu = zg[:, :D_SGU]
        zv = zg[:, D_SGU:]
        xc = zv - _mean(zv)
        rln = lax.rsqrt(_mean(xc * xc) + EPS)
        zh = xc * rln
        lng = lng_ref[...]
        zvn = (zh * lng + lnb_ref[...]).astype(BF16)
        dsgu = dsgu_ref[...]
        lane_grp = lax.broadcasted_iota(jnp.int32, (SGU_CHUNK, D_SGU), 1) >> 6
        for ch in range(nch):
            rows = slice(ch * SGU_CHUNK, (ch + 1) * SGU_CHUNK)
            zc = zvn[rows, :]
            ds_c = dsgu[rows, :]
            mixed = _sgu_mix(wsm_ref, zc, lane_grp) + bsf_ref[...]
            dzu_ref[rows, :] = ds_c * mixed
            dmix = ds_c * zu[rows, :]
            dbacc_ref[...] += dmix
            dmb = dmix.astype(BF16)
            dzvn_ref[rows, :] = _sgu_mix(wsmt_ref, dmb, lane_grp)
            for g in range(N_GROUPS):
                dws_ref[g] += _dot_nt(jnp.where(lane_grp == g, dmb, jnp.zeros_like(dmb)), zc)
        dzvn = dzvn_ref[...]
        dlng_ref[...] += jnp.sum(dzvn * zh, axis=0, keepdims=True)
        dlnb_ref[...] += jnp.sum(dzvn, axis=0, keepdims=True)
        dzh = dzvn * lng
        dzv = rln * ((dzh - _mean(dzh)) - zh * _mean(dzh * zh))
        pdf = jnp.exp(-0.5 * (z * z)) * INV_SQRT_2PI
        dgelu = cdf + z * pdf
        z_at = 3 * D_ATTN + LANES
        dh_ref[:, z_at:z_at + D_SGU] = (dzu_ref[...] * dgelu[:, :D_SGU]).astype(BF16)
        dh_ref[:, z_at + D_SGU:] = (dzv * dgelu[:, D_SGU:]).astype(BF16)

        dc = jnp.concatenate([dc_ref[...], jnp.zeros((LANES - N_HEADS, tm), F32)], axis=0).T
        row = lax.broadcasted_iota(jnp.int32, (tm, tm), 0)
        col = lax.broadcasted_iota(jnp.int32, (tm, tm), 1)
        dlogf = _dot_exact((col >= row).astype(F32), dc) + carry_ref[...]
        carry_ref[...] = dlogf[0:1, :]
        dfl = dlogf * _sigmoid(-fl_ref[...])
        dbf_ref[...] += jnp.sum(dfl, axis=0, keepdims=True)
        dh_ref[:, 3 * D_ATTN:z_at] = dfl.astype(BF16)

        for pair in range(HEAD_PAIRS):
            dh_ref[:, pair * LANES:(pair + 1) * LANES] = (dq_ref[pair].T * QK_SCALE).astype(BF16)
        dh_ref[:, D_ATTN:2 * D_ATTN] = dk_ref[...]
        dh_ref[:, 2 * D_ATTN:3 * D_ATTN] = dv_ref[...]
        dxn = _dot_nt(dh_ref[...], w_ref[...])

        xt = x_ref[...]
        gmix = gmix_ref[...]
        r = lax.rsqrt(_mean(xt * xt) + EPS)
        xh = xt * r
        xn_ref[...] = (xh * gmix).astype(BF16)
        dgmix_ref[...] += jnp.sum(dxn * xh, axis=0, keepdims=True)
        dyg = dxn * gmix
        dx_ref[...] = dx1_ref[...] + r * (dyg - xh * _mean(dyg * xh))

        @pl.when(step == nt - 1)
        def _():
            for g in range(N_GROUPS):
                dws_ref[g] = dws_ref[g] * mask_ref[...]
            dbs_ref[...] = _dot_exact(dbacc_ref[...], _group_indicator())

    rev = functools.partial(_rev_spec, nt=nt)
    return _row_tile_call(
        body, "bwd_a", nt, (dx1, x, z, fl, dsgu, dq, dk, dv, dc, gmix, w_in, lng, lnb, wsm, wsm_t, bsf, mask),
        exchange=scatter, scatter=True,
        in_specs=[rev(tm, D_MODEL), rev(tm, D_MODEL), rev(tm, 2 * D_SGU), rev(tm, LANES), rev(tm, D_SGU),
                  pl.BlockSpec((HEAD_PAIRS, LANES, tm), lambda i: (0, 0, nt - 1 - i)), rev(tm, D_ATTN), rev(tm, D_ATTN),
                  pl.BlockSpec((N_HEADS, tm), lambda i: (0, nt - 1 - i)),
                  _const_spec((1, D_MODEL)), _const_spec((D_MODEL, D_IN_PADDED)),
                  _const_spec((1, D_SGU)), _const_spec((1, D_SGU)),
                  _const_spec((N_GROUPS, SGU_CHUNK, SGU_CHUNK)), _const_spec((N_GROUPS, SGU_CHUNK, SGU_CHUNK)),
                  _const_spec((SGU_CHUNK, D_SGU)), _const_spec((SGU_CHUNK, SGU_CHUNK))],
        out_specs=[rev(tm, D_MODEL), rev(tm, D_MODEL), rev(tm, D_IN_PADDED),
                   _const_spec((1, D_MODEL)), _const_spec((1, LANES)), _const_spec((1, D_SGU)), _const_spec((1, D_SGU)),
                   _const_spec((N_GROUPS, SGU_CHUNK, SGU_CHUNK)), _const_spec((SGU_CHUNK, LANES))],
        out_shape=[SDS((t, D_MODEL), F32), SDS((t, D_MODEL), BF16), SDS((t, D_IN_PADDED), BF16), SDS((1, D_MODEL), F32), SDS((1, LANES), F32), SDS((1, D_SGU), F32),
                   SDS((1, D_SGU), F32), SDS((N_GROUPS, SGU_CHUNK, SGU_CHUNK), F32), SDS((SGU_CHUNK, LANES), F32)],
        scratch_shapes=[pltpu.VMEM((1, LANES), F32), pltpu.VMEM((tm, D_SGU), F32), pltpu.VMEM((tm, D_SGU), F32),
                        pltpu.VMEM((SGU_CHUNK, D_SGU), F32)])


def _pick(n, cap):
    if n <= cap:
        return n
    best = LANES
    for cand in range(LANES, cap + 1, LANES):
        if n % cand == 0:
            best = cand
    return best


def _tn_matmul(a, b, *, bt):
    t, k1 = a.shape
    n = b.shape[1]
    bk = _pick(k1, 1408)
    bn = _pick(n, 1408)
    nsteps = t // bt

    def body(a_ref, b_ref, o_ref):
        @pl.when(pl.program_id(2) == 0)
        def _():
            o_ref[...] = jnp.zeros_like(o_ref)

        o_ref[...] += _dot_tn(a_ref[...], b_ref[...])

    return pl.pallas_call(
        body, name=f"wgrad_{k1}x{n}", grid=(k1 // bk, n // bn, nsteps),
        in_specs=[pl.BlockSpec((bt, bk), lambda i, j, s: (s, i)), pl.BlockSpec((bt, bn), lambda i, j, s: (s, j))],
        out_specs=pl.BlockSpec((bk, bn), lambda i, j, s: (i, j)),
        out_shape=SDS((k1, n), F32),
        compiler_params=_params("arbitrary", "arbitrary", "arbitrary"),
    )(a, b)


def _adamw(parts, w, m, v, *, name):
    rows, cols = w.shape
    br = _pick_rows(rows, cols)
    c1 = 1.0 - ADAM_B1 ** ADAM_STEP
    c2 = 1.0 - ADAM_B2 ** ADAM_STEP

    def body(p_ref, w_ref, m_ref, v_ref, g_ref, d_ref, nm_ref, nv_ref):
        g = p_ref[0].astype(F32)
        for j in range(1, N_DEV):
            g = g + p_ref[j].astype(F32)
        g_ref[...] = g
        nm = ADAM_B1 * m_ref[...] + (1.0 - ADAM_B1) * g
        nv = ADAM_B2 * v_ref[...] + (1.0 - ADAM_B2) * (g * g)
        nm_ref[...] = nm
        nv_ref[...] = nv
        d_ref[...] = -ADAM_LR * ((nm / c1) / (jnp.sqrt(nv / c2) + ADAM_EPS) + ADAM_WD * w_ref[...])

    spec = pl.BlockSpec((br, cols), lambda i: (i, 0))
    return pl.pallas_call(
        body, name=name, grid=(rows // br,),
        in_specs=[pl.BlockSpec((N_DEV, br, cols), lambda i: (0, i, 0)), spec, spec, spec],
        out_specs=[spec] * 4, out_shape=[SDS((rows, cols), F32)] * 4,
        compiler_params=_params("arbitrary"),
    )(parts, w, m, v)


def _pick_rows(rows, cols):
    target = max(8, (256 * 1024) // cols)
    best = 8
    for cand in range(8, min(rows, target) + 1, 8):
        if rows % cand == 0:
            best = cand
    return best


def _peer(k):
    x, y, c = lax.axis_index("x"), lax.axis_index("y"), lax.axis_index("c")
    px = 1 - x if k & 4 else x
    py = 1 - y if k & 2 else y
    pc = 1 - c if k & 1 else c
    return (px, py, pc), 4 * px + 2 * py + pc


def _exchange_scratch(n):
    return [pltpu.SemaphoreType.DMA((N_DEV - 1, n)), pltpu.SemaphoreType.DMA((N_DEV - 1, n)),
            pltpu.SemaphoreType.DMA((n,))]


def _exchange_copies(ins, outs, sems, scatter, landing):
    send_sems, recv_sems, local_sems = sems
    me = 4 * lax.axis_index("x") + 2 * lax.axis_index("y") + lax.axis_index("c")
    copies = [pltpu.make_async_copy(ins[a].at[me] if scatter[a] else ins[a], outs[a].at[me], local_sems.at[a])
              for a in range(len(ins))]
    for k in range(1, N_DEV):
        peer, pidx = _peer(k)
        for a in range(len(ins)):
            copies.append(pltpu.make_async_remote_copy(
                src_ref=ins[a].at[pidx] if scatter[a] else ins[a], dst_ref=outs[a].at[pidx if landing else me],
                send_sem=send_sems.at[k - 1, a], recv_sem=recv_sems.at[k - 1, a], device_id=peer, device_id_type=MESH))
    return copies


def _exchange_start(ins, outs, sems, scatter):
    for cp in _exchange_copies(ins, outs, sems, scatter, landing=False):
        cp.start()


def _exchange_wait(ins, outs, sems, scatter):
    for cp in _exchange_copies(ins, outs, sems, scatter, landing=True):
        cp.wait()


def _exchange_shapes(arrs, scatter):
    return [SDS(a.shape if sc else (N_DEV,) + a.shape, a.dtype) for a, sc in zip(arrs, scatter)]


def _exchange(arrs, scatter, *, name):
    n = len(arrs)

    def body(*refs):
        ins, outs, sems = refs[:n], refs[n:2 * n], refs[2 * n:]
        _exchange_start(ins, outs, sems, scatter)
        _exchange_wait(ins, outs, sems, scatter)

    return pl.pallas_call(
        body, name=name, in_specs=[ANY] * n, out_specs=[ANY] * n, out_shape=_exchange_shapes(arrs, scatter),
        scratch_shapes=_exchange_scratch(n),
    )(*arrs)


def _fused_exchange(body, n_in, n_out, scatter, nsteps):
    n = len(scatter)

    def wrapped(*refs):
        ins, ex_in = refs[:n_in], refs[n_in:n_in + n]
        outs, ex_out = refs[n_in + n:n_in + n + n_out], refs[n_in + n + n_out:n_in + 2 * n + n_out]
        scratch, sems = refs[n_in + 2 * n + n_out:-3], refs[-3:]

        @pl.when(pl.program_id(0) == 0)
        def _():
            _exchange_start(ex_in, ex_out, sems, scatter)

        body(*ins, *outs, *scratch)

        @pl.when(pl.program_id(0) == nsteps - 1)
        def _():
            _exchange_wait(ex_in, ex_out, sems, scatter)

    return wrapped


def _step(x, tgt, small, shards):
    t = x.shape[0]
    tm, tq, tw = _tiles(t)
    r = jnp.arange(SGU_CHUNK, dtype=jnp.int32) // SGU_BLOCK
    mask = (r[None, :] <= r[:, None]).astype(F32)
    layer_shards = lambda l: [shards[n][l] for n in BIG]

    layers = []
    saved = []
    gathered = _exchange(layer_shards(0)[:1], [False], name="gather_weights")
    for l in range(DEPTH):
        w_in = _assemble(gathered[0], BIG[0])
        w_pad = jnp.concatenate([w_in[:, :3 * D_ATTN + N_HEADS], jnp.zeros((D_MODEL, LANES - N_HEADS), BF16),
                                 w_in[:, 3 * D_ATTN + N_HEADS:]], axis=1)
        bf = jnp.pad(small["b_f"][l], (0, LANES - N_HEADS))[None, :]
        wsm = (small["w_s"][l] * mask[None]).astype(BF16)
        wsm_t = jnp.swapaxes(wsm, 1, 2)
        bsf = jnp.repeat(small["b_s"][l].T, GROUP_DIM, axis=1)
        lw = dict(w_in=w_pad, bf=bf, wsm=wsm, wsm_t=wsm_t, bsf=bsf,
                  gmix=small["mix_norm_g"][l][None, :], lng=small["sgu_ln_g"][l][None, :],
                  lnb=small["sgu_ln_b"][l][None, :], gout=small["out_norm_g"][l][None, :],
                  gffn=small["ffn_norm_g"][l][None, :])
        layers.append(lw)
        q, k, v, fl, z, sgu, *late = _fwd_a(x, lw["gmix"], w_pad, bf, lw["lng"], lw["lnb"], wsm, bsf, tm=tm,
                                                 gather=layer_shards(0)[1:] if l == 0 else ())
        w_out, w_gu, w_dn = (_assemble(g, n) for g, n in zip(late if l == 0 else gathered[1:], BIG[1:]))
        lw.update(w_out=w_out, w_gu=w_gu, w_dn=w_dn)
        o, lse = _attn_fwd(q, k, v, tq=tq)
        x1, x2, gu, *gathered = _fwd_b(x, o, sgu, lw["gout"], lw["gffn"], w_out, w_gu, w_dn, tm=tm,
                                       gather=layer_shards(l + 1) if l + 1 < DEPTH else ())
        saved.append(dict(x=x, q=q, k=k, v=v, fl=fl, z=z, sgu=sgu, o=o, lse=lse[:, :2, :], x1=x1, gu=gu))
        x = x2

    dx, loss, dgfin = _loss_bwd(x, tgt, small["final_norm_g"][None, :], tm=tm)
    grads = {n: [None] * DEPTH for n in SMALL if n != "final_norm_g"}
    parts = [None] * DEPTH
    pending = ()
    for l in reversed(range(DEPTH)):
        lw, sv = layers[l], saved[l]
        (dx1, dx2b, a, dgu, xn2, mrg, dx1b, do, delta, dsgu, dgffn, dgout, *landed) = _bwd_b(
            dx, sv["x1"], sv["gu"], sv["o"], sv["sgu"], lw["gout"], lw["gffn"], lw["w_out"], lw["w_gu"], lw["w_dn"],
            tm=tm, scatter=pending)
        if pending:
            parts[l + 1] = landed
        big_grads = {"w_down": _tn_matmul(a, dx2b, bt=tw), "w_gate_up": _tn_matmul(xn2, dgu, bt=tw),
                     "w_out": _tn_matmul(mrg, dx1b, bt=tw)}
        dqt, dk, dv, dck, dcq = _attn_bwd(sv["q"], sv["k"], sv["v"], do, sv["lse"],
                                          delta.reshape(HEAD_PAIRS, 2, t), tq=tq)
        early = [_split(big_grads[n], n).astype(BF16) for n in BIG[1:]] if l == 0 else ()
        (dx, xn, dh, dgmix, dbf, dlng, dlnb, dws, dbs, *landed) = _bwd_a(
            dx1, sv["x"], sv["z"], sv["fl"], dsgu, dqt, dk, dv, (dck + dcq).reshape(N_HEADS, t), lw["gmix"],
            lw["w_in"], lw["lng"], lw["lnb"], lw["wsm"], lw["wsm_t"], lw["bsf"], mask, tm=tm,
            scatter=early)
        g_pad = _tn_matmul(xn, dh, bt=tw)
        big_grads["w_in"] = jnp.concatenate([g_pad[:, :3 * D_ATTN + N_HEADS], g_pad[:, 3 * D_ATTN + LANES:]], axis=1)
        pending = [_split(big_grads[n], n).astype(BF16) for n in (BIG[:1] if l == 0 else BIG)]
        grads["mix_norm_g"][l] = dgmix[0]
        grads["b_f"][l] = dbf[0, :N_HEADS]
        grads["sgu_ln_g"][l] = dlng[0]
        grads["sgu_ln_b"][l] = dlnb[0]
        grads["w_s"][l] = dws
        grads["b_s"][l] = dbs[:, :N_GROUPS].T
        grads["out_norm_g"][l] = dgout[0]
        grads["ffn_norm_g"][l] = dgffn[0]
    grads = {n: jnp.stack(g) for n, g in grads.items()}
    grads["final_norm_g"] = dgfin[0]
    first, small_parts = _exchange(pending + [_pack(grads)], [True, False], name="scatter_grads")
    parts[0] = [first] + landed
    big_parts = {}
    for a, n in enumerate(BIG):
        stacked = jnp.stack([parts[l][a] for l in range(DEPTH)], axis=1)
        big_parts[n] = stacked.reshape(N_DEV, -1, stacked.shape[-1])
    return loss[0, 0], dx, big_parts, small_parts


SMALL = ("mix_norm_g", "b_f", "sgu_ln_g", "sgu_ln_b", "w_s", "b_s", "out_norm_g", "ffn_norm_g", "final_norm_g")
BIG = ("w_in", "w_out", "w_gate_up", "w_down")
WEIGHTS = ("mix_norm_g", "w_in", "b_f", "sgu_ln_g", "sgu_ln_b", "w_s", "b_s", "out_norm_g", "w_out", "ffn_norm_g",
           "w_gate_up", "w_down", "final_norm_g")
SHARD_AXIS = {"w_in": 1, "w_out": 0, "w_gate_up": 1, "w_down": 0}


def _assemble(gathered, name):
    if SHARD_AXIS[name] == 0:
        return gathered.reshape(-1, gathered.shape[-1])
    return gathered.transpose(1, 0, 2).reshape(gathered.shape[1], -1)


def _split(full, name):
    rows, cols = full.shape
    if SHARD_AXIS[name] == 0:
        return full.reshape(N_DEV, rows // N_DEV, cols)
    return full.reshape(rows, N_DEV, cols // N_DEV).transpose(1, 0, 2)


def _pack(tree):
    flat = jnp.concatenate([tree[n].reshape(-1) for n in SMALL])
    pad = (-flat.shape[0]) % (8 * LANES)
    return jnp.pad(flat, (0, pad)).reshape(-1, LANES)


def _unpack(packed, like):
    flat = packed.reshape(-1)
    out, at = {}, 0
    for n in SMALL:
        size = like[n].size
        out[n] = flat[at:at + size].reshape(like[n].shape)
        at += size
    return out


def kernel(x, mix_norm_g, w_in, b_f, sgu_ln_g, sgu_ln_b, w_s, b_s, out_norm_g, w_out, ffn_norm_g, w_gate_up, w_down, final_norm_g, loss_target, m_mix_norm_g, m_w_in, m_b_f, m_sgu_ln_g, m_sgu_ln_b, m_w_s, m_b_s, m_out_norm_g, m_w_out, m_ffn_norm_g, m_w_gate_up, m_w_down, m_final_norm_g, v_mix_norm_g, v_w_in, v_b_f, v_sgu_ln_g, v_sgu_ln_b, v_w_s, v_b_s, v_out_norm_g, v_w_out, v_ffn_norm_g, v_w_gate_up, v_w_down, v_final_norm_g):
    w = dict(mix_norm_g=mix_norm_g, w_in=w_in, b_f=b_f, sgu_ln_g=sgu_ln_g, sgu_ln_b=sgu_ln_b, w_s=w_s, b_s=b_s,
             out_norm_g=out_norm_g, w_out=w_out, ffn_norm_g=ffn_norm_g, w_gate_up=w_gate_up, w_down=w_down,
             final_norm_g=final_norm_g)
    m = dict(mix_norm_g=m_mix_norm_g, w_in=m_w_in, b_f=m_b_f, sgu_ln_g=m_sgu_ln_g, sgu_ln_b=m_sgu_ln_b, w_s=m_w_s,
             b_s=m_b_s, out_norm_g=m_out_norm_g, w_out=m_w_out, ffn_norm_g=m_ffn_norm_g, w_gate_up=m_w_gate_up,
             w_down=m_w_down, final_norm_g=m_final_norm_g)
    v = dict(mix_norm_g=v_mix_norm_g, w_in=v_w_in, b_f=v_b_f, sgu_ln_g=v_sgu_ln_g, sgu_ln_b=v_sgu_ln_b, w_s=v_w_s,
             b_s=v_b_s, out_norm_g=v_out_norm_g, w_out=v_w_out, ffn_norm_g=v_ffn_norm_g, w_gate_up=v_w_gate_up,
             w_down=v_w_down, final_norm_g=v_final_norm_g)

    loss, dx, big_parts, small_parts = _step(x[0], loss_target[0], {n: w[n] for n in SMALL},
                                             {n: w[n].astype(BF16) for n in BIG})
    loss = lax.psum(loss, ("x", "y", "c"))

    g_out, d_out, m_out, v_out = {}, {}, {}, {}
    for n in BIG:
        shape = w[n].shape
        two_d = lambda a: a.reshape(-1, shape[-1])
        res = _adamw(big_parts[n], two_d(w[n]), two_d(m[n]), two_d(v[n]), name=f"adamw_{n}")
        g_out[n], d_out[n], m_out[n], v_out[n] = (r.reshape(shape) for r in res)
    res = _adamw(small_parts, _pack(w), _pack(m), _pack(v), name="adamw_small")
    for dst, packed in zip((g_out, d_out, m_out, v_out), res):
        dst.update(_unpack(packed, w))

    return (loss, dx[None], *[g_out[n] for n in WEIGHTS], *[d_out[n] for n in WEIGHTS],
            *[m_out[n] for n in WEIGHTS], *[v_out[n] for n in WEIGHTS])
```

```python
import functools
import math

import jax
import jax.numpy as jnp
from jax import lax
from jax.experimental import pallas as pl
from jax.experimental.pallas import tpu as pltpu

F32, BF16 = jnp.float32, jnp.bfloat16
HIGHEST = lax.Precision.HIGHEST
MESH = pl.DeviceIdType.MESH
ANY = pl.BlockSpec(memory_space=pl.ANY)
SDS = jax.ShapeDtypeStruct

N_DEV = 8
DEPTH = 4
D_MODEL = 1024
D_ATTN = 512
D_SGU = 512
N_HEADS = 8
HEAD_DIM = 64
HEAD_PAIRS = N_HEADS // 2
SGU_CHUNK = 128
SGU_BLOCK = 64
N_GROUPS = 8
GROUP_DIM = 64
D_FF = 2816
FF_CHUNK = 1408
N_FF_CHUNKS = D_FF // FF_CHUNK
D_IN = 3 * D_ATTN + N_HEADS + 2 * D_SGU
LANES = 128
D_IN_PADDED = 3 * D_ATTN + LANES + 2 * D_SGU
EPS = 1e-6
QK_SCALE = HEAD_DIM ** -0.5
INV_SQRT2 = 1.0 / math.sqrt(2.0)
INV_SQRT_2PI = 1.0 / math.sqrt(2.0 * math.pi)
LOG2E = 1.0 / math.log(2.0)
LN2 = math.log(2.0)
ROW_CHUNK = 32
KX_ROWS = LANES + 16

ADAM_LR = 0.001
ADAM_B1 = 0.9
ADAM_B2 = 0.999
ADAM_EPS = 1e-08
ADAM_WD = 0.01
ADAM_STEP = 10

VMEM_LIMIT_BYTES = 56 * 1024 * 1024


def _params(*sem):
    return pltpu.CompilerParams(dimension_semantics=sem or None, vmem_limit_bytes=VMEM_LIMIT_BYTES)


def _dot(a, b):
    return jnp.dot(a, b, preferred_element_type=F32)


def _dot_nt(a, b):
    return lax.dot_general(a, b, (((1,), (1,)), ((), ())), preferred_element_type=F32)


def _dot_tn(a, b):
    return lax.dot_general(a, b, (((0,), (0,)), ((), ())), preferred_element_type=F32)


def _dot_exact(a, b):
    return jnp.dot(a, b, precision=HIGHEST, preferred_element_type=F32)


def _mean(v):
    return jnp.mean(v, axis=-1, keepdims=True)


def _sigmoid(v):
    return 1.0 / (1.0 + jnp.exp(-v))


def _row_spec(tm, n):
    return pl.BlockSpec((tm, n), lambda i: (i, 0))


def _rev_spec(tm, n, nt):
    return pl.BlockSpec((tm, n), lambda i: (nt - 1 - i, 0))


def _const_spec(shape):
    return pl.BlockSpec(shape, lambda i: (0,) * len(shape))


def _tiles(t):
    return min(256, t), min(512, t), min(2048, t)


def _group_indicator():
    r = lax.broadcasted_iota(jnp.int32, (D_ATTN, LANES), 0)
    c = lax.broadcasted_iota(jnp.int32, (D_ATTN, LANES), 1)
    return ((r >> 6) == c).astype(F32)


def _sgu_mix(w_ref, zc, lane_grp):
    out = jnp.zeros((SGU_CHUNK, D_SGU), F32)
    for g in range(N_GROUPS):
        out = out + jnp.where(lane_grp == g, _dot(w_ref[g], zc), 0.0)
    return out


def _fwd_a(x, gmix, w_in, bf, lng, lnb, wsm, bsf, *, tm, gather=()):
    t = x.shape[0]
    nt = t // tm
    nch = tm // SGU_CHUNK

    def body(x_ref, gmix_ref, w_ref, bf_ref, lng_ref, lnb_ref, wsm_ref, bsf_ref,
             q_ref, k_ref, v_ref, fl_ref, z_ref, sgu_ref, carry_ref):
        @pl.when(pl.program_id(0) == 0)
        def _():
            carry_ref[...] = jnp.zeros_like(carry_ref)

        xt = x_ref[...]
        r = lax.rsqrt(_mean(xt * xt) + EPS)
        xn = ((xt * r) * gmix_ref[...]).astype(BF16)
        proj = _dot(xn, w_ref[...])
        qkv = proj[:, :3 * D_ATTN]

        fl = proj[:, 3 * D_ATTN:3 * D_ATTN + LANES] + bf_ref[...]
        fl_ref[...] = fl
        logf = jnp.minimum(fl, 0.0) - jnp.log1p(jnp.exp(-jnp.abs(fl)))
        row = lax.broadcasted_iota(jnp.int32, (tm, tm), 0)
        col = lax.broadcasted_iota(jnp.int32, (tm, tm), 1)
        c = _dot_exact((col <= row).astype(F32), logf) + carry_ref[...]
        carry_ref[...] = c[tm - 1:tm, :]

        c2 = c * LOG2E
        lane = lax.broadcasted_iota(jnp.int32, (tm, LANES), 1)
        for h in range(N_HEADS):
            pair, hh = divmod(h, 2)
            base = _aug_lane(hh)
            in_head = (lane >= hh * HEAD_DIM) & (lane < (hh + 1) * HEAD_DIM)
            col_h = jnp.sum(jnp.where(lane == h, c2, 0.0), axis=1, keepdims=True)
            hi = col_h.astype(BF16).astype(F32)
            mid = (col_h - hi).astype(BF16).astype(F32)
            lo = (col_h - hi) - mid
            split = jnp.where(lane == base, hi, jnp.where(lane == base + 1, mid, jnp.where(lane == base + 2, lo, 0.0)))
            split_k = jnp.where(lane == base + 3, hi, jnp.where(lane == base + 4, mid,
                                                                 jnp.where(lane == base + 5, lo, 0.0)))
            ones_q = ((lane >= base + 3) & (lane < base + 6)).astype(F32)
            ones_k = ((lane >= base) & (lane < base + 3)).astype(F32)
            blk = slice(pair * LANES, (pair + 1) * LANES)
            q_ref[h] = jnp.where(in_head, qkv[:, blk] * (QK_SCALE * LOG2E), split + ones_q).astype(BF16)
            k_ref[h] = jnp.where(in_head, qkv[:, D_ATTN:2 * D_ATTN][:, blk], ones_k - split_k).astype(BF16)
            v_ref[h] = jnp.where(in_head, qkv[:, 2 * D_ATTN:][:, blk], (lane == base).astype(F32)).astype(BF16)

        z = proj[:, 3 * D_ATTN + LANES:]
        z_ref[...] = z
        zg = 0.5 * z * (1.0 + lax.erf(z * INV_SQRT2))
        zu = zg[:, :D_SGU]
        zv = zg[:, D_SGU:]
        xc = zv - _mean(zv)
        zvn = ((xc * lax.rsqrt(_mean(xc * xc) + EPS)) * lng_ref[...] + lnb_ref[...]).astype(BF16)
        lane_grp = lax.broadcasted_iota(jnp.int32, (SGU_CHUNK, D_SGU), 1) >> 6
        for ch in range(nch):
            rows = slice(ch * SGU_CHUNK, (ch + 1) * SGU_CHUNK)
            mixed = _sgu_mix(wsm_ref, zvn[rows, :], lane_grp) + bsf_ref[...]
            sgu_ref[rows, :] = zu[rows, :] * mixed

    head_spec = pl.BlockSpec((N_HEADS, tm, LANES), lambda i: (0, i, 0))
    return _row_tile_call(
        body, "fwd_a", nt, (x, gmix, w_in, bf, lng, lnb, wsm, bsf), exchange=gather, scatter=False,
        in_specs=[_row_spec(tm, D_MODEL), _const_spec((1, D_MODEL)), _const_spec((D_MODEL, D_IN_PADDED)),
                  _const_spec((1, LANES)), _const_spec((1, D_SGU)), _const_spec((1, D_SGU)),
                  _const_spec((N_GROUPS, SGU_CHUNK, SGU_CHUNK)), _const_spec((SGU_CHUNK, D_SGU))],
        out_specs=[head_spec, head_spec, head_spec, _row_spec(tm, LANES), _row_spec(tm, 2 * D_SGU),
                   _row_spec(tm, D_SGU)],
        out_shape=[SDS((N_HEADS, t, LANES), BF16)] * 3 + [SDS((t, LANES), F32), SDS((t, 2 * D_SGU), F32),
                                                          SDS((t, D_SGU), F32)],
        scratch_shapes=[pltpu.VMEM((1, LANES), F32)])


def _aug_lane(hh):
    return (1 - hh) * HEAD_DIM


def _attn_fwd(qa, ka, va, *, tq):
    t = qa.shape[1]
    nq = t // tq
    nrc = tq // ROW_CHUNK

    def body(q_ref, k_hbm, v_hbm, o_ref, lse_ref, k_vm, vt_vm, v_tmp, qt_ref, s_ref, p_ref, m_ref, a_ref, acc_ref):
        p = pl.program_id(0)
        i = pl.program_id(1)

        @pl.when(i == 0)
        def _():
            pltpu.sync_copy(k_hbm.at[pl.ds(2 * p, 2)], k_vm)

            def transpose_block(b, carry):
                start = pl.multiple_of(b * tq, tq)
                for h in range(2):
                    pltpu.sync_copy(v_hbm.at[2 * p + h, pl.ds(start, tq), :], v_tmp.at[h])
                    vt_vm[h, :, pl.ds(start, tq)] = v_tmp[h].astype(F32).T.astype(BF16)
                return carry

            lax.fori_loop(0, nq, transpose_block, 0)

        for h in range(2):
            qt_ref[h] = q_ref[h].astype(F32).T.astype(BF16)
        m_ref[...] = jnp.full(m_ref.shape, -jnp.inf, F32)
        acc_ref[...] = jnp.zeros_like(acc_ref)
        rowk = lax.broadcasted_iota(jnp.int32, (ROW_CHUNK, tq), 0)
        colq = lax.broadcasted_iota(jnp.int32, (ROW_CHUNK, tq), 1)

        def scores(j, slot):
            start = pl.multiple_of(j * tq, tq)
            for h in range(2):
                s_ref[slot, h] = _dot(k_vm[h, pl.ds(start, tq), :], qt_ref[h])

        def chunk(slot, h, r, masked):
            sc = s_ref[slot, h, r * ROW_CHUNK:(r + 1) * ROW_CHUNK, :]
            if masked:
                sc = jnp.where(rowk + r * ROW_CHUNK <= colq, sc, -jnp.inf)
            return sc

        def softmax(slot, masked):
            for h in range(2):
                m_old = m_ref[h]
                top = jnp.full((8, tq), -jnp.inf, F32)
                for r in range(nrc):
                    sc = chunk(slot, h, r, masked)
                    for g in range(ROW_CHUNK // 8):
                        top = jnp.maximum(top, sc[g * 8:(g + 1) * 8, :])
                m_new = jnp.maximum(m_old, jnp.max(top, axis=0, keepdims=True))
                for r in range(nrc):
                    rows = slice(r * ROW_CHUNK, (r + 1) * ROW_CHUNK)
                    p_ref[slot, h, rows, :] = jnp.exp2(chunk(slot, h, r, masked) - m_new).astype(BF16)
                a_ref[slot, h] = jnp.exp2(m_old - m_new)
                m_ref[h] = m_new

        def accumulate(j, slot):
            start = pl.multiple_of(j * tq, tq)
            for h in range(2):
                acc_ref[h] = acc_ref[h] * a_ref[slot, h] + _dot(vt_vm[h, :, pl.ds(start, tq)], p_ref[slot, h])

        scores(0, 0)

        @pl.when(i > 0)
        def _():
            scores(1, 1)
            softmax(0, False)

        def stage(j, slot):
            scores(j + 1, 1 - slot)
            softmax(slot, False)
            accumulate(j - 1, 1 - slot)

        def pair_body(n, carry):
            j = 1 + 2 * n
            stage(j, 1)
            stage(j + 1, 0)
            return carry

        rest = i - 1
        lax.fori_loop(0, rest // 2, pair_body, 0)

        @pl.when((rest > 0) & (lax.rem(rest, 2) == 1))
        def _():
            stage(i - 1, 1)

        @pl.when(lax.rem(i, 2) == 0)
        def _():
            softmax(0, True)

            @pl.when(i > 0)
            def _():
                accumulate(i - 1, 1)

            accumulate(i, 0)

        @pl.when(lax.rem(i, 2) == 1)
        def _():
            softmax(1, True)
            accumulate(i - 1, 0)
            accumulate(i, 1)

        row = lax.broadcasted_iota(jnp.int32, (LANES, tq), 0)
        l_h = [acc_ref[h, _aug_lane(h):_aug_lane(h) + 1, :] for h in range(2)]
        o_ref[...] = jnp.where(row < HEAD_DIM, acc_ref[0] / l_h[0], acc_ref[1] / l_h[1]).T
        lse_ref[0] = jnp.zeros((8, tq), F32)
        for h in range(2):
            lse_ref[0, h:h + 1, :] = m_ref[h] + jnp.log2(l_h[h])

    return pl.pallas_call(
        body, name="attn_fwd", grid=(HEAD_PAIRS, nq),
        in_specs=[pl.BlockSpec((2, tq, LANES), lambda p, i: (p, i, 0)), ANY, ANY],
        out_specs=[pl.BlockSpec((tq, LANES), lambda p, i: (i, p)),
                   pl.BlockSpec((1, 8, tq), lambda p, i: (p, 0, i))],
        out_shape=[SDS((t, D_ATTN), F32), SDS((HEAD_PAIRS, 8, t), F32)],
        scratch_shapes=[pltpu.VMEM((2, t, LANES), BF16), pltpu.VMEM((2, LANES, t), BF16),
                        pltpu.VMEM((2, tq, LANES), BF16), pltpu.VMEM((2, LANES, tq), BF16),
                        pltpu.VMEM((2, 2, tq, tq), F32), pltpu.VMEM((2, 2, tq, tq), BF16),
                        pltpu.VMEM((2, 1, tq), F32), pltpu.VMEM((2, 2, 1, tq), F32), pltpu.VMEM((2, LANES, tq), F32)],
        compiler_params=_params("arbitrary", "arbitrary"),
    )(qa, ka, va)


def _attn_bwd(qa, ka, va, do, lse_row, delta_row, *, tq):
    t = qa.shape[1]
    nq = t // tq
    nrc = tq // ROW_CHUNK

    def body(q_hbm, do_ref, k_ref, v_ref, lse_ref, dl_ref, dqt_ref, dk_ref, dv_ref, dck_ref, dcq_ref,
             q_vm, st_ref, dp_ref, pt_ref, ds_ref, dka_ref, dva_ref, vh_ref, kx_ref):
        p = pl.program_id(0)
        j = pl.program_id(1)

        @pl.when(j == 0)
        def _():
            pltpu.sync_copy(q_hbm.at[pl.ds(2 * p, 2)], q_vm)
            dqt_ref[...] = jnp.zeros_like(dqt_ref)
            dcq_ref[...] = jnp.zeros_like(dcq_ref)

        dka_ref[...] = jnp.zeros_like(dka_ref)
        dva_ref[...] = jnp.zeros_like(dva_ref)
        lane = lax.broadcasted_iota(jnp.int32, (tq, LANES), 1)
        in_head = (lane < HEAD_DIM, lane >= HEAD_DIM)
        for h in range(2):
            zero = jnp.zeros((tq, LANES), BF16)
            vh_ref[h] = jnp.where(in_head[h], v_ref[h], zero)
            kx_ref[h, :LANES, :] = jnp.where(in_head[h], k_ref[h], zero).astype(F32).T.astype(BF16)
            kx_ref[h, LANES:, :] = jnp.ones((KX_ROWS - LANES, tq), BF16)
        rowk = lax.broadcasted_iota(jnp.int32, (ROW_CHUNK, tq), 0)
        colq = lax.broadcasted_iota(jnp.int32, (ROW_CHUNK, tq), 1)

        def step(i, masked, slot):
            start = pl.multiple_of(i * tq, tq)
            cols = pl.ds(start, tq)
            do2 = do_ref[pl.ds(start, tq), :]
            q_h = [q_vm[h, pl.ds(start, tq), :] for h in range(2)]
            for h in range(2):
                st_ref[slot, h] = _dot_nt(k_ref[h], q_h[h])
                dp_ref[slot, h] = _dot_nt(vh_ref[h], do2)
            for h in range(2):
                lse = lse_ref[0, h:h + 1, cols]
                delta = dl_ref[0, h:h + 1, cols]
                for r in range(nrc):
                    rows = slice(r * ROW_CHUNK, (r + 1) * ROW_CHUNK)
                    st = st_ref[slot, h, rows, :]
                    if masked:
                        st = jnp.where(rowk + r * ROW_CHUNK <= colq, st, -jnp.inf)
                    pt = jnp.exp2(st - lse)
                    pt_ref[slot, h, rows, :] = pt.astype(BF16)
                    ds_ref[slot, h, rows, :] = (pt * (dp_ref[slot, h, rows, :] - delta)).astype(BF16)
            dq_t = jnp.zeros((LANES, tq), F32)
            for h in range(2):
                dva_ref[h] += _dot(pt_ref[slot, h], do2)
                dka_ref[h] += _dot(ds_ref[slot, h], q_h[h])
                ext = _dot(kx_ref[h], ds_ref[slot, h])
                dq_t = dq_t + ext[:LANES, :]
                dcq_ref[0, h:h + 1, cols] += ext[LANES:LANES + 1, :]
            dqt_ref[0, :, cols] += dq_t

        def pair_body(n, carry):
            i = j + 1 + 2 * n
            step(i, False, 0)
            step(i + 1, False, 1)
            return carry

        step(j, True, 0)
        after = nq - 1 - j
        lax.fori_loop(0, after // 2, pair_body, 0)

        @pl.when(lax.rem(after, 2) == 1)
        def _():
            step(nq - 1, False, 0)

        dk_ref[...] = (jnp.where(in_head[0], dka_ref[0], dka_ref[1]) * LN2).astype(BF16)
        dv_ref[...] = jnp.where(in_head[0], dva_ref[0], dva_ref[1]).astype(BF16)
        own = pl.ds(pl.multiple_of(j * tq, tq), tq)
        for h in range(2):
            at = _aug_lane(h) + 3
            dck_ref[0, h:h + 1, own] = -dka_ref[h].T[at:at + 1, :]

    rows = pl.BlockSpec((1, 2, t), lambda p, j: (p, 0, 0))
    tiles = pl.BlockSpec((1, LANES, t), lambda p, j: (p, 0, 0))
    return pl.pallas_call(
        body, name="attn_bwd", grid=(HEAD_PAIRS, nq),
        in_specs=[ANY, pl.BlockSpec((t, LANES), lambda p, j: (0, p)),
                  pl.BlockSpec((2, tq, LANES), lambda p, j: (p, j, 0)),
                  pl.BlockSpec((2, tq, LANES), lambda p, j: (p, j, 0)), rows, rows],
        out_specs=[tiles,
                   pl.BlockSpec((tq, LANES), lambda p, j: (j, p)),
                   pl.BlockSpec((tq, LANES), lambda p, j: (j, p)), rows, rows],
        out_shape=[SDS((HEAD_PAIRS, LANES, t), F32), SDS((t, D_ATTN), BF16), SDS((t, D_ATTN), BF16),
                   SDS((HEAD_PAIRS, 2, t), F32), SDS((HEAD_PAIRS, 2, t), F32)],
        scratch_shapes=[pltpu.VMEM((2, t, LANES), BF16), pltpu.VMEM((2, 2, tq, tq), F32),
                        pltpu.VMEM((2, 2, tq, tq), F32), pltpu.VMEM((2, 2, tq, tq), BF16),
                        pltpu.VMEM((2, 2, tq, tq), BF16),
                        pltpu.VMEM((2, tq, LANES), F32), pltpu.VMEM((2, tq, LANES), F32),
                        pltpu.VMEM((2, tq, LANES), BF16), pltpu.VMEM((2, KX_ROWS, tq), BF16)],
        compiler_params=_params("arbitrary", "arbitrary"),
    )(qa, do, ka, va, lse_row, delta_row)

def _load_weights_once(pairs):
    @pl.when(pl.program_id(0) == 0)
    def _():
        for src, dst in pairs:
            pltpu.sync_copy(src, dst)


def _row_tile_call(body, name, nt, operands, *, in_specs, out_specs, out_shape, scratch_shapes, exchange, scatter):
    if not exchange:
        return pl.pallas_call(body, name=name, grid=(nt,), in_specs=in_specs, out_specs=out_specs,
                              out_shape=out_shape, scratch_shapes=scratch_shapes,
                              compiler_params=_params("arbitrary"))(*operands)
    flags = [scatter] * len(exchange)
    n = len(exchange)
    return pl.pallas_call(
        _fused_exchange(body, len(operands), len(out_shape), flags, nt),
        name=name + ("_scatter" if scatter else "_gather"), grid=(nt,),
        in_specs=list(in_specs) + [ANY] * n, out_specs=list(out_specs) + [ANY] * n,
        out_shape=list(out_shape) + _exchange_shapes(exchange, flags),
        scratch_shapes=list(scratch_shapes) + _exchange_scratch(n),
        compiler_params=_params("arbitrary"))(*operands, *exchange)


def _fwd_b(x, o, sgu, gout, gffn, w_out, w_gu, w_dn, *, tm, gather=()):
    t = x.shape[0]
    nt = t // tm

    def body(x_ref, o_ref, s_ref, gout_ref, gffn_ref, wout_hbm, wgu_hbm, wdn_hbm,
             x1_ref, x2_ref, gu_ref, wout, wgu, wdn):
        _load_weights_once(((wout_hbm, wout), (wgu_hbm, wgu), (wdn_hbm, wdn)))
        ov = o_ref[...]
        sv = s_ref[...]
        mo = ((ov * lax.rsqrt(_mean(ov * ov) + EPS)) * gout_ref[:, :D_ATTN]).astype(BF16)
        ms = ((sv * lax.rsqrt(_mean(sv * sv) + EPS)) * gout_ref[:, D_ATTN:]).astype(BF16)
        x1 = x_ref[...] + (_dot(mo, wout[:D_ATTN, :]) + _dot(ms, wout[D_ATTN:, :]))
        x1_ref[...] = x1
        xn2 = ((x1 * lax.rsqrt(_mean(x1 * x1) + EPS)) * gffn_ref[...]).astype(BF16)
        y = jnp.zeros((tm, D_MODEL), F32)
        for n in range(N_FF_CHUNKS):
            lo, hi = n * FF_CHUNK, (n + 1) * FF_CHUNK
            gate = _dot(xn2, wgu[:, lo:hi])
            up = _dot(xn2, wgu[:, D_FF + lo:D_FF + hi])
            gu_ref[:, lo:hi] = gate
            gu_ref[:, D_FF + lo:D_FF + hi] = up
            a = ((gate * _sigmoid(gate)) * up).astype(BF16)
            y = y + _dot(a, wdn[lo:hi, :])
        x2_ref[...] = x1 + y

    return _row_tile_call(
        body, "fwd_b", nt, (x, o, sgu, gout, gffn, w_out, w_gu, w_dn),
        in_specs=[_row_spec(tm, D_MODEL), _row_spec(tm, D_ATTN), _row_spec(tm, D_SGU),
                  _const_spec((1, D_MODEL)), _const_spec((1, D_MODEL)), ANY, ANY, ANY],
        out_specs=[_row_spec(tm, D_MODEL), _row_spec(tm, D_MODEL), _row_spec(tm, 2 * D_FF)],
        out_shape=[SDS((t, D_MODEL), F32), SDS((t, D_MODEL), F32), SDS((t, 2 * D_FF), F32)],
        scratch_shapes=[pltpu.VMEM((D_MODEL, D_MODEL), BF16), pltpu.VMEM((D_MODEL, 2 * D_FF), BF16),
                        pltpu.VMEM((D_FF, D_MODEL), BF16)],
        exchange=gather, scatter=False)


def _loss_bwd(x, tgt, gfin, *, tm):
    t = x.shape[0]
    nt = t // tm

    def body(x_ref, t_ref, g_ref, dx_ref, loss_ref, dg_ref):
        @pl.when(pl.program_id(0) == 0)
        def _():
            loss_ref[...] = jnp.zeros_like(loss_ref)
            dg_ref[...] = jnp.zeros_like(dg_ref)

        xt = x_ref[...]
        g = g_ref[...]
        r = lax.rsqrt(_mean(xt * xt) + EPS)
        xh = xt * r
        err = xh * g - t_ref[...]
        loss_ref[...] += 0.5 * jnp.sum(_mean(err * err), axis=0, keepdims=True)
        dy = err * (1.0 / D_MODEL)
        dg_ref[...] += jnp.sum(dy * xh, axis=0, keepdims=True)
        dyg = dy * g
        dx_ref[...] = r * (dyg - xh * _mean(dyg * xh))

    return pl.pallas_call(
        body, name="loss_bwd", grid=(nt,),
        in_specs=[_row_spec(tm, D_MODEL), _row_spec(tm, D_MODEL), _const_spec((1, D_MODEL))],
        out_specs=[_row_spec(tm, D_MODEL), _const_spec((1, 1)), _const_spec((1, D_MODEL))],
        out_shape=[SDS((t, D_MODEL), F32), SDS((1, 1), F32), SDS((1, D_MODEL), F32)],
        compiler_params=_params("arbitrary"),
    )(x, tgt, gfin)


def _bwd_b(dx2, x1, gu, o, sgu, gout, gffn, w_out, w_gu, w_dn, *, tm, scatter=()):
    t = dx2.shape[0]
    nt = t // tm

    def body(dx2_ref, x1_ref, gu_ref, o_ref, s_ref, gout_ref, gffn_ref, wout_hbm, wgu_hbm, wdn_hbm,
             dx1_ref, dx2b_ref, a_ref, dgu_ref, xn2_ref, mrg_ref, dx1b_ref, do_ref, dl_ref, dsgu_ref,
             dgffn_ref, dgout_ref, wout, wgu, wdn):
        _load_weights_once(((wout_hbm, wout), (wgu_hbm, wgu), (wdn_hbm, wdn)))

        @pl.when(pl.program_id(0) == 0)
        def _():
            dgffn_ref[...] = jnp.zeros_like(dgffn_ref)
            dgout_ref[...] = jnp.zeros_like(dgout_ref)

        dx2 = dx2_ref[...]
        dx2b = dx2.astype(BF16)
        dx2b_ref[...] = dx2b
        dxn2 = jnp.zeros((tm, D_MODEL), F32)
        for n in range(N_FF_CHUNKS):
            lo, hi = n * FF_CHUNK, (n + 1) * FF_CHUNK
            gate = gu_ref[:, lo:hi]
            up = gu_ref[:, D_FF + lo:D_FF + hi]
            sg = _sigmoid(gate)
            si = gate * sg
            a_ref[:, lo:hi] = (si * up).astype(BF16)
            d_a = _dot_nt(dx2b, wdn[lo:hi, :])
            dgb = ((d_a * up) * (sg * (1.0 + gate * (1.0 - sg)))).astype(BF16)
            dub = (d_a * si).astype(BF16)
            dgu_ref[:, lo:hi] = dgb
            dgu_ref[:, D_FF + lo:D_FF + hi] = dub
            dxn2 = dxn2 + (_dot_nt(dgb, wgu[:, lo:hi]) + _dot_nt(dub, wgu[:, D_FF + lo:D_FF + hi]))

        x1 = x1_ref[...]
        gffn = gffn_ref[...]
        r1 = lax.rsqrt(_mean(x1 * x1) + EPS)
        xh1 = x1 * r1
        xn2_ref[...] = (xh1 * gffn).astype(BF16)
        dgffn_ref[...] += jnp.sum(dxn2 * xh1, axis=0, keepdims=True)
        dyg = dxn2 * gffn
        dx1 = dx2 + r1 * (dyg - xh1 * _mean(dyg * xh1))
        dx1_ref[...] = dx1
        dx1b = dx1.astype(BF16)
        dx1b_ref[...] = dx1b

        ov = o_ref[...]
        sv = s_ref[...]
        go = gout_ref[:, :D_ATTN]
        gs = gout_ref[:, D_ATTN:]
        ro = lax.rsqrt(_mean(ov * ov) + EPS)
        rs = lax.rsqrt(_mean(sv * sv) + EPS)
        oh = ov * ro
        sh = sv * rs
        mrg_ref[:, :D_ATTN] = (oh * go).astype(BF16)
        mrg_ref[:, D_ATTN:] = (sh * gs).astype(BF16)
        dmo = _dot_nt(dx1b, wout[:D_ATTN, :])
        dms = _dot_nt(dx1b, wout[D_ATTN:, :])
        dgout_ref[:, :D_ATTN] += jnp.sum(dmo * oh, axis=0, keepdims=True)
        dgout_ref[:, D_ATTN:] += jnp.sum(dms * sh, axis=0, keepdims=True)
        dmog = dmo * go
        d_o = ro * (dmog - oh * _mean(dmog * oh))
        do_ref[...] = d_o.astype(BF16)
        dl_ref[...] = _dot_exact(d_o * ov, _group_indicator()).T[:N_HEADS, :]
        dmsg = dms * gs
        dsgu_ref[...] = rs * (dmsg - sh * _mean(dmsg * sh))

    return _row_tile_call(
        body, "bwd_b", nt, (dx2, x1, gu, o, sgu, gout, gffn, w_out, w_gu, w_dn),
        in_specs=[_row_spec(tm, D_MODEL), _row_spec(tm, D_MODEL), _row_spec(tm, 2 * D_FF), _row_spec(tm, D_ATTN),
                  _row_spec(tm, D_SGU), _const_spec((1, D_MODEL)), _const_spec((1, D_MODEL)), ANY, ANY, ANY],
        out_specs=[_row_spec(tm, D_MODEL), _row_spec(tm, D_MODEL), _row_spec(tm, D_FF), _row_spec(tm, 2 * D_FF),
                   _row_spec(tm, D_MODEL), _row_spec(tm, D_MODEL), _row_spec(tm, D_MODEL), _row_spec(tm, D_ATTN),
                   pl.BlockSpec((N_HEADS, tm), lambda i: (0, i)), _row_spec(tm, D_SGU),
                   _const_spec((1, D_MODEL)), _const_spec((1, D_MODEL))],
        out_shape=[SDS((t, D_MODEL), F32), SDS((t, D_MODEL), BF16), SDS((t, D_FF), BF16), SDS((t, 2 * D_FF), BF16),
                   SDS((t, D_MODEL), BF16), SDS((t, D_MODEL), BF16), SDS((t, D_MODEL), BF16),
                   SDS((t, D_ATTN), BF16), SDS((N_HEADS, t), F32), SDS((t, D_SGU), F32),
                   SDS((1, D_MODEL), F32), SDS((1, D_MODEL), F32)],
        scratch_shapes=[pltpu.VMEM((D_MODEL, D_MODEL), BF16), pltpu.VMEM((D_MODEL, 2 * D_FF), BF16),
                        pltpu.VMEM((D_FF, D_MODEL), BF16)],
        exchange=scatter, scatter=True)


def _bwd_a(dx1, x, z, fl, dsgu, dq, dk, dv, dc, gmix, w_in, lng, lnb, wsm, wsm_t, bsf, mask, *, tm, scatter=()):
    t = x.shape[0]
    nt = t // tm
    nch = tm // SGU_CHUNK

    def body(dx1_ref, x_ref, z_ref, fl_ref, dsgu_ref, dq_ref, dk_ref, dv_ref, dc_ref, gmix_ref, w_ref,
             lng_ref, lnb_ref, wsm_ref, wsmt_ref, bsf_ref, mask_ref,
             dx_ref, xn_ref, dh_ref, dgmix_ref, dbf_ref, dlng_ref, dlnb_ref, dws_ref, dbs_ref,
             carry_ref, dzvn_ref, dzu_ref, dbacc_ref):
        step = pl.program_id(0)

        @pl.when(step == 0)
        def _():
            carry_ref[...] = jnp.zeros_like(carry_ref)
            dbacc_ref[...] = jnp.zeros_like(dbacc_ref)
            for ref in (dgmix_ref, dbf_ref, dlng_ref, dlnb_ref, dws_ref):
                ref[...] = jnp.zeros_like(ref)

        z = z_ref[...]
        erf = lax.erf(z * INV_SQRT2)
        cdf = 0.5 * (1.0 + erf)
        zg = z * cdf
        zu = zg[:, :D_SGU]
        zv = zg[:, D_SGU:]
        xc = zv - _mean(zv)
        rln = lax.rsqrt(_mean(xc * xc) + EPS)
        zh = xc * rln
        lng = lng_ref[...]
        zvn = (zh * lng + lnb_ref[...]).astype(BF16)
        dsgu = dsgu_ref[...]
        lane_grp = lax.broadcasted_iota(jnp.int32, (SGU_CHUNK, D_SGU), 1) >> 6
        for ch in range(nch):
            rows = slice(ch * SGU_CHUNK, (ch + 1) * SGU_CHUNK)
            zc = zvn[rows, :]
            ds_c = dsgu[rows, :]
            mixed = _sgu_mix(wsm_ref, zc, lane_grp) + bsf_ref[...]
            dzu_ref[rows, :] = ds_c * mixed
            dmix = ds_c * zu[rows, :]
            dbacc_ref[...] += dmix
            dmb = dmix.astype(BF16)
            dzvn_ref[rows, :] = _sgu_mix(wsmt_ref, dmb, lane_grp)
            for g in range(N_GROUPS):
                dws_ref[g] += _dot_nt(jnp.where(lane_grp == g, dmb, jnp.zeros_like(dmb)), zc)
        dzvn = dzvn_ref[...]
        dlng_ref[...] += jnp.sum(dzvn * zh, axis=0, keepdims=True)
        dlnb_ref[...] += jnp.sum(dzvn, axis=0, keepdims=True)
        dzh = dzvn * lng
        dzv = rln * ((dzh - _mean(dzh)) - zh * _mean(dzh * zh))
        pdf = jnp.exp(-0.5 * (z * z)) * INV_SQRT_2PI
        dgelu = cdf + z * pdf
        z_at = 3 * D_ATTN + LANES
        dh_ref[:, z_at:z_at + D_SGU] = (dzu_ref[...] * dgelu[:, :D_SGU]).astype(BF16)
        dh_ref[:, z_at + D_SGU:] = (dzv * dgelu[:, D_SGU:]).astype(BF16)

        dc = jnp.concatenate([dc_ref[...], jnp.zeros((LANES - N_HEADS, tm), F32)], axis=0).T
        row = lax.broadcasted_iota(jnp.int32, (tm, tm), 0)
        col = lax.broadcasted_iota(jnp.int32, (tm, tm), 1)
        dlogf = _dot_exact((col >= row).astype(F32), dc) + carry_ref[...]
        carry_ref[...] = dlogf[0:1, :]
        dfl = dlogf * _sigmoid(-fl_ref[...])
        dbf_ref[...] += jnp.sum(dfl, axis=0, keepdims=True)
        dh_ref[:, 3 * D_ATTN:z_at] = dfl.astype(BF16)

        for pair in range(HEAD_PAIRS):
            dh_ref[:, pair * LANES:(pair + 1) * LANES] = (dq_ref[pair].T * QK_SCALE).astype(BF16)
        dh_ref[:, D_ATTN:2 * D_ATTN] = dk_ref[...]
        dh_ref[:, 2 * D_ATTN:3 * D_ATTN] = dv_ref[...]
        dxn = _dot_nt(dh_ref[...], w_ref[...])

        xt = x_ref[...]
        gmix = gmix_ref[...]
        r = lax.rsqrt(_mean(xt * xt) + EPS)
        xh = xt * r
        xn_ref[...] = (xh * gmix).astype(BF16)
        dgmix_ref[...] += jnp.sum(dxn * xh, axis=0, keepdims=True)
        dyg = dxn * gmix
        dx_ref[...] = dx1_ref[...] + r * (dyg - xh * _mean(dyg * xh))

        @pl.when(step == nt - 1)
        def _():
            for g in range(N_GROUPS):
                dws_ref[g] = dws_ref[g] * mask_ref[...]
            dbs_ref[...] = _dot_exact(dbacc_ref[...], _group_indicator())

    rev = functools.partial(_rev_spec, nt=nt)
    return _row_tile_call(
        body, "bwd_a", nt, (dx1, x, z, fl, dsgu, dq, dk, dv, dc, gmix, w_in, lng, lnb, wsm, wsm_t, bsf, mask),
        exchange=scatter, scatter=True,
        in_specs=[rev(tm, D_MODEL), rev(tm, D_MODEL), rev(tm, 2 * D_SGU), rev(tm, LANES), rev(tm, D_SGU),
                  pl.BlockSpec((HEAD_PAIRS, LANES, tm), lambda i: (0, 0, nt - 1 - i)), rev(tm, D_ATTN), rev(tm, D_ATTN),
                  pl.BlockSpec((N_HEADS, tm), lambda i: (0, nt - 1 - i)),
                  _const_spec((1, D_MODEL)), _const_spec((D_MODEL, D_IN_PADDED)),
                  _const_spec((1, D_SGU)), _const_spec((1, D_SGU)),
                  _const_spec((N_GROUPS, SGU_CHUNK, SGU_CHUNK)), _const_spec((N_GROUPS, SGU_CHUNK, SGU_CHUNK)),
                  _const_spec((SGU_CHUNK, D_SGU)), _const_spec((SGU_CHUNK, SGU_CHUNK))],
        out_specs=[rev(tm, D_MODEL), rev(tm, D_MODEL), rev(tm, D_IN_PADDED),
                   _const_spec((1, D_MODEL)), _const_spec((1, LANES)), _const_spec((1, D_SGU)), _const_spec((1, D_SGU)),
                   _const_spec((N_GROUPS, SGU_CHUNK, SGU_CHUNK)), _const_spec((SGU_CHUNK, LANES))],
        out_shape=[SDS((t, D_MODEL), F32), SDS((t, D_MODEL), BF16), SDS((t, D_IN_PADDED), BF16), SDS((1, D_MODEL), F32), SDS((1, LANES), F32), SDS((1, D_SGU), F32),
                   SDS((1, D_SGU), F32), SDS((N_GROUPS, SGU_CHUNK, SGU_CHUNK), F32), SDS((SGU_CHUNK, LANES), F32)],
        scratch_shapes=[pltpu.VMEM((1, LANES), F32), pltpu.VMEM((tm, D_SGU), F32), pltpu.VMEM((tm, D_SGU), F32),
                        pltpu.VMEM((SGU_CHUNK, D_SGU), F32)])


def _pick(n, cap):
    if n <= cap:
        return n
    best = LANES
    for cand in range(LANES, cap + 1, LANES):
        if n % cand == 0:
            best = cand
    return best


def _tn_matmul(a, b, *, bt):
    t, k1 = a.shape
    n = b.shape[1]
    bk = _pick(k1, 1408)
    bn = _pick(n, 1408)
    nsteps = t // bt

    def body(a_ref, b_ref, o_ref):
        @pl.when(pl.program_id(2) == 0)
        def _():
            o_ref[...] = jnp.zeros_like(o_ref)

        o_ref[...] += _dot_tn(a_ref[...], b_ref[...])

    return pl.pallas_call(
        body, name=f"wgrad_{k1}x{n}", grid=(k1 // bk, n // bn, nsteps),
        in_specs=[pl.BlockSpec((bt, bk), lambda i, j, s: (s, i)), pl.BlockSpec((bt, bn), lambda i, j, s: (s, j))],
        out_specs=pl.BlockSpec((bk, bn), lambda i, j, s: (i, j)),
        out_shape=SDS((k1, n), F32),
        compiler_params=_params("arbitrary", "arbitrary", "arbitrary"),
    )(a, b)


def _adamw(parts, w, m, v, *, name):
    rows, cols = w.shape
    br = _pick_rows(rows, cols)
    c1 = 1.0 - ADAM_B1 ** ADAM_STEP
    c2 = 1.0 - ADAM_B2 ** ADAM_STEP

    def body(p_ref, w_ref, m_ref, v_ref, g_ref, d_ref, nm_ref, nv_ref):
        g = p_ref[0].astype(F32)
        for j in range(1, N_DEV):
            g = g + p_ref[j].astype(F32)
        g_ref[...] = g
        nm = ADAM_B1 * m_ref[...] + (1.0 - ADAM_B1) * g
        nv = ADAM_B2 * v_ref[...] + (1.0 - ADAM_B2) * (g * g)
        nm_ref[...] = nm
        nv_ref[...] = nv
        d_ref[...] = -ADAM_LR * ((nm / c1) / (jnp.sqrt(nv / c2) + ADAM_EPS) + ADAM_WD * w_ref[...])

    spec = pl.BlockSpec((br, cols), lambda i: (i, 0))
    return pl.pallas_call(
        body, name=name, grid=(rows // br,),
        in_specs=[pl.BlockSpec((N_DEV, br, cols), lambda i: (0, i, 0)), spec, spec, spec],
        out_specs=[spec] * 4, out_shape=[SDS((rows, cols), F32)] * 4,
        compiler_params=_params("arbitrary"),
    )(parts, w, m, v)


def _pick_rows(rows, cols):
    target = max(8, (256 * 1024) // cols)
    best = 8
    for cand in range(8, min(rows, target) + 1, 8):
        if rows % cand == 0:
            best = cand
    return best


def _peer(k):
    x, y, c = lax.axis_index("x"), lax.axis_index("y"), lax.axis_index("c")
    px = 1 - x if k & 4 else x
    py = 1 - y if k & 2 else y
    pc = 1 - c if k & 1 else c
    return (px, py, pc), 4 * px + 2 * py + pc


def _exchange_scratch(n):
    return [pltpu.SemaphoreType.DMA((N_DEV - 1, n)), pltpu.SemaphoreType.DMA((N_DEV - 1, n)),
            pltpu.SemaphoreType.DMA((n,))]


def _exchange_copies(ins, outs, sems, scatter, landing):
    send_sems, recv_sems, local_sems = sems
    me = 4 * lax.axis_index("x") + 2 * lax.axis_index("y") + lax.axis_index("c")
    copies = [pltpu.make_async_copy(ins[a].at[me] if scatter[a] else ins[a], outs[a].at[me], local_sems.at[a])
              for a in range(len(ins))]
    for k in range(1, N_DEV):
        peer, pidx = _peer(k)
        for a in range(len(ins)):
            copies.append(pltpu.make_async_remote_copy(
                src_ref=ins[a].at[pidx] if scatter[a] else ins[a], dst_ref=outs[a].at[pidx if landing else me],
                send_sem=send_sems.at[k - 1, a], recv_sem=recv_sems.at[k - 1, a], device_id=peer, device_id_type=MESH))
    return copies


def _exchange_start(ins, outs, sems, scatter):
    for cp in _exchange_copies(ins, outs, sems, scatter, landing=False):
        cp.start()


def _exchange_wait(ins, outs, sems, scatter):
    for cp in _exchange_copies(ins, outs, sems, scatter, landing=True):
        cp.wait()


def _exchange_shapes(arrs, scatter):
    return [SDS(a.shape if sc else (N_DEV,) + a.shape, a.dtype) for a, sc in zip(arrs, scatter)]


def _exchange(arrs, scatter, *, name):
    n = len(arrs)

    def body(*refs):
        ins, outs, sems = refs[:n], refs[n:2 * n], refs[2 * n:]
        _exchange_start(ins, outs, sems, scatter)
        _exchange_wait(ins, outs, sems, scatter)

    return pl.pallas_call(
        body, name=name, in_specs=[ANY] * n, out_specs=[ANY] * n, out_shape=_exchange_shapes(arrs, scatter),
        scratch_shapes=_exchange_scratch(n),
    )(*arrs)


def _fused_exchange(body, n_in, n_out, scatter, nsteps):
    n = len(scatter)

    def wrapped(*refs):
        ins, ex_in = refs[:n_in], refs[n_in:n_in + n]
        outs, ex_out = refs[n_in + n:n_in + n + n_out], refs[n_in + n + n_out:n_in + 2 * n + n_out]
        scratch, sems = refs[n_in + 2 * n + n_out:-3], refs[-3:]

        @pl.when(pl.program_id(0) == 0)
        def _():
            _exchange_start(ex_in, ex_out, sems, scatter)

        body(*ins, *outs, *scratch)

        @pl.when(pl.program_id(0) == nsteps - 1)
        def _():
            _exchange_wait(ex_in, ex_out, sems, scatter)

    return wrapped


def _step(x, tgt, small, shards):
    t = x.shape[0]
    tm, tq, tw = _tiles(t)
    r = jnp.arange(SGU_CHUNK, dtype=jnp.int32) // SGU_BLOCK
    mask = (r[None, :] <= r[:, None]).astype(F32)
    layer_shards = lambda l: [shards[n][l] for n in BIG]

    layers = []
    saved = []
    gathered = _exchange(layer_shards(0)[:1], [False], name="gather_weights")
    for l in range(DEPTH):
        w_in = _assemble(gathered[0], BIG[0])
        w_pad = jnp.concatenate([w_in[:, :3 * D_ATTN + N_HEADS], jnp.zeros((D_MODEL, LANES - N_HEADS), BF16),
                                 w_in[:, 3 * D_ATTN + N_HEADS:]], axis=1)
        bf = jnp.pad(small["b_f"][l], (0, LANES - N_HEADS))[None, :]
        wsm = (small["w_s"][l] * mask[None]).astype(BF16)
        wsm_t = jnp.swapaxes(wsm, 1, 2)
        bsf = jnp.repeat(small["b_s"][l].T, GROUP_DIM, axis=1)
        lw = dict(w_in=w_pad, bf=bf, wsm=wsm, wsm_t=wsm_t, bsf=bsf,
                  gmix=small["mix_norm_g"][l][None, :], lng=small["sgu_ln_g"][l][None, :],
                  lnb=small["sgu_ln_b"][l][None, :], gout=small["out_norm_g"][l][None, :],
                  gffn=small["ffn_norm_g"][l][None, :])
        layers.append(lw)
        q, k, v, fl, z, sgu, *late = _fwd_a(x, lw["gmix"], w_pad, bf, lw["lng"], lw["lnb"], wsm, bsf, tm=tm,
                                                 gather=layer_shards(0)[1:] if l == 0 else ())
        w_out, w_gu, w_dn = (_assemble(g, n) for g, n in zip(late if l == 0 else gathered[1:], BIG[1:]))
        lw.update(w_out=w_out, w_gu=w_gu, w_dn=w_dn)
        o, lse = _attn_fwd(q, k, v, tq=tq)
        x1, x2, gu, *gathered = _fwd_b(x, o, sgu, lw["gout"], lw["gffn"], w_out, w_gu, w_dn, tm=tm,
                                       gather=layer_shards(l + 1) if l + 1 < DEPTH else ())
        saved.append(dict(x=x, q=q, k=k, v=v, fl=fl, z=z, sgu=sgu, o=o, lse=lse[:, :2, :], x1=x1, gu=gu))
        x = x2

    dx, loss, dgfin = _loss_bwd(x, tgt, small["final_norm_g"][None, :], tm=tm)
    grads = {n: [None] * DEPTH for n in SMALL if n != "final_norm_g"}
    parts = [None] * DEPTH
    pending = ()
    for l in reversed(range(DEPTH)):
        lw, sv = layers[l], saved[l]
        (dx1, dx2b, a, dgu, xn2, mrg, dx1b, do, delta, dsgu, dgffn, dgout, *landed) = _bwd_b(
            dx, sv["x1"], sv["gu"], sv["o"], sv["sgu"], lw["gout"], lw["gffn"], lw["w_out"], lw["w_gu"], lw["w_dn"],
            tm=tm, scatter=pending)
        if pending:
            parts[l + 1] = landed
        big_grads = {"w_down": _tn_matmul(a, dx2b, bt=tw), "w_gate_up": _tn_matmul(xn2, dgu, bt=tw),
                     "w_out": _tn_matmul(mrg, dx1b, bt=tw)}
        dqt, dk, dv, dck, dcq = _attn_bwd(sv["q"], sv["k"], sv["v"], do, sv["lse"],
                                          delta.reshape(HEAD_PAIRS, 2, t), tq=tq)
        early = [_split(big_grads[n], n).astype(BF16) for n in BIG[1:]] if l == 0 else ()
        (dx, xn, dh, dgmix, dbf, dlng, dlnb, dws, dbs, *landed) = _bwd_a(
            dx1, sv["x"], sv["z"], sv["fl"], dsgu, dqt, dk, dv, (dck + dcq).reshape(N_HEADS, t), lw["gmix"],
            lw["w_in"], lw["lng"], lw["lnb"], lw["wsm"], lw["wsm_t"], lw["bsf"], mask, tm=tm,
            scatter=early)
        g_pad = _tn_matmul(xn, dh, bt=tw)
        big_grads["w_in"] = jnp.concatenate([g_pad[:, :3 * D_ATTN + N_HEADS], g_pad[:, 3 * D_ATTN + LANES:]], axis=1)
        pending = [_split(big_grads[n], n).astype(BF16) for n in (BIG[:1] if l == 0 else BIG)]
        grads["mix_norm_g"][l] = dgmix[0]
        grads["b_f"][l] = dbf[0, :N_HEADS]
        grads["sgu_ln_g"][l] = dlng[0]
        grads["sgu_ln_b"][l] = dlnb[0]
        grads["w_s"][l] = dws
        grads["b_s"][l] = dbs[:, :N_GROUPS].T
        grads["out_norm_g"][l] = dgout[0]
        grads["ffn_norm_g"][l] = dgffn[0]
    grads = {n: jnp.stack(g) for n, g in grads.items()}
    grads["final_norm_g"] = dgfin[0]
    first, small_parts = _exchange(pending + [_pack(grads)], [True, False], name="scatter_grads")
    parts[0] = [first] + landed
    big_parts = {}
    for a, n in enumerate(BIG):
        stacked = jnp.stack([parts[l][a] for l in range(DEPTH)], axis=1)
        big_parts[n] = stacked.reshape(N_DEV, -1, stacked.shape[-1])
    return loss[0, 0], dx, big_parts, small_parts


SMALL = ("mix_norm_g", "b_f", "sgu_ln_g", "sgu_ln_b", "w_s", "b_s", "out_norm_g", "ffn_norm_g", "final_norm_g")
BIG = ("w_in", "w_out", "w_gate_up", "w_down")
WEIGHTS = ("mix_norm_g", "w_in", "b_f", "sgu_ln_g", "sgu_ln_b", "w_s", "b_s", "out_norm_g", "w_out", "ffn_norm_g",
           "w_gate_up", "w_down", "final_norm_g")
SHARD_AXIS = {"w_in": 1, "w_out": 0, "w_gate_up": 1, "w_down": 0}


def _assemble(gathered, name):
    if SHARD_AXIS[name] == 0:
        return gathered.reshape(-1, gathered.shape[-1])
    return gathered.transpose(1, 0, 2).reshape(gathered.shape[1], -1)


def _split(full, name):
    rows, cols = full.shape
    if SHARD_AXIS[name] == 0:
        return full.reshape(N_DEV, rows // N_DEV, cols)
    return full.reshape(rows, N_DEV, cols // N_DEV).transpose(1, 0, 2)


def _pack(tree):
    flat = jnp.concatenate([tree[n].reshape(-1) for n in SMALL])
    pad = (-flat.shape[0]) % (8 * LANES)
    return jnp.pad(flat, (0, pad)).reshape(-1, LANES)


def _unpack(packed, like):
    flat = packed.reshape(-1)
    out, at = {}, 0
    for n in SMALL:
        size = like[n].size
        out[n] = flat[at:at + size].reshape(like[n].shape)
        at += size
    return out


def kernel(x, mix_norm_g, w_in, b_f, sgu_ln_g, sgu_ln_b, w_s, b_s, out_norm_g, w_out, ffn_norm_g, w_gate_up, w_down, final_norm_g, loss_target, m_mix_norm_g, m_w_in, m_b_f, m_sgu_ln_g, m_sgu_ln_b, m_w_s, m_b_s, m_out_norm_g, m_w_out, m_ffn_norm_g, m_w_gate_up, m_w_down, m_final_norm_g, v_mix_norm_g, v_w_in, v_b_f, v_sgu_ln_g, v_sgu_ln_b, v_w_s, v_b_s, v_out_norm_g, v_w_out, v_ffn_norm_g, v_w_gate_up, v_w_down, v_final_norm_g):
    w = dict(mix_norm_g=mix_norm_g, w_in=w_in, b_f=b_f, sgu_ln_g=sgu_ln_g, sgu_ln_b=sgu_ln_b, w_s=w_s, b_s=b_s,
             out_norm_g=out_norm_g, w_out=w_out, ffn_norm_g=ffn_norm_g, w_gate_up=w_gate_up, w_down=w_down,
             final_norm_g=final_norm_g)
    m = dict(mix_norm_g=m_mix_norm_g, w_in=m_w_in, b_f=m_b_f, sgu_ln_g=m_sgu_ln_g, sgu_ln_b=m_sgu_ln_b, w_s=m_w_s,
             b_s=m_b_s, out_norm_g=m_out_norm_g, w_out=m_w_out, ffn_norm_g=m_ffn_norm_g, w_gate_up=m_w_gate_up,
             w_down=m_w_down, final_norm_g=m_final_norm_g)
    v = dict(mix_norm_g=v_mix_norm_g, w_in=v_w_in, b_f=v_b_f, sgu_ln_g=v_sgu_ln_g, sgu_ln_b=v_sgu_ln_b, w_s=v_w_s,
             b_s=v_b_s, out_norm_g=v_out_norm_g, w_out=v_w_out, ffn_norm_g=v_ffn_norm_g, w_gate_up=v_w_gate_up,
             w_down=v_w_down, final_norm_g=v_final_norm_g)

    loss, dx, big_parts, small_parts = _step(x[0], loss_target[0], {n: w[n] for n in SMALL},
                                             {n: w[n].astype(BF16) for n in BIG})
    loss = lax.psum(loss, ("x", "y", "c"))

    g_out, d_out, m_out, v_out = {}, {}, {}, {}
    for n in BIG:
        shape = w[n].shape
        two_d = lambda a: a.reshape(-1, shape[-1])
        res = _adamw(big_parts[n], two_d(w[n]), two_d(m[n]), two_d(v[n]), name=f"adamw_{n}")
        g_out[n], d_out[n], m_out[n], v_out[n] = (r.reshape(shape) for r in res)
    res = _adamw(small_parts, _pack(w), _pack(m), _pack(v), name="adamw_small")
    for dst, packed in zip((g_out, d_out, m_out, v_out), res):
        dst.update(_unpack(packed, w))

    return (loss, dx[None], *[g_out[n] for n in WEIGHTS], *[d_out[n] for n in WEIGHTS],
            *[m_out[n] for n in WEIGHTS], *[v_out[n] for n in WEIGHTS])
```

```python
import functools
import math

import jax
import jax.numpy as jnp
from jax import lax
from jax.experimental import pallas as pl
from jax.experimental.pallas import tpu as pltpu

F32, BF16 = jnp.float32, jnp.bfloat16
HIGHEST = lax.Precision.HIGHEST
MESH = pl.DeviceIdType.MESH
ANY = pl.BlockSpec(memory_space=pl.ANY)
SDS = jax.ShapeDtypeStruct

N_DEV = 8
DEPTH = 4
D_MODEL = 1024
D_ATTN = 512
D_SGU = 512
N_HEADS = 8
HEAD_DIM = 64
HEAD_PAIRS = N_HEADS // 2
SGU_CHUNK = 128
SGU_BLOCK = 64
N_GROUPS = 8
GROUP_DIM = 64
D_FF = 2816
FF_CHUNK = 1408
N_FF_CHUNKS = D_FF // FF_CHUNK
D_IN = 3 * D_ATTN + N_HEADS + 2 * D_SGU
LANES = 128
D_IN_PADDED = 3 * D_ATTN + LANES + 2 * D_SGU
EPS = 1e-6
QK_SCALE = HEAD_DIM ** -0.5
INV_SQRT2 = 1.0 / math.sqrt(2.0)
INV_SQRT_2PI = 1.0 / math.sqrt(2.0 * math.pi)
LOG2E = 1.0 / math.log(2.0)
LN2 = math.log(2.0)
ROW_CHUNK = 32
KX_ROWS = LANES + 16

ADAM_LR = 0.001
ADAM_B1 = 0.9
ADAM_B2 = 0.999
ADAM_EPS = 1e-08
ADAM_WD = 0.01
ADAM_STEP = 10

VMEM_LIMIT_BYTES = 56 * 1024 * 1024


def _params(*sem):
    return pltpu.CompilerParams(dimension_semantics=sem or None, vmem_limit_bytes=VMEM_LIMIT_BYTES)


def _dot(a, b):
    return jnp.dot(a, b, preferred_element_type=F32)


def _dot_nt(a, b):
    return lax.dot_general(a, b, (((1,), (1,)), ((), ())), preferred_element_type=F32)


def _dot_tn(a, b):
    return lax.dot_general(a, b, (((0,), (0,)), ((), ())), preferred_element_type=F32)


def _dot_exact(a, b):
    return jnp.dot(a, b, precision=HIGHEST, preferred_element_type=F32)


def _mean(v):
    return jnp.mean(v, axis=-1, keepdims=True)


def _sigmoid(v):
    return 1.0 / (1.0 + jnp.exp(-v))


def _row_spec(tm, n):
    return pl.BlockSpec((tm, n), lambda i: (i, 0))


def _rev_spec(tm, n, nt):
    return pl.BlockSpec((tm, n), lambda i: (nt - 1 - i, 0))


def _const_spec(shape):
    return pl.BlockSpec(shape, lambda i: (0,) * len(shape))


def _tiles(t):
    return min(256, t), min(512, t), min(2048, t)


def _group_indicator():
    r = lax.broadcasted_iota(jnp.int32, (D_ATTN, LANES), 0)
    c = lax.broadcasted_iota(jnp.int32, (D_ATTN, LANES), 1)
    return ((r >> 6) == c).astype(F32)


def _sgu_mix(w_ref, zc, lane_grp):
    out = jnp.zeros((SGU_CHUNK, D_SGU), F32)
    for g in range(N_GROUPS):
        out = out + jnp.where(lane_grp == g, _dot(w_ref[g], zc), 0.0)
    return out


def _fwd_a(x, gmix, w_in, bf, lng, lnb, wsm, bsf, *, tm, gather=()):
    t = x.shape[0]
    nt = t // tm
    nch = tm // SGU_CHUNK

    def body(x_ref, gmix_ref, w_ref, bf_ref, lng_ref, lnb_ref, wsm_ref, bsf_ref,
             q_ref, k_ref, v_ref, fl_ref, z_ref, sgu_ref, carry_ref):
        @pl.when(pl.program_id(0) == 0)
        def _():
            carry_ref[...] = jnp.zeros_like(carry_ref)

        xt = x_ref[...]
        r = lax.rsqrt(_mean(xt * xt) + EPS)
        xn = ((xt * r) * gmix_ref[...]).astype(BF16)
        proj = _dot(xn, w_ref[...])
        qkv = proj[:, :3 * D_ATTN]

        fl = proj[:, 3 * D_ATTN:3 * D_ATTN + LANES] + bf_ref[...]
        fl_ref[...] = fl
        logf = jnp.minimum(fl, 0.0) - jnp.log1p(jnp.exp(-jnp.abs(fl)))
        row = lax.broadcasted_iota(jnp.int32, (tm, tm), 0)
        col = lax.broadcasted_iota(jnp.int32, (tm, tm), 1)
        c = _dot_exact((col <= row).astype(F32), logf) + carry_ref[...]
        carry_ref[...] = c[tm - 1:tm, :]

        c2 = c * LOG2E
        lane = lax.broadcasted_iota(jnp.int32, (tm, LANES), 1)
        for h in range(N_HEADS):
            pair, hh = divmod(h, 2)
            base = _aug_lane(hh)
            in_head = (lane >= hh * HEAD_DIM) & (lane < (hh + 1) * HEAD_DIM)
            col_h = jnp.sum(jnp.where(lane == h, c2, 0.0), axis=1, keepdims=True)
            hi = col_h.astype(BF16).astype(F32)
            mid = (col_h - hi).astype(BF16).astype(F32)
            lo = (col_h - hi) - mid
            split = jnp.where(lane == base, hi, jnp.where(lane == base + 1, mid, jnp.where(lane == base + 2, lo, 0.0)))
            split_k = jnp.where(lane == base + 3, hi, jnp.where(lane == base + 4, mid,
                                                                 jnp.where(lane == base + 5, lo, 0.0)))
            ones_q = ((lane >= base + 3) & (lane < base + 6)).astype(F32)
            ones_k = ((lane >= base) & (lane < base + 3)).astype(F32)
            blk = slice(pair * LANES, (pair + 1) * LANES)
            q_ref[h] = jnp.where(in_head, qkv[:, blk] * (QK_SCALE * LOG2E), split + ones_q).astype(BF16)
            k_ref[h] = jnp.where(in_head, qkv[:, D_ATTN:2 * D_ATTN][:, blk], ones_k - split_k).astype(BF16)
            v_ref[h] = jnp.where(in_head, qkv[:, 2 * D_ATTN:][:, blk], (lane == base).astype(F32)).astype(BF16)

        z = proj[:, 3 * D_ATTN + LANES:]
        z_ref[...] = z
        zg = 0.5 * z * (1.0 + lax.erf(z * INV_SQRT2))
        zu = zg[:, :D_SGU]
        zv = zg[:, D_SGU:]
        xc = zv - _mean(zv)
        zvn = ((xc * lax.rsqrt(_mean(xc * xc) + EPS)) * lng_ref[...] + lnb_ref[...]).astype(BF16)
        lane_grp = lax.broadcasted_iota(jnp.int32, (SGU_CHUNK, D_SGU), 1) >> 6
        for ch in range(nch):
            rows = slice(ch * SGU_CHUNK, (ch + 1) * SGU_CHUNK)
            mixed = _sgu_mix(wsm_ref, zvn[rows, :], lane_grp) + bsf_ref[...]
            sgu_ref[rows, :] = zu[rows, :] * mixed

    head_spec = pl.BlockSpec((N_HEADS, tm, LANES), lambda i: (0, i, 0))
    return _row_tile_call(
        body, "fwd_a", nt, (x, gmix, w_in, bf, lng, lnb, wsm, bsf), exchange=gather, scatter=False,
        in_specs=[_row_spec(tm, D_MODEL), _const_spec((1, D_MODEL)), _const_spec((D_MODEL, D_IN_PADDED)),
                  _const_spec((1, LANES)), _const_spec((1, D_SGU)), _const_spec((1, D_SGU)),
                  _const_spec((N_GROUPS, SGU_CHUNK, SGU_CHUNK)), _const_spec((SGU_CHUNK, D_SGU))],
        out_specs=[head_spec, head_spec, head_spec, _row_spec(tm, LANES), _row_spec(tm, 2 * D_SGU),
                   _row_spec(tm, D_SGU)],
        out_shape=[SDS((N_HEADS, t, LANES), BF16)] * 3 + [SDS((t, LANES), F32), SDS((t, 2 * D_SGU), F32),
                                                          SDS((t, D_SGU), F32)],
        scratch_shapes=[pltpu.VMEM((1, LANES), F32)])


def _aug_lane(hh):
    return (1 - hh) * HEAD_DIM


def _attn_fwd(qa, ka, va, *, tq):
    t = qa.shape[1]
    nq = t // tq
    nrc = tq // ROW_CHUNK

    def body(q_ref, k_hbm, v_hbm, o_ref, lse_ref, k_vm, vt_vm, v_tmp, qt_ref, s_ref, p_ref, m_ref, a_ref, acc_ref):
        p = pl.program_id(0)
        i = pl.program_id(1)

        @pl.when(i == 0)
        def _():
            pltpu.sync_copy(k_hbm.at[pl.ds(2 * p, 2)], k_vm)

            def transpose_block(b, carry):
                start = pl.multiple_of(b * tq, tq)
                for h in range(2):
                    pltpu.sync_copy(v_hbm.at[2 * p + h, pl.ds(start, tq), :], v_tmp.at[h])
                    vt_vm[h, :, pl.ds(start, tq)] = v_tmp[h].astype(F32).T.astype(BF16)
                return carry

            lax.fori_loop(0, nq, transpose_block, 0)

        for h in range(2):
            qt_ref[h] = q_ref[h].astype(F32).T.astype(BF16)
        m_ref[...] = jnp.full(m_ref.shape, -jnp.inf, F32)
        acc_ref[...] = jnp.zeros_like(acc_ref)
        rowk = lax.broadcasted_iota(jnp.int32, (ROW_CHUNK, tq), 0)
        colq = lax.broadcasted_iota(jnp.int32, (ROW_CHUNK, tq), 1)

        def scores(j, slot):
            start = pl.multiple_of(j * tq, tq)
            for h in range(2):
                s_ref[slot, h] = _dot(k_vm[h, pl.ds(start, tq), :], qt_ref[h])

        def chunk(slot, h, r, masked):
            sc = s_ref[slot, h, r * ROW_CHUNK:(r + 1) * ROW_CHUNK, :]
            if masked:
                sc = jnp.where(rowk + r * ROW_CHUNK <= colq, sc, -jnp.inf)
            return sc

        def softmax(slot, masked):
            for h in range(2):
                m_old = m_ref[h]
                top = jnp.full((8, tq), -jnp.inf, F32)
                for r in range(nrc):
                    sc = chunk(slot, h, r, masked)
                    for g in range(ROW_CHUNK // 8):
                        top = jnp.maximum(top, sc[g * 8:(g + 1) * 8, :])
                m_new = jnp.maximum(m_old, jnp.max(top, axis=0, keepdims=True))
                for r in range(nrc):
                    rows = slice(r * ROW_CHUNK, (r + 1) * ROW_CHUNK)
                    p_ref[slot, h, rows, :] = jnp.exp2(chunk(slot, h, r, masked) - m_new).astype(BF16)
                a_ref[slot, h] = jnp.exp2(m_old - m_new)
                m_ref[h] = m_new

        def accumulate(j, slot):
            start = pl.multiple_of(j * tq, tq)
            for h in range(2):
                acc_ref[h] = acc_ref[h] * a_ref[slot, h] + _dot(vt_vm[h, :, pl.ds(start, tq)], p_ref[slot, h])

        scores(0, 0)

        @pl.when(i > 0)
        def _():
            scores(1, 1)
            softmax(0, False)

        def stage(j, slot):
            scores(j + 1, 1 - slot)
            softmax(slot, False)
            accumulate(j - 1, 1 - slot)

        def pair_body(n, carry):
            j = 1 + 2 * n
            stage(j, 1)
            stage(j + 1, 0)
            return carry

        rest = i - 1
        lax.fori_loop(0, rest // 2, pair_body, 0)

        @pl.when((rest > 0) & (lax.rem(rest, 2) == 1))
        def _():
            stage(i - 1, 1)

        @pl.when(lax.rem(i, 2) == 0)
        def _():
            softmax(0, True)

            @pl.when(i > 0)
            def _():
                accumulate(i - 1, 1)

            accumulate(i, 0)

        @pl.when(lax.rem(i, 2) == 1)
        def _():
            softmax(1, True)
            accumulate(i - 1, 0)
            accumulate(i, 1)

        row = lax.broadcasted_iota(jnp.int32, (LANES, tq), 0)
        l_h = [acc_ref[h, _aug_lane(h):_aug_lane(h) + 1, :] for h in range(2)]
        o_ref[...] = jnp.where(row < HEAD_DIM, acc_ref[0] / l_h[0], acc_ref[1] / l_h[1]).T
        lse_ref[0] = jnp.zeros((8, tq), F32)
        for h in range(2):
            lse_ref[0, h:h + 1, :] = m_ref[h] + jnp.log2(l_h[h])

    return pl.pallas_call(
        body, name="attn_fwd", grid=(HEAD_PAIRS, nq),
        in_specs=[pl.BlockSpec((2, tq, LANES), lambda p, i: (p, i, 0)), ANY, ANY],
        out_specs=[pl.BlockSpec((tq, LANES), lambda p, i: (i, p)),
                   pl.BlockSpec((1, 8, tq), lambda p, i: (p, 0, i))],
        out_shape=[SDS((t, D_ATTN), F32), SDS((HEAD_PAIRS, 8, t), F32)],
        scratch_shapes=[pltpu.VMEM((2, t, LANES), BF16), pltpu.VMEM((2, LANES, t), BF16),
                        pltpu.VMEM((2, tq, LANES), BF16), pltpu.VMEM((2, LANES, tq), BF16),
                        pltpu.VMEM((2, 2, tq, tq), F32), pltpu.VMEM((2, 2, tq, tq), BF16),
                        pltpu.VMEM((2, 1, tq), F32), pltpu.VMEM((2, 2, 1, tq), F32), pltpu.VMEM((2, LANES, tq), F32)],
        compiler_params=_params("arbitrary", "arbitrary"),
    )(qa, ka, va)


def _attn_bwd(qa, ka, va, do, lse_row, delta_row, *, tq):
    t = qa.shape[1]
    nq = t // tq
    nrc = tq // ROW_CHUNK

    def body(q_hbm, do_ref, k_ref, v_ref, lse_ref, dl_ref, dqt_ref, dk_ref, dv_ref, dck_ref, dcq_ref,
             q_vm, st_ref, dp_ref, pt_ref, ds_ref, dka_ref, dva_ref, vh_ref, kx_ref):
        p = pl.program_id(0)
        j = pl.program_id(1)

        @pl.when(j == 0)
        def _():
            pltpu.sync_copy(q_hbm.at[pl.ds(2 * p, 2)], q_vm)
            dqt_ref[...] = jnp.zeros_like(dqt_ref)
            dcq_ref[...] = jnp.zeros_like(dcq_ref)

        dka_ref[...] = jnp.zeros_like(dka_ref)
        dva_ref[...] = jnp.zeros_like(dva_ref)
        lane = lax.broadcasted_iota(jnp.int32, (tq, LANES), 1)
        in_head = (lane < HEAD_DIM, lane >= HEAD_DIM)
        for h in range(2):
            zero = jnp.zeros((tq, LANES), BF16)
            vh_ref[h] = jnp.where(in_head[h], v_ref[h], zero)
            kx_ref[h, :LANES, :] = jnp.where(in_head[h], k_ref[h], zero).astype(F32).T.astype(BF16)
            kx_ref[h, LANES:, :] = jnp.ones((KX_ROWS - LANES, tq), BF16)
        rowk = lax.broadcasted_iota(jnp.int32, (ROW_CHUNK, tq), 0)
        colq = lax.broadcasted_iota(jnp.int32, (ROW_CHUNK, tq), 1)

        def step(i, masked, slot):
            start = pl.multiple_of(i * tq, tq)
            cols = pl.ds(start, tq)
            do2 = do_ref[pl.ds(start, tq), :]
            q_h = [q_vm[h, pl.ds(start, tq), :] for h in range(2)]
            for h in range(2):
                st_ref[slot, h] = _dot_nt(k_ref[h], q_h[h])
                dp_ref[slot, h] = _dot_nt(vh_ref[h], do2)
            for h in range(2):
                lse = lse_ref[0, h:h + 1, cols]
                delta = dl_ref[0, h:h + 1, cols]
                for r in range(nrc):
                    rows = slice(r * ROW_CHUNK, (r + 1) * ROW_CHUNK)
                    st = st_ref[slot, h, rows, :]
                    if masked:
                        st = jnp.where(rowk + r * ROW_CHUNK <= colq, st, -jnp.inf)
                    pt = jnp.exp2(st - lse)
                    pt_ref[slot, h, rows, :] = pt.astype(BF16)
                    ds_ref[slot, h, rows, :] = (pt * (dp_ref[slot, h, rows, :] - delta)).astype(BF16)
            dq_t = jnp.zeros((LANES, tq), F32)
            for h in range(2):
                dva_ref[h] += _dot(pt_ref[slot, h], do2)
                dka_ref[h] += _dot(ds_ref[slot, h], q_h[h])
                ext = _dot(kx_ref[h], ds_ref[slot, h])
                dq_t = dq_t + ext[:LANES, :]
                dcq_ref[0, h:h + 1, cols] += ext[LANES:LANES + 1, :]
            dqt_ref[0, :, cols] += dq_t

        def pair_body(n, carry):
            i = j + 1 + 2 * n
            step(i, False, 0)
            step(i + 1, False, 1)
            return carry

        step(j, True, 0)
        after = nq - 1 - j
        lax.fori_loop(0, after // 2, pair_body, 0)

        @pl.when(lax.rem(after, 2) == 1)
        def _():
            step(nq - 1, False, 0)

        dk_ref[...] = (jnp.where(in_head[0], dka_ref[0], dka_ref[1]) * LN2).astype(BF16)
        dv_ref[...] = jnp.where(in_head[0], dva_ref[0], dva_ref[1]).astype(BF16)
        own = pl.ds(pl.multiple_of(j * tq, tq), tq)
        for h in range(2):
            at = _aug_lane(h) + 3
            dck_ref[0, h:h + 1, own] = -dka_ref[h].T[at:at + 1, :]

    rows = pl.BlockSpec((1, 2, t), lambda p, j: (p, 0, 0))
    tiles = pl.BlockSpec((1, LANES, t), lambda p, j: (p, 0, 0))
    return pl.pallas_call(
        body, name="attn_bwd", grid=(HEAD_PAIRS, nq),
        in_specs=[ANY, pl.BlockSpec((t, LANES), lambda p, j: (0, p)),
                  pl.BlockSpec((2, tq, LANES), lambda p, j: (p, j, 0)),
                  pl.BlockSpec((2, tq, LANES), lambda p, j: (p, j, 0)), rows, rows],
        out_specs=[tiles,
                   pl.BlockSpec((tq, LANES), lambda p, j: (j, p)),
                   pl.BlockSpec((tq, LANES), lambda p, j: (j, p)), rows, rows],
        out_shape=[SDS((HEAD_PAIRS, LANES, t), F32), SDS((t, D_ATTN), BF16), SDS((t, D_ATTN), BF16),
                   SDS((HEAD_PAIRS, 2, t), F32), SDS((HEAD_PAIRS, 2, t), F32)],
        scratch_shapes=[pltpu.VMEM((2, t, LANES), BF16), pltpu.VMEM((2, 2, tq, tq), F32),
                        pltpu.VMEM((2, 2, tq, tq), F32), pltpu.VMEM((2, 2, tq, tq), BF16),
                        pltpu.VMEM((2, 2, tq, tq), BF16),
                        pltpu.VMEM((2, tq, LANES), F32), pltpu.VMEM((2, tq, LANES), F32),
                        pltpu.VMEM((2, tq, LANES), BF16), pltpu.VMEM((2, KX_ROWS, tq), BF16)],
        compiler_params=_params("arbitrary", "arbitrary"),
    )(qa, do, ka, va, lse_row, delta_row)

def _load_weights_once(pairs):
    @pl.when(pl.program_id(0) == 0)
    def _():
        for src, dst in pairs:
            pltpu.sync_copy(src, dst)


def _row_tile_call(body, name, nt, operands, *, in_specs, out_specs, out_shape, scratch_shapes, exchange, scatter):
    if not exchange:
        return pl.pallas_call(body, name=name, grid=(nt,), in_specs=in_specs, out_specs=out_specs,
                              out_shape=out_shape, scratch_shapes=scratch_shapes,
                              compiler_params=_params("arbitrary"))(*operands)
    flags = [scatter] * len(exchange)
    n = len(exchange)
    return pl.pallas_call(
        _fused_exchange(body, len(operands), len(out_shape), flags, nt),
        name=name + ("_scatter" if scatter else "_gather"), grid=(nt,),
        in_specs=list(in_specs) + [ANY] * n, out_specs=list(out_specs) + [ANY] * n,
        out_shape=list(out_shape) + _exchange_shapes(exchange, flags),
        scratch_shapes=list(scratch_shapes) + _exchange_scratch(n),
        compiler_params=_params("arbitrary"))(*operands, *exchange)


def _fwd_b(x, o, sgu, gout, gffn, w_out, w_gu, w_dn, *, tm, gather=()):
    t = x.shape[0]
    nt = t // tm

    def body(x_ref, o_ref, s_ref, gout_ref, gffn_ref, wout_hbm, wgu_hbm, wdn_hbm,
             x1_ref, x2_ref, gu_ref, wout, wgu, wdn):
        _load_weights_once(((wout_hbm, wout), (wgu_hbm, wgu), (wdn_hbm, wdn)))
        ov = o_ref[...]
        sv = s_ref[...]
        mo = ((ov * lax.rsqrt(_mean(ov * ov) + EPS)) * gout_ref[:, :D_ATTN]).astype(BF16)
        ms = ((sv * lax.rsqrt(_mean(sv * sv) + EPS)) * gout_ref[:, D_ATTN:]).astype(BF16)
        x1 = x_ref[...] + (_dot(mo, wout[:D_ATTN, :]) + _dot(ms, wout[D_ATTN:, :]))
        x1_ref[...] = x1
        xn2 = ((x1 * lax.rsqrt(_mean(x1 * x1) + EPS)) * gffn_ref[...]).astype(BF16)
        y = jnp.zeros((tm, D_MODEL), F32)
        for n in range(N_FF_CHUNKS):
            lo, hi = n * FF_CHUNK, (n + 1) * FF_CHUNK
            gate = _dot(xn2, wgu[:, lo:hi])
            up = _dot(xn2, wgu[:, D_FF + lo:D_FF + hi])
            gu_ref[:, lo:hi] = gate
            gu_ref[:, D_FF + lo:D_FF + hi] = up
            a = ((gate * _sigmoid(gate)) * up).astype(BF16)
            y = y + _dot(a, wdn[lo:hi, :])
        x2_ref[...] = x1 + y

    return _row_tile_call(
        body, "fwd_b", nt, (x, o, sgu, gout, gffn, w_out, w_gu, w_dn),
        in_specs=[_row_spec(tm, D_MODEL), _row_spec(tm, D_ATTN), _row_spec(tm, D_SGU),
                  _const_spec((1, D_MODEL)), _const_spec((1, D_MODEL)), ANY, ANY, ANY],
        out_specs=[_row_spec(tm, D_MODEL), _row_spec(tm, D_MODEL), _row_spec(tm, 2 * D_FF)],
        out_shape=[SDS((t, D_MODEL), F32), SDS((t, D_MODEL), F32), SDS((t, 2 * D_FF), F32)],
        scratch_shapes=[pltpu.VMEM((D_MODEL, D_MODEL), BF16), pltpu.VMEM((D_MODEL, 2 * D_FF), BF16),
                        pltpu.VMEM((D_FF, D_MODEL), BF16)],
        exchange=gather, scatter=False)


def _loss_bwd(x, tgt, gfin, *, tm):
    t = x.shape[0]
    nt = t // tm

    def body(x_ref, t_ref, g_ref, dx_ref, loss_ref, dg_ref):
        @pl.when(pl.program_id(0) == 0)
        def _():
            loss_ref[...] = jnp.zeros_like(loss_ref)
            dg_ref[...] = jnp.zeros_like(dg_ref)

        xt = x_ref[...]
        g = g_ref[...]
        r = lax.rsqrt(_mean(xt * xt) + EPS)
        xh = xt * r
        err = xh * g - t_ref[...]
        loss_ref[...] += 0.5 * jnp.sum(_mean(err * err), axis=0, keepdims=True)
        dy = err * (1.0 / D_MODEL)
        dg_ref[...] += jnp.sum(dy * xh, axis=0, keepdims=True)
        dyg = dy * g
        dx_ref[...] = r * (dyg - xh * _mean(dyg * xh))

    return pl.pallas_call(
        body, name="loss_bwd", grid=(nt,),
        in_specs=[_row_spec(tm, D_MODEL), _row_spec(tm, D_MODEL), _const_spec((1, D_MODEL))],
        out_specs=[_row_spec(tm, D_MODEL), _const_spec((1, 1)), _const_spec((1, D_MODEL))],
        out_shape=[SDS((t, D_MODEL), F32), SDS((1, 1), F32), SDS((1, D_MODEL), F32)],
        compiler_params=_params("arbitrary"),
    )(x, tgt, gfin)


def _bwd_b(dx2, x1, gu, o, sgu, gout, gffn, w_out, w_gu, w_dn, *, tm, scatter=()):
    t = dx2.shape[0]
    nt = t // tm

    def body(dx2_ref, x1_ref, gu_ref, o_ref, s_ref, gout_ref, gffn_ref, wout_hbm, wgu_hbm, wdn_hbm,
             dx1_ref, dx2b_ref, a_ref, dgu_ref, xn2_ref, mrg_ref, dx1b_ref, do_ref, dl_ref, dsgu_ref,
             dgffn_ref, dgout_ref, wout, wgu, wdn):
        _load_weights_once(((wout_hbm, wout), (wgu_hbm, wgu), (wdn_hbm, wdn)))

        @pl.when(pl.program_id(0) == 0)
        def _():
            dgffn_ref[...] = jnp.zeros_like(dgffn_ref)
            dgout_ref[...] = jnp.zeros_like(dgout_ref)

        dx2 = dx2_ref[...]
        dx2b = dx2.astype(BF16)
        dx2b_ref[...] = dx2b
        dxn2 = jnp.zeros((tm, D_MODEL), F32)
        for n in range(N_FF_CHUNKS):
            lo, hi = n * FF_CHUNK, (n + 1) * FF_CHUNK
            gate = gu_ref[:, lo:hi]
            up = gu_ref[:, D_FF + lo:D_FF + hi]
            sg = _sigmoid(gate)
            si = gate * sg
            a_ref[:, lo:hi] = (si * up).astype(BF16)
            d_a = _dot_nt(dx2b, wdn[lo:hi, :])
            dgb = ((d_a * up) * (sg * (1.0 + gate * (1.0 - sg)))).astype(BF16)
            dub = (d_a * si).astype(BF16)
            dgu_ref[:, lo:hi] = dgb
            dgu_ref[:, D_FF + lo:D_FF + hi] = dub
            dxn2 = dxn2 + (_dot_nt(dgb, wgu[:, lo:hi]) + _dot_nt(dub, wgu[:, D_FF + lo:D_FF + hi]))

        x1 = x1_ref[...]
        gffn = gffn_ref[...]
        r1 = lax.rsqrt(_mean(x1 * x1) + EPS)
        xh1 = x1 * r1
        xn2_ref[...] = (xh1 * gffn).astype(BF16)
        dgffn_ref[...] += jnp.sum(dxn2 * xh1, axis=0, keepdims=True)
        dyg = dxn2 * gffn
        dx1 = dx2 + r1 * (dyg - xh1 * _mean(dyg * xh1))
        dx1_ref[...] = dx1
        dx1b = dx1.astype(BF16)
        dx1b_ref[...] = dx1b

        ov = o_ref[...]
        sv = s_ref[...]
        go = gout_ref[:, :D_ATTN]
        gs = gout_ref[:, D_ATTN:]
        ro = lax.rsqrt(_mean(ov * ov) + EPS)
        rs = lax.rsqrt(_mean(sv * sv) + EPS)
        oh = ov * ro
        sh = sv * rs
        mrg_ref[:, :D_ATTN] = (oh * go).astype(BF16)
        mrg_ref[:, D_ATTN:] = (sh * gs).astype(BF16)
        dmo = _dot_nt(dx1b, wout[:D_ATTN, :])
        dms = _dot_nt(dx1b, wout[D_ATTN:, :])
        dgout_ref[:, :D_ATTN] += jnp.sum(dmo * oh, axis=0, keepdims=True)
        dgout_ref[:, D_ATTN:] += jnp.sum(dms * sh, axis=0, keepdims=True)
        dmog = dmo * go
        d_o = ro * (dmog - oh * _mean(dmog * oh))
        do_ref[...] = d_o.astype(BF16)
        dl_ref[...] = _dot_exact(d_o * ov, _group_indicator()).T[:N_HEADS, :]
        dmsg = dms * gs
        dsgu_ref[...] = rs * (dmsg - sh * _mean(dmsg * sh))

    return _row_tile_call(
        body, "bwd_b", nt, (dx2, x1, gu, o, sgu, gout, gffn, w_out, w_gu, w_dn),
        in_specs=[_row_spec(tm, D_MODEL), _row_spec(tm, D_MODEL), _row_spec(tm, 2 * D_FF), _row_spec(tm, D_ATTN),
                  _row_spec(tm, D_SGU), _const_spec((1, D_MODEL)), _const_spec((1, D_MODEL)), ANY, ANY, ANY],
        out_specs=[_row_spec(tm, D_MODEL), _row_spec(tm, D_MODEL), _row_spec(tm, D_FF), _row_spec(tm, 2 * D_FF),
                   _row_spec(tm, D_MODEL), _row_spec(tm, D_MODEL), _row_spec(tm, D_MODEL), _row_spec(tm, D_ATTN),
                   pl.BlockSpec((N_HEADS, tm), lambda i: (0, i)), _row_spec(tm, D_SGU),
                   _const_spec((1, D_MODEL)), _const_spec((1, D_MODEL))],
        out_shape=[SDS((t, D_MODEL), F32), SDS((t, D_MODEL), BF16), SDS((t, D_FF), BF16), SDS((t, 2 * D_FF), BF16),
                   SDS((t, D_MODEL), BF16), SDS((t, D_MODEL), BF16), SDS((t, D_MODEL), BF16),
                   SDS((t, D_ATTN), BF16), SDS((N_HEADS, t), F32), SDS((t, D_SGU), F32),
                   SDS((1, D_MODEL), F32), SDS((1, D_MODEL), F32)],
        scratch_shapes=[pltpu.VMEM((D_MODEL, D_MODEL), BF16), pltpu.VMEM((D_MODEL, 2 * D_FF), BF16),
                        pltpu.VMEM((D_FF, D_MODEL), BF16)],
        exchange=scatter, scatter=True)


def _bwd_a(dx1, x, z, fl, dsgu, dq, dk, dv, dc, gmix, w_in, lng, lnb, wsm, wsm_t, bsf, mask, *, tm, scatter=()):
    t = x.shape[0]
    nt = t // tm
    nch = tm // SGU_CHUNK

    def body(dx1_ref, x_ref, z_ref, fl_ref, dsgu_ref, dq_ref, dk_ref, dv_ref, dc_ref, gmix_ref, w_ref,
             lng_ref, lnb_ref, wsm_ref, wsmt_ref, bsf_ref, mask_ref,
             dx_ref, xn_ref, dh_ref, dgmix_ref, dbf_ref, dlng_ref, dlnb_ref, dws_ref, dbs_ref,
             carry_ref, dzvn_ref, dzu_ref, dbacc_ref):
        step = pl.program_id(0)

        @pl.when(step == 0)
        def _():
            carry_ref[...] = jnp.zeros_like(carry_ref)
            dbacc_ref[...] = jnp.zeros_like(dbacc_ref)
            for ref in (dgmix_ref, dbf_ref, dlng_ref, dlnb_ref, dws_ref):
                ref[...] = jnp.zeros_like(ref)

        z = z_ref[...]
        erf = lax.erf(z * INV_SQRT2)
        cdf = 0.5 * (1.0 + erf)
        zg = z * cdf
        zu = zg[:, :D_SGU]
        zv = zg[:, D_SGU:]
        xc = zv - _mean(zv)
        rln = lax.rsqrt(_mean(xc * xc) + EPS)
        zh = xc * rln
        lng = lng_ref[...]
        zvn = (zh * lng + lnb_ref[...]).astype(BF16)
        dsgu = dsgu_ref[...]
        lane_grp = lax.broadcasted_iota(jnp.int32, (SGU_CHUNK, D_SGU), 1) >> 6
        for ch in range(nch):
            rows = slice(ch * SGU_CHUNK, (ch + 1) * SGU_CHUNK)
            zc = zvn[rows, :]
            ds_c = dsgu[rows, :]
            mixed = _sgu_mix(wsm_ref, zc, lane_grp) + bsf_ref[...]
            dzu_ref[rows, :] = ds_c * mixed
            dmix = ds_c * zu[rows, :]
            dbacc_ref[...] += dmix
            dmb = dmix.astype(BF16)
            dzvn_ref[rows, :] = _sgu_mix(wsmt_ref, dmb, lane_grp)
            for g in range(N_GROUPS):
                dws_ref[g] += _dot_nt(jnp.where(lane_grp == g, dmb, jnp.zeros_like(dmb)), zc)
        dzvn = dzvn_ref[...]
        dlng_ref[...] += jnp.sum(dzvn * zh, axis=0, keepdims=True)
        dlnb_ref[...] += jnp.sum(dzvn, axis=0, keepdims=True)
        dzh = dzvn * lng
        dzv = rln * ((dzh - _mean(dzh)) - zh * _mean(dzh * zh))
        pdf = jnp.exp(-0.5 * (z * z)) * INV_SQRT_2PI
        dgelu = cdf + z * pdf
        z_at = 3 * D_ATTN + LANES
        dh_ref[:, z_at:z_at + D_SGU] = (dzu_ref[...] * dgelu[:, :D_SGU]).astype(BF16)
        dh_ref[:, z_at + D_SGU:] = (dzv * dgelu[:, D_SGU:]).astype(BF16)

        dc = jnp.concatenate([dc_ref[...], jnp.zeros((LANES - N_HEADS, tm), F32)], axis=0).T
        row = lax.broadcasted_iota(jnp.int32, (tm, tm), 0)
        col = lax.broadcasted_iota(jnp.int32, (tm, tm), 1)
        dlogf = _dot_exact((col >= row).astype(F32), dc) + carry_ref[...]
        carry_ref[...] = dlogf[0:1, :]
        dfl = dlogf * _sigmoid(-fl_ref[...])
        dbf_ref[...] += jnp.sum(dfl, axis=0, keepdims=True)
        dh_ref[:, 3 * D_ATTN:z_at] = dfl.astype(BF16)

        for pair in range(HEAD_PAIRS):
            dh_ref[:, pair * LANES:(pair + 1) * LANES] = (dq_ref[pair].T * QK_SCALE).astype(BF16)
        dh_ref[:, D_ATTN:2 * D_ATTN] = dk_ref[...]
        dh_ref[:, 2 * D_ATTN:3 * D_ATTN] = dv_ref[...]
        dxn = _dot_nt(dh_ref[...], w_ref[...])

        xt = x_ref[...]
        gmix = gmix_ref[...]
        r = lax.rsqrt(_mean(xt * xt) + EPS)
        xh = xt * r
        xn_ref[...] = (xh * gmix).astype(BF16)
        dgmix_ref[...] += jnp.sum(dxn * xh, axis=0, keepdims=True)
        dyg = dxn * gmix
        dx_ref[...] = dx1_ref[...] + r * (dyg - xh * _mean(dyg * xh))

        @pl.when(step == nt - 1)
        def _():
            for g in range(N_GROUPS):
                dws_ref[g] = dws_ref[g] * mask_ref[...]
            dbs_ref[...] = _dot_exact(dbacc_ref[...], _group_indicator())

    rev = functools.partial(_rev_spec, nt=nt)
    return _row_tile_call(
        body, "bwd_a", nt, (dx1, x, z, fl, dsgu, dq, dk, dv, dc, gmix, w_in, lng, lnb, wsm, wsm_t, bsf, mask),
        exchange=scatter, scatter=True,
        in_specs=[rev(tm, D_MODEL), rev(tm, D_MODEL), rev(tm, 2 * D_SGU), rev(tm, LANES), rev(tm, D_SGU),
                  pl.BlockSpec((HEAD_PAIRS, LANES, tm), lambda i: (0, 0, nt - 1 - i)), rev(tm, D_ATTN), rev(tm, D_ATTN),
                  pl.BlockSpec((N_HEADS, tm), lambda i: (0, nt - 1 - i)),
                  _const_spec((1, D_MODEL)), _const_spec((D_MODEL, D_IN_PADDED)),
                  _const_spec((1, D_SGU)), _const_spec((1, D_SGU)),
                  _const_spec((N_GROUPS, SGU_CHUNK, SGU_CHUNK)), _const_spec((N_GROUPS, SGU_CHUNK, SGU_CHUNK)),
                  _const_spec((SGU_CHUNK, D_SGU)), _const_spec((SGU_CHUNK, SGU_CHUNK))],
        out_specs=[rev(tm, D_MODEL), rev(tm, D_MODEL), rev(tm, D_IN_PADDED),
                   _const_spec((1, D_MODEL)), _const_spec((1, LANES)), _const_spec((1, D_SGU)), _const_spec((1, D_SGU)),
                   _const_spec((N_GROUPS, SGU_CHUNK, SGU_CHUNK)), _const_spec((SGU_CHUNK, LANES))],
        out_shape=[SDS((t, D_MODEL), F32), SDS((t, D_MODEL), BF16), SDS((t, D_IN_PADDED), BF16), SDS((1, D_MODEL), F32), SDS((1, LANES), F32), SDS((1, D_SGU), F32),
                   SDS((1, D_SGU), F32), SDS((N_GROUPS, SGU_CHUNK, SGU_CHUNK), F32), SDS((SGU_CHUNK, LANES), F32)],
        scratch_shapes=[pltpu.VMEM((1, LANES), F32), pltpu.VMEM((tm, D_SGU), F32), pltpu.VMEM((tm, D_SGU), F32),
                        pltpu.VMEM((SGU_CHUNK, D_SGU), F32)])


def _pick(n, cap):
    if n <= cap:
        return n
    best = LANES
    for cand in range(LANES, cap + 1, LANES):
        if n % cand == 0:
            best = cand
    return best


def _tn_matmul(a, b, *, bt):
    t, k1 = a.shape
    n = b.shape[1]
    bk = _pick(k1, 1408)
    bn = _pick(n, 1408)
    nsteps = t // bt

    def body(a_ref, b_ref, o_ref):
        @pl.when(pl.program_id(2) == 0)
        def _():
            o_ref[...] = jnp.zeros_like(o_ref)

        o_ref[...] += _dot_tn(a_ref[...], b_ref[...])

    return pl.pallas_call(
        body, name=f"wgrad_{k1}x{n}", grid=(k1 // bk, n // bn, nsteps),
        in_specs=[pl.BlockSpec((bt, bk), lambda i, j, s: (s, i)), pl.BlockSpec((bt, bn), lambda i, j, s: (s, j))],
        out_specs=pl.BlockSpec((bk, bn), lambda i, j, s: (i, j)),
        out_shape=SDS((k1, n), F32),
        compiler_params=_params("arbitrary", "arbitrary", "arbitrary"),
    )(a, b)


def _adamw(parts, w, m, v, *, name):
    rows, cols = w.shape
    br = _pick_rows(rows, cols)
    c1 = 1.0 - ADAM_B1 ** ADAM_STEP
    c2 = 1.0 - ADAM_B2 ** ADAM_STEP

    def body(p_ref, w_ref, m_ref, v_ref, g_ref, d_ref, nm_ref, nv_ref):
        g = p_ref[0].astype(F32)
        for j in range(1, N_DEV):
            g = g + p_ref[j].astype(F32)
        g_ref[...] = g
        nm = ADAM_B1 * m_ref[...] + (1.0 - ADAM_B1) * g
        nv = ADAM_B2 * v_ref[...] + (1.0 - ADAM_B2) * (g * g)
        nm_ref[...] = nm
        nv_ref[...] = nv
        d_ref[...] = -ADAM_LR * ((nm / c1) / (jnp.sqrt(nv / c2) + ADAM_EPS) + ADAM_WD * w_ref[...])

    spec = pl.BlockSpec((br, cols), lambda i: (i, 0))
    return pl.pallas_call(
        body, name=name, grid=(rows // br,),
        in_specs=[pl.BlockSpec((N_DEV, br, cols), lambda i: (0, i, 0)), spec, spec, spec],
        out_specs=[spec] * 4, out_shape=[SDS((rows, cols), F32)] * 4,
        compiler_params=_params("arbitrary"),
    )(parts, w, m, v)


def _pick_rows(rows, cols):
    target = max(8, (256 * 1024) // cols)
    best = 8
    for cand in range(8, min(rows, target) + 1, 8):
        if rows % cand == 0:
            best = cand
    return best


def _peer(k):
    x, y, c = lax.axis_index("x"), lax.axis_index("y"), lax.axis_index("c")
    px = 1 - x if k & 4 else x
    py = 1 - y if k & 2 else y
    pc = 1 - c if k & 1 else c
    return (px, py, pc), 4 * px + 2 * py + pc


def _exchange_scratch(n):
    return [pltpu.SemaphoreType.DMA((N_DEV - 1, n)), pltpu.SemaphoreType.DMA((N_DEV - 1, n)),
            pltpu.SemaphoreType.DMA((n,))]


def _exchange_copies(ins, outs, sems, scatter, landing):
    send_sems, recv_sems, local_sems = sems
    me = 4 * lax.axis_index("x") + 2 * lax.axis_index("y") + lax.axis_index("c")
    copies = [pltpu.make_async_copy(ins[a].at[me] if scatter[a] else ins[a], outs[a].at[me], local_sems.at[a])
              for a in range(len(ins))]
    for k in range(1, N_DEV):
        peer, pidx = _peer(k)
        for a in range(len(ins)):
            copies.append(pltpu.make_async_remote_copy(
                src_ref=ins[a].at[pidx] if scatter[a] else ins[a], dst_ref=outs[a].at[pidx if landing else me],
                send_sem=send_sems.at[k - 1, a], recv_sem=recv_sems.at[k - 1, a], device_id=peer, device_id_type=MESH))
    return copies


def _exchange_start(ins, outs, sems, scatter):
    for cp in _exchange_copies(ins, outs, sems, scatter, landing=False):
        cp.start()


def _exchange_wait(ins, outs, sems, scatter):
    for cp in _exchange_copies(ins, outs, sems, scatter, landing=True):
        cp.wait()


def _exchange_shapes(arrs, scatter):
    return [SDS(a.shape if sc else (N_DEV,) + a.shape, a.dtype) for a, sc in zip(arrs, scatter)]


def _exchange(arrs, scatter, *, name):
    n = len(arrs)

    def body(*refs):
        ins, outs, sems = refs[:n], refs[n:2 * n], refs[2 * n:]
        _exchange_start(ins, outs, sems, scatter)
        _exchange_wait(ins, outs, sems, scatter)

    return pl.pallas_call(
        body, name=name, in_specs=[ANY] * n, out_specs=[ANY] * n, out_shape=_exchange_shapes(arrs, scatter),
        scratch_shapes=_exchange_scratch(n),
    )(*arrs)


def _fused_exchange(body, n_in, n_out, scatter, nsteps):
    n = len(scatter)

    def wrapped(*refs):
        ins, ex_in = refs[:n_in], refs[n_in:n_in + n]
        outs, ex_out = refs[n_in + n:n_in + n + n_out], refs[n_in + n + n_out:n_in + 2 * n + n_out]
        scratch, sems = refs[n_in + 2 * n + n_out:-3], refs[-3:]

        @pl.when(pl.program_id(0) == 0)
        def _():
            _exchange_start(ex_in, ex_out, sems, scatter)

        body(*ins, *outs, *scratch)

        @pl.when(pl.program_id(0) == nsteps - 1)
        def _():
            _exchange_wait(ex_in, ex_out, sems, scatter)

    return wrapped


def _step(x, tgt, small, shards):
    t = x.shape[0]
    tm, tq, tw = _tiles(t)
    r = jnp.arange(SGU_CHUNK, dtype=jnp.int32) // SGU_BLOCK
    mask = (r[None, :] <= r[:, None]).astype(F32)
    layer_shards = lambda l: [shards[n][l] for n in BIG]

    layers = []
    saved = []
    gathered = _exchange(layer_shards(0)[:1], [False], name="gather_weights")
    for l in range(DEPTH):
        w_in = _assemble(gathered[0], BIG[0])
        w_pad = jnp.concatenate([w_in[:, :3 * D_ATTN + N_HEADS], jnp.zeros((D_MODEL, LANES - N_HEADS), BF16),
                                 w_in[:, 3 * D_ATTN + N_HEADS:]], axis=1)
        bf = jnp.pad(small["b_f"][l], (0, LANES - N_HEADS))[None, :]
        wsm = (small["w_s"][l] * mask[None]).astype(BF16)
        wsm_t = jnp.swapaxes(wsm, 1, 2)
        bsf = jnp.repeat(small["b_s"][l].T, GROUP_DIM, axis=1)
        lw = dict(w_in=w_pad, bf=bf, wsm=wsm, wsm_t=wsm_t, bsf=bsf,
                  gmix=small["mix_norm_g"][l][None, :], lng=small["sgu_ln_g"][l][None, :],
                  lnb=small["sgu_ln_b"][l][None, :], gout=small["out_norm_g"][l][None, :],
                  gffn=small["ffn_norm_g"][l][None, :])
        layers.append(lw)
        q, k, v, fl, z, sgu, *late = _fwd_a(x, lw["gmix"], w_pad, bf, lw["lng"], lw["lnb"], wsm, bsf, tm=tm,
                                                 gather=layer_shards(0)[1:] if l == 0 else ())
        w_out, w_gu, w_dn = (_assemble(g, n) for g, n in zip(late if l == 0 else gathered[1:], BIG[1:]))
        lw.update(w_out=w_out, w_gu=w_gu, w_dn=w_dn)
        o, lse = _attn_fwd(q, k, v, tq=tq)
        x1, x2, gu, *gathered = _fwd_b(x, o, sgu, lw["gout"], lw["gffn"], w_out, w_gu, w_dn, tm=tm,
                                       gather=layer_shards(l + 1) if l + 1 < DEPTH else ())
        saved.append(dict(x=x, q=q, k=k, v=v, fl=fl, z=z, sgu=sgu, o=o, lse=lse[:, :2, :], x1=x1, gu=gu))
        x = x2

    dx, loss, dgfin = _loss_bwd(x, tgt, small["final_norm_g"][None, :], tm=tm)
    grads = {n: [None] * DEPTH for n in SMALL if n != "final_norm_g"}
    parts = [None] * DEPTH
    pending = ()
    for l in reversed(range(DEPTH)):
        lw, sv = layers[l], saved[l]
        (dx1, dx2b, a, dgu, xn2, mrg, dx1b, do, delta, dsgu, dgffn, dgout, *landed) = _bwd_b(
            dx, sv["x1"], sv["gu"], sv["o"], sv["sgu"], lw["gout"], lw["gffn"], lw["w_out"], lw["w_gu"], lw["w_dn"],
            tm=tm, scatter=pending)
        if pending:
            parts[l + 1] = landed
        big_grads = {"w_down": _tn_matmul(a, dx2b, bt=tw), "w_gate_up": _tn_matmul(xn2, dgu, bt=tw),
                     "w_out": _tn_matmul(mrg, dx1b, bt=tw)}
        dqt, dk, dv, dck, dcq = _attn_bwd(sv["q"], sv["k"], sv["v"], do, sv["lse"],
                                          delta.reshape(HEAD_PAIRS, 2, t), tq=tq)
        early = [_split(big_grads[n], n).astype(BF16) for n in BIG[1:]] if l == 0 else ()
        (dx, xn, dh, dgmix, dbf, dlng, dlnb, dws, dbs, *landed) = _bwd_a(
            dx1, sv["x"], sv["z"], sv["fl"], dsgu, dqt, dk, dv, (dck + dcq).reshape(N_HEADS, t), lw["gmix"],
            lw["w_in"], lw["lng"], lw["lnb"], lw["wsm"], lw["wsm_t"], lw["bsf"], mask, tm=tm,
            scatter=early)
        g_pad = _tn_matmul(xn, dh, bt=tw)
        big_grads["w_in"] = jnp.concatenate([g_pad[:, :3 * D_ATTN + N_HEADS], g_pad[:, 3 * D_ATTN + LANES:]], axis=1)
        pending = [_split(big_grads[n], n).astype(BF16) for n in (BIG[:1] if l == 0 else BIG)]
        grads["mix_norm_g"][l] = dgmix[0]
        grads["b_f"][l] = dbf[0, :N_HEADS]
        grads["sgu_ln_g"][l] = dlng[0]
        grads["sgu_ln_b"][l] = dlnb[0]
        grads["w_s"][l] = dws
        grads["b_s"][l] = dbs[:, :N_GROUPS].T
        grads["out_norm_g"][l] = dgout[0]
        grads["ffn_norm_g"][l] = dgffn[0]
    grads = {n: jnp.stack(g) for n, g in grads.items()}
    grads["final_norm_g"] = dgfin[0]
    first, small_parts, final_parts = _exchange(
        pending + [_pack(grads, LAYER_SMALL).astype(BF16), _pack(grads, FINAL)], [True, False, False],
        name="scatter_grads")
    parts[0] = [first] + landed
    big_parts = {}
    for a, n in enumerate(BIG):
        stacked = jnp.stack([parts[l][a] for l in range(DEPTH)], axis=1)
        big_parts[n] = stacked.reshape(N_DEV, -1, stacked.shape[-1])
    return loss[0, 0], dx, big_parts, small_parts, final_parts


SMALL = ("mix_norm_g", "b_f", "sgu_ln_g", "sgu_ln_b", "w_s", "b_s", "out_norm_g", "ffn_norm_g", "final_norm_g")
LAYER_SMALL, FINAL = SMALL[:-1], SMALL[-1:]
BIG = ("w_in", "w_out", "w_gate_up", "w_down")
WEIGHTS = ("mix_norm_g", "w_in", "b_f", "sgu_ln_g", "sgu_ln_b", "w_s", "b_s", "out_norm_g", "w_out", "ffn_norm_g",
           "w_gate_up", "w_down", "final_norm_g")
SHARD_AXIS = {"w_in": 1, "w_out": 0, "w_gate_up": 1, "w_down": 0}


def _assemble(gathered, name):
    if SHARD_AXIS[name] == 0:
        return gathered.reshape(-1, gathered.shape[-1])
    return gathered.transpose(1, 0, 2).reshape(gathered.shape[1], -1)


def _split(full, name):
    rows, cols = full.shape
    if SHARD_AXIS[name] == 0:
        return full.reshape(N_DEV, rows // N_DEV, cols)
    return full.reshape(rows, N_DEV, cols // N_DEV).transpose(1, 0, 2)


def _pack(tree, names):
    flat = jnp.concatenate([tree[n].reshape(-1) for n in names])
    pad = (-flat.shape[0]) % (16 * LANES)
    return jnp.pad(flat, (0, pad)).reshape(-1, LANES)


def _unpack(packed, like, names):
    flat = packed.reshape(-1)
    out, at = {}, 0
    for n in names:
        size = like[n].size
        out[n] = flat[at:at + size].reshape(like[n].shape)
        at += size
    return out


def kernel(x, mix_norm_g, w_in, b_f, sgu_ln_g, sgu_ln_b, w_s, b_s, out_norm_g, w_out, ffn_norm_g, w_gate_up, w_down, final_norm_g, loss_target, m_mix_norm_g, m_w_in, m_b_f, m_sgu_ln_g, m_sgu_ln_b, m_w_s, m_b_s, m_out_norm_g, m_w_out, m_ffn_norm_g, m_w_gate_up, m_w_down, m_final_norm_g, v_mix_norm_g, v_w_in, v_b_f, v_sgu_ln_g, v_sgu_ln_b, v_w_s, v_b_s, v_out_norm_g, v_w_out, v_ffn_norm_g, v_w_gate_up, v_w_down, v_final_norm_g):
    w = dict(mix_norm_g=mix_norm_g, w_in=w_in, b_f=b_f, sgu_ln_g=sgu_ln_g, sgu_ln_b=sgu_ln_b, w_s=w_s, b_s=b_s,
             out_norm_g=out_norm_g, w_out=w_out, ffn_norm_g=ffn_norm_g, w_gate_up=w_gate_up, w_down=w_down,
             final_norm_g=final_norm_g)
    m = dict(mix_norm_g=m_mix_norm_g, w_in=m_w_in, b_f=m_b_f, sgu_ln_g=m_sgu_ln_g, sgu_ln_b=m_sgu_ln_b, w_s=m_w_s,
             b_s=m_b_s, out_norm_g=m_out_norm_g, w_out=m_w_out, ffn_norm_g=m_ffn_norm_g, w_gate_up=m_w_gate_up,
             w_down=m_w_down, final_norm_g=m_final_norm_g)
    v = dict(mix_norm_g=v_mix_norm_g, w_in=v_w_in, b_f=v_b_f, sgu_ln_g=v_sgu_ln_g, sgu_ln_b=v_sgu_ln_b, w_s=v_w_s,
             b_s=v_b_s, out_norm_g=v_out_norm_g, w_out=v_w_out, ffn_norm_g=v_ffn_norm_g, w_gate_up=v_w_gate_up,
             w_down=v_w_down, final_norm_g=v_final_norm_g)

    loss, dx, big_parts, small_parts, final_parts = _step(x[0], loss_target[0], {n: w[n] for n in SMALL},
                                             {n: w[n].astype(BF16) for n in BIG})
    loss = lax.psum(loss, ("x", "y", "c"))

    g_out, d_out, m_out, v_out = {}, {}, {}, {}
    for n in BIG:
        shape = w[n].shape
        two_d = lambda a: a.reshape(-1, shape[-1])
        res = _adamw(big_parts[n], two_d(w[n]), two_d(m[n]), two_d(v[n]), name=f"adamw_{n}")
        g_out[n], d_out[n], m_out[n], v_out[n] = (r.reshape(shape) for r in res)
    for names, parts, name in ((LAYER_SMALL, small_parts, "adamw_small"), (FINAL, final_parts, "adamw_final")):
        res = _adamw(parts, _pack(w, names), _pack(m, names), _pack(v, names), name=name)
        for dst, packed in zip((g_out, d_out, m_out, v_out), res):
            dst.update(_unpack(packed, w, names))

    return (loss, dx[None], *[g_out[n] for n in WEIGHTS], *[d_out[n] for n in WEIGHTS],
            *[m_out[n] for n in WEIGHTS], *[v_out[n] for n in WEIGHTS])
```

```python
import functools
import math

import jax
import jax.numpy as jnp
from jax import lax
from jax.experimental import pallas as pl
from jax.experimental.pallas import tpu as pltpu

F32, BF16 = jnp.float32, jnp.bfloat16
HIGHEST = lax.Precision.HIGHEST
MESH = pl.DeviceIdType.MESH
ANY = pl.BlockSpec(memory_space=pl.ANY)
SDS = jax.ShapeDtypeStruct

N_DEV = 8
DEPTH = 4
D_MODEL = 1024
D_ATTN = 512
D_SGU = 512
N_HEADS = 8
HEAD_DIM = 64
HEAD_PAIRS = N_HEADS // 2
SGU_CHUNK = 128
SGU_BLOCK = 64
N_GROUPS = 8
GROUP_DIM = 64
D_FF = 2816
FF_CHUNK = 1408
N_FF_CHUNKS = D_FF // FF_CHUNK
D_IN = 3 * D_ATTN + N_HEADS + 2 * D_SGU
LANES = 128
D_IN_PADDED = 3 * D_ATTN + LANES + 2 * D_SGU
EPS = 1e-6
QK_SCALE = HEAD_DIM ** -0.5
INV_SQRT2 = 1.0 / math.sqrt(2.0)
INV_SQRT_2PI = 1.0 / math.sqrt(2.0 * math.pi)
LOG2E = 1.0 / math.log(2.0)
LN2 = math.log(2.0)
ROW_CHUNK = 32
KX_ROWS = LANES + 16

ADAM_LR = 0.001
ADAM_B1 = 0.9
ADAM_B2 = 0.999
ADAM_EPS = 1e-08
ADAM_WD = 0.01
ADAM_STEP = 10

VMEM_LIMIT_BYTES = 56 * 1024 * 1024


def _params(*sem):
    return pltpu.CompilerParams(dimension_semantics=sem or None, vmem_limit_bytes=VMEM_LIMIT_BYTES)


def _dot(a, b):
    return jnp.dot(a, b, preferred_element_type=F32)


def _dot_nt(a, b):
    return lax.dot_general(a, b, (((1,), (1,)), ((), ())), preferred_element_type=F32)


def _dot_tn(a, b):
    return lax.dot_general(a, b, (((0,), (0,)), ((), ())), preferred_element_type=F32)


def _dot_exact(a, b):
    return jnp.dot(a, b, precision=HIGHEST, preferred_element_type=F32)


def _mean(v):
    return jnp.mean(v, axis=-1, keepdims=True)


def _sigmoid(v):
    return 1.0 / (1.0 + jnp.exp(-v))


def _row_spec(tm, n):
    return pl.BlockSpec((tm, n), lambda i: (i, 0))


def _rev_spec(tm, n, nt):
    return pl.BlockSpec((tm, n), lambda i: (nt - 1 - i, 0))


def _const_spec(shape):
    return pl.BlockSpec(shape, lambda i: (0,) * len(shape))


def _tiles(t):
    return min(256, t), min(512, t), min(2048, t)


def _group_indicator():
    r = lax.broadcasted_iota(jnp.int32, (D_ATTN, LANES), 0)
    c = lax.broadcasted_iota(jnp.int32, (D_ATTN, LANES), 1)
    return ((r >> 6) == c).astype(F32)


def _sgu_mix(w_ref, zc, lane_grp):
    out = jnp.zeros((SGU_CHUNK, D_SGU), F32)
    for g in range(N_GROUPS):
        out = out + jnp.where(lane_grp == g, _dot(w_ref[g], zc), 0.0)
    return out


def _fwd_a(x, gmix, w_in, bf, lng, lnb, wsm, bsf, *, tm, gather=()):
    t = x.shape[0]
    nt = t // tm
    nch = tm // SGU_CHUNK

    def body(x_ref, gmix_ref, w_ref, bf_ref, lng_ref, lnb_ref, wsm_ref, bsf_ref,
             q_ref, k_ref, v_ref, fl_ref, z_ref, sgu_ref, carry_ref):
        @pl.when(pl.program_id(0) == 0)
        def _():
            carry_ref[...] = jnp.zeros_like(carry_ref)

        xt = x_ref[...]
        r = lax.rsqrt(_mean(xt * xt) + EPS)
        xn = ((xt * r) * gmix_ref[...]).astype(BF16)
        proj = _dot(xn, w_ref[...])
        qkv = proj[:, :3 * D_ATTN]

        fl = proj[:, 3 * D_ATTN:3 * D_ATTN + LANES] + bf_ref[...]
        fl_ref[...] = fl
        logf = jnp.minimum(fl, 0.0) - jnp.log1p(jnp.exp(-jnp.abs(fl)))
        row = lax.broadcasted_iota(jnp.int32, (tm, tm), 0)
        col = lax.broadcasted_iota(jnp.int32, (tm, tm), 1)
        c = _dot_exact((col <= row).astype(F32), logf) + carry_ref[...]
        carry_ref[...] = c[tm - 1:tm, :]

        c2 = c * LOG2E
        lane = lax.broadcasted_iota(jnp.int32, (tm, LANES), 1)
        for h in range(N_HEADS):
            pair, hh = divmod(h, 2)
            base = _aug_lane(hh)
            in_head = (lane >= hh * HEAD_DIM) & (lane < (hh + 1) * HEAD_DIM)
            col_h = jnp.sum(jnp.where(lane == h, c2, 0.0), axis=1, keepdims=True)
            hi = col_h.astype(BF16).astype(F32)
            mid = (col_h - hi).astype(BF16).astype(F32)
            lo = (col_h - hi) - mid
            split = jnp.where(lane == base, hi, jnp.where(lane == base + 1, mid, jnp.where(lane == base + 2, lo, 0.0)))
            split_k = jnp.where(lane == base + 3, hi, jnp.where(lane == base + 4, mid,
                                                                 jnp.where(lane == base + 5, lo, 0.0)))
            ones_q = ((lane >= base + 3) & (lane < base + 6)).astype(F32)
            ones_k = ((lane >= base) & (lane < base + 3)).astype(F32)
            blk = slice(pair * LANES, (pair + 1) * LANES)
            q_ref[h] = jnp.where(in_head, qkv[:, blk] * (QK_SCALE * LOG2E), split + ones_q).astype(BF16)
            k_ref[h] = jnp.where(in_head, qkv[:, D_ATTN:2 * D_ATTN][:, blk], ones_k - split_k).astype(BF16)
            v_ref[h] = jnp.where(in_head, qkv[:, 2 * D_ATTN:][:, blk], (lane == base).astype(F32)).astype(BF16)

        z = proj[:, 3 * D_ATTN + LANES:]
        z_ref[...] = z
        zg = 0.5 * z * (1.0 + lax.erf(z * INV_SQRT2))
        zu = zg[:, :D_SGU]
        zv = zg[:, D_SGU:]
        xc = zv - _mean(zv)
        zvn = ((xc * lax.rsqrt(_mean(xc * xc) + EPS)) * lng_ref[...] + lnb_ref[...]).astype(BF16)
        lane_grp = lax.broadcasted_iota(jnp.int32, (SGU_CHUNK, D_SGU), 1) >> 6
        for ch in range(nch):
            rows = slice(ch * SGU_CHUNK, (ch + 1) * SGU_CHUNK)
            mixed = _sgu_mix(wsm_ref, zvn[rows, :], lane_grp) + bsf_ref[...]
            sgu_ref[rows, :] = zu[rows, :] * mixed

    head_spec = pl.BlockSpec((N_HEADS, tm, LANES), lambda i: (0, i, 0))
    return _row_tile_call(
        body, "fwd_a", nt, (x, gmix, w_in, bf, lng, lnb, wsm, bsf), exchange=gather, scatter=False,
        in_specs=[_row_spec(tm, D_MODEL), _const_spec((1, D_MODEL)), _const_spec((D_MODEL, D_IN_PADDED)),
                  _const_spec((1, LANES)), _const_spec((1, D_SGU)), _const_spec((1, D_SGU)),
                  _const_spec((N_GROUPS, SGU_CHUNK, SGU_CHUNK)), _const_spec((SGU_CHUNK, D_SGU))],
        out_specs=[head_spec, head_spec, head_spec, _row_spec(tm, LANES), _row_spec(tm, 2 * D_SGU),
                   _row_spec(tm, D_SGU)],
        out_shape=[SDS((N_HEADS, t, LANES), BF16)] * 3 + [SDS((t, LANES), F32), SDS((t, 2 * D_SGU), F32),
                                                          SDS((t, D_SGU), F32)],
        scratch_shapes=[pltpu.VMEM((1, LANES), F32)])


def _aug_lane(hh):
    return (1 - hh) * HEAD_DIM


def _attn_fwd(qa, ka, va, *, tq):
    t = qa.shape[1]
    nq = t // tq
    nrc = tq // ROW_CHUNK

    def body(q_ref, k_hbm, v_hbm, o_ref, lse_ref, k_vm, vt_vm, v_tmp, qt_ref, s_ref, p_ref, m_ref, a_ref, acc_ref):
        p = pl.program_id(0)
        i = pl.program_id(1)

        @pl.when(i == 0)
        def _():
            pltpu.sync_copy(k_hbm.at[pl.ds(2 * p, 2)], k_vm)

            def transpose_block(b, carry):
                start = pl.multiple_of(b * tq, tq)
                for h in range(2):
                    pltpu.sync_copy(v_hbm.at[2 * p + h, pl.ds(start, tq), :], v_tmp.at[h])
                    vt_vm[h, :, pl.ds(start, tq)] = v_tmp[h].astype(F32).T.astype(BF16)
                return carry

            lax.fori_loop(0, nq, transpose_block, 0)

        for h in range(2):
            qt_ref[h] = q_ref[h].astype(F32).T.astype(BF16)
        m_ref[...] = jnp.full(m_ref.shape, -jnp.inf, F32)
        acc_ref[...] = jnp.zeros_like(acc_ref)
        rowk = lax.broadcasted_iota(jnp.int32, (ROW_CHUNK, tq), 0)
        colq = lax.broadcasted_iota(jnp.int32, (ROW_CHUNK, tq), 1)

        def scores(j, slot):
            start = pl.multiple_of(j * tq, tq)
            for h in range(2):
                s_ref[slot, h] = _dot(k_vm[h, pl.ds(start, tq), :], qt_ref[h])

        def chunk(slot, h, r, masked):
            sc = s_ref[slot, h, r * ROW_CHUNK:(r + 1) * ROW_CHUNK, :]
            if masked:
                sc = jnp.where(rowk + r * ROW_CHUNK <= colq, sc, -jnp.inf)
            return sc

        def softmax(slot, masked):
            for h in range(2):
                m_old = m_ref[h]
                top = jnp.full((8, tq), -jnp.inf, F32)
                for r in range(nrc):
                    sc = chunk(slot, h, r, masked)
                    for g in range(ROW_CHUNK // 8):
                        top = jnp.maximum(top, sc[g * 8:(g + 1) * 8, :])
                m_new = jnp.maximum(m_old, jnp.max(top, axis=0, keepdims=True))
                for r in range(nrc):
                    rows = slice(r * ROW_CHUNK, (r + 1) * ROW_CHUNK)
                    p_ref[slot, h, rows, :] = jnp.exp2(chunk(slot, h, r, masked) - m_new).astype(BF16)
                a_ref[slot, h] = jnp.exp2(m_old - m_new)
                m_ref[h] = m_new

        def accumulate(j, slot):
            start = pl.multiple_of(j * tq, tq)
            for h in range(2):
                acc_ref[h] = acc_ref[h] * a_ref[slot, h] + _dot(vt_vm[h, :, pl.ds(start, tq)], p_ref[slot, h])

        scores(0, 0)

        @pl.when(i > 0)
        def _():
            scores(1, 1)
            softmax(0, False)

        def stage(j, slot):
            scores(j + 1, 1 - slot)
            softmax(slot, False)
            accumulate(j - 1, 1 - slot)

        def pair_body(n, carry):
            j = 1 + 2 * n
            stage(j, 1)
            stage(j + 1, 0)
            return carry

        rest = i - 1
        lax.fori_loop(0, rest // 2, pair_body, 0)

        @pl.when((rest > 0) & (lax.rem(rest, 2) == 1))
        def _():
            stage(i - 1, 1)

        @pl.when(lax.rem(i, 2) == 0)
        def _():
            softmax(0, True)

            @pl.when(i > 0)
            def _():
                accumulate(i - 1, 1)

            accumulate(i, 0)

        @pl.when(lax.rem(i, 2) == 1)
        def _():
            softmax(1, True)
            accumulate(i - 1, 0)
            accumulate(i, 1)

        row = lax.broadcasted_iota(jnp.int32, (LANES, tq), 0)
        l_h = [acc_ref[h, _aug_lane(h):_aug_lane(h) + 1, :] for h in range(2)]
        o_ref[...] = jnp.where(row < HEAD_DIM, acc_ref[0] / l_h[0], acc_ref[1] / l_h[1]).T
        lse_ref[0] = jnp.zeros((8, tq), F32)
        for h in range(2):
            lse_ref[0, h:h + 1, :] = m_ref[h] + jnp.log2(l_h[h])

    return pl.pallas_call(
        body, name="attn_fwd", grid=(HEAD_PAIRS, nq),
        in_specs=[pl.BlockSpec((2, tq, LANES), lambda p, i: (p, i, 0)), ANY, ANY],
        out_specs=[pl.BlockSpec((tq, LANES), lambda p, i: (i, p)),
                   pl.BlockSpec((1, 8, tq), lambda p, i: (p, 0, i))],
        out_shape=[SDS((t, D_ATTN), F32), SDS((HEAD_PAIRS, 8, t), F32)],
        scratch_shapes=[pltpu.VMEM((2, t, LANES), BF16), pltpu.VMEM((2, LANES, t), BF16),
                        pltpu.VMEM((2, tq, LANES), BF16), pltpu.VMEM((2, LANES, tq), BF16),
                        pltpu.VMEM((2, 2, tq, tq), F32), pltpu.VMEM((2, 2, tq, tq), BF16),
                        pltpu.VMEM((2, 1, tq), F32), pltpu.VMEM((2, 2, 1, tq), F32), pltpu.VMEM((2, LANES, tq), F32)],
        compiler_params=_params("arbitrary", "arbitrary"),
    )(qa, ka, va)


def _attn_bwd(qa, ka, va, do, lse_row, delta_row, *, tq):
    t = qa.shape[1]
    nq = t // tq
    nrc = tq // ROW_CHUNK

    def body(q_hbm, do_ref, k_ref, v_ref, lse_ref, dl_ref, dqt_ref, dk_ref, dv_ref, dck_ref, dcq_ref,
             q_vm, st_ref, dp_ref, pt_ref, ds_ref, dka_ref, dva_ref, vh_ref, kx_ref):
        p = pl.program_id(0)
        j = pl.program_id(1)

        @pl.when(j == 0)
        def _():
            pltpu.sync_copy(q_hbm.at[pl.ds(2 * p, 2)], q_vm)
            dqt_ref[...] = jnp.zeros_like(dqt_ref)
            dcq_ref[...] = jnp.zeros_like(dcq_ref)

        dka_ref[...] = jnp.zeros_like(dka_ref)
        dva_ref[...] = jnp.zeros_like(dva_ref)
        lane = lax.broadcasted_iota(jnp.int32, (tq, LANES), 1)
        in_head = (lane < HEAD_DIM, lane >= HEAD_DIM)
        for h in range(2):
            zero = jnp.zeros((tq, LANES), BF16)
            vh_ref[h] = jnp.where(in_head[h], v_ref[h], zero)
            kx_ref[h, :LANES, :] = jnp.where(in_head[h], k_ref[h], zero).astype(F32).T.astype(BF16)
            kx_ref[h, LANES:, :] = jnp.ones((KX_ROWS - LANES, tq), BF16)
        rowk = lax.broadcasted_iota(jnp.int32, (ROW_CHUNK, tq), 0)
        colq = lax.broadcasted_iota(jnp.int32, (ROW_CHUNK, tq), 1)

        def step(i, masked, slot):
            start = pl.multiple_of(i * tq, tq)
            cols = pl.ds(start, tq)
            do2 = do_ref[pl.ds(start, tq), :]
            q_h = [q_vm[h, pl.ds(start, tq), :] for h in range(2)]
            for h in range(2):
                st_ref[slot, h] = _dot_nt(k_ref[h], q_h[h])
                dp_ref[slot, h] = _dot_nt(vh_ref[h], do2)
            for h in range(2):
                lse = lse_ref[0, h:h + 1, cols]
                delta = dl_ref[0, h:h + 1, cols]
                for r in range(nrc):
                    rows = slice(r * ROW_CHUNK, (r + 1) * ROW_CHUNK)
                    st = st_ref[slot, h, rows, :]
                    if masked:
                        st = jnp.where(rowk + r * ROW_CHUNK <= colq, st, -jnp.inf)
                    pt = jnp.exp2(st - lse)
                    pt_ref[slot, h, rows, :] = pt.astype(BF16)
                    ds_ref[slot, h, rows, :] = (pt * (dp_ref[slot, h, rows, :] - delta)).astype(BF16)
            dq_t = jnp.zeros((LANES, tq), F32)
            for h in range(2):
                dva_ref[h] += _dot(pt_ref[slot, h], do2)
                dka_ref[h] += _dot(ds_ref[slot, h], q_h[h])
                ext = _dot(kx_ref[h], ds_ref[slot, h])
                dq_t = dq_t + ext[:LANES, :]
                dcq_ref[0, h:h + 1, cols] += ext[LANES:LANES + 1, :]
            dqt_ref[0, :, cols] += dq_t

        def quad_body(n, carry):
            i = j + 1 + 4 * n
            step(i, False, 0)
            step(i + 1, False, 1)
            step(i + 2, False, 0)
            step(i + 3, False, 1)
            return carry

        step(j, True, 0)
        after = nq - 1 - j
        quads = after // 4
        lax.fori_loop(0, quads, quad_body, 0)
        left = after - 4 * quads

        @pl.when(left >= 2)
        def _():
            step(j + 1 + 4 * quads, False, 0)
            step(j + 2 + 4 * quads, False, 1)

        @pl.when(lax.rem(left, 2) == 1)
        def _():
            step(nq - 1, False, 0)

        dk_ref[...] = (jnp.where(in_head[0], dka_ref[0], dka_ref[1]) * LN2).astype(BF16)
        dv_ref[...] = jnp.where(in_head[0], dva_ref[0], dva_ref[1]).astype(BF16)
        own = pl.ds(pl.multiple_of(j * tq, tq), tq)
        for h in range(2):
            at = _aug_lane(h) + 3
            dck_ref[0, h:h + 1, own] = -dka_ref[h].T[at:at + 1, :]

    rows = pl.BlockSpec((1, 2, t), lambda p, j: (p, 0, 0))
    tiles = pl.BlockSpec((1, LANES, t), lambda p, j: (p, 0, 0))
    return pl.pallas_call(
        body, name="attn_bwd", grid=(HEAD_PAIRS, nq),
        in_specs=[ANY, pl.BlockSpec((t, LANES), lambda p, j: (0, p)),
                  pl.BlockSpec((2, tq, LANES), lambda p, j: (p, j, 0)),
                  pl.BlockSpec((2, tq, LANES), lambda p, j: (p, j, 0)), rows, rows],
        out_specs=[tiles,
                   pl.BlockSpec((tq, LANES), lambda p, j: (j, p)),
                   pl.BlockSpec((tq, LANES), lambda p, j: (j, p)), rows, rows],
        out_shape=[SDS((HEAD_PAIRS, LANES, t), F32), SDS((t, D_ATTN), BF16), SDS((t, D_ATTN), BF16),
                   SDS((HEAD_PAIRS, 2, t), F32), SDS((HEAD_PAIRS, 2, t), F32)],
        scratch_shapes=[pltpu.VMEM((2, t, LANES), BF16), pltpu.VMEM((2, 2, tq, tq), F32),
                        pltpu.VMEM((2, 2, tq, tq), F32), pltpu.VMEM((2, 2, tq, tq), BF16),
                        pltpu.VMEM((2, 2, tq, tq), BF16),
                        pltpu.VMEM((2, tq, LANES), F32), pltpu.VMEM((2, tq, LANES), F32),
                        pltpu.VMEM((2, tq, LANES), BF16), pltpu.VMEM((2, KX_ROWS, tq), BF16)],
        compiler_params=_params("arbitrary", "arbitrary"),
    )(qa, do, ka, va, lse_row, delta_row)

def _load_weights_once(pairs):
    @pl.when(pl.program_id(0) == 0)
    def _():
        for src, dst in pairs:
            pltpu.sync_copy(src, dst)


def _row_tile_call(body, name, nt, operands, *, in_specs, out_specs, out_shape, scratch_shapes, exchange, scatter):
    if not exchange:
        return pl.pallas_call(body, name=name, grid=(nt,), in_specs=in_specs, out_specs=out_specs,
                              out_shape=out_shape, scratch_shapes=scratch_shapes,
                              compiler_params=_params("arbitrary"))(*operands)
    flags = [scatter] * len(exchange)
    n = len(exchange)
    return pl.pallas_call(
        _fused_exchange(body, len(operands), len(out_shape), flags, nt),
        name=name + ("_scatter" if scatter else "_gather"), grid=(nt,),
        in_specs=list(in_specs) + [ANY] * n, out_specs=list(out_specs) + [ANY] * n,
        out_shape=list(out_shape) + _exchange_shapes(exchange, flags),
        scratch_shapes=list(scratch_shapes) + _exchange_scratch(n),
        compiler_params=_params("arbitrary"))(*operands, *exchange)


def _fwd_b(x, o, sgu, gout, gffn, w_out, w_gu, w_dn, *, tm, gather=()):
    t = x.shape[0]
    nt = t // tm

    def body(x_ref, o_ref, s_ref, gout_ref, gffn_ref, wout_hbm, wgu_hbm, wdn_hbm,
             x1_ref, x2_ref, gu_ref, wout, wgu, wdn):
        _load_weights_once(((wout_hbm, wout), (wgu_hbm, wgu), (wdn_hbm, wdn)))
        ov = o_ref[...]
        sv = s_ref[...]
        mo = ((ov * lax.rsqrt(_mean(ov * ov) + EPS)) * gout_ref[:, :D_ATTN]).astype(BF16)
        ms = ((sv * lax.rsqrt(_mean(sv * sv) + EPS)) * gout_ref[:, D_ATTN:]).astype(BF16)
        x1 = x_ref[...] + (_dot(mo, wout[:D_ATTN, :]) + _dot(ms, wout[D_ATTN:, :]))
        x1_ref[...] = x1
        xn2 = ((x1 * lax.rsqrt(_mean(x1 * x1) + EPS)) * gffn_ref[...]).astype(BF16)
        y = jnp.zeros((tm, D_MODEL), F32)
        for n in range(N_FF_CHUNKS):
            lo, hi = n * FF_CHUNK, (n + 1) * FF_CHUNK
            gate = _dot(xn2, wgu[:, lo:hi])
            up = _dot(xn2, wgu[:, D_FF + lo:D_FF + hi])
            gu_ref[:, lo:hi] = gate
            gu_ref[:, D_FF + lo:D_FF + hi] = up
            a = ((gate * _sigmoid(gate)) * up).astype(BF16)
            y = y + _dot(a, wdn[lo:hi, :])
        x2_ref[...] = x1 + y

    return _row_tile_call(
        body, "fwd_b", nt, (x, o, sgu, gout, gffn, w_out, w_gu, w_dn),
        in_specs=[_row_spec(tm, D_MODEL), _row_spec(tm, D_ATTN), _row_spec(tm, D_SGU),
                  _const_spec((1, D_MODEL)), _const_spec((1, D_MODEL)), ANY, ANY, ANY],
        out_specs=[_row_spec(tm, D_MODEL), _row_spec(tm, D_MODEL), _row_spec(tm, 2 * D_FF)],
        out_shape=[SDS((t, D_MODEL), F32), SDS((t, D_MODEL), F32), SDS((t, 2 * D_FF), F32)],
        scratch_shapes=[pltpu.VMEM((D_MODEL, D_MODEL), BF16), pltpu.VMEM((D_MODEL, 2 * D_FF), BF16),
                        pltpu.VMEM((D_FF, D_MODEL), BF16)],
        exchange=gather, scatter=False)


def _loss_bwd(x, tgt, gfin, *, tm):
    t = x.shape[0]
    nt = t // tm

    def body(x_ref, t_ref, g_ref, dx_ref, loss_ref, dg_ref):
        @pl.when(pl.program_id(0) == 0)
        def _():
            loss_ref[...] = jnp.zeros_like(loss_ref)
            dg_ref[...] = jnp.zeros_like(dg_ref)

        xt = x_ref[...]
        g = g_ref[...]
        r = lax.rsqrt(_mean(xt * xt) + EPS)
        xh = xt * r
        err = xh * g - t_ref[...]
        loss_ref[...] += 0.5 * jnp.sum(_mean(err * err), axis=0, keepdims=True)
        dy = err * (1.0 / D_MODEL)
        dg_ref[...] += jnp.sum(dy * xh, axis=0, keepdims=True)
        dyg = dy * g
        dx_ref[...] = r * (dyg - xh * _mean(dyg * xh))

    return pl.pallas_call(
        body, name="loss_bwd", grid=(nt,),
        in_specs=[_row_spec(tm, D_MODEL), _row_spec(tm, D_MODEL), _const_spec((1, D_MODEL))],
        out_specs=[_row_spec(tm, D_MODEL), _const_spec((1, 1)), _const_spec((1, D_MODEL))],
        out_shape=[SDS((t, D_MODEL), F32), SDS((1, 1), F32), SDS((1, D_MODEL), F32)],
        compiler_params=_params("arbitrary"),
    )(x, tgt, gfin)


def _bwd_b(dx2, x1, gu, o, sgu, gout, gffn, w_out, w_gu, w_dn, *, tm, scatter=()):
    t = dx2.shape[0]
    nt = t // tm

    def body(dx2_ref, x1_ref, gu_ref, o_ref, s_ref, gout_ref, gffn_ref, wout_hbm, wgu_hbm, wdn_hbm,
             dx1_ref, dx2b_ref, a_ref, dgu_ref, xn2_ref, mrg_ref, dx1b_ref, do_ref, dl_ref, dsgu_ref,
             dgffn_ref, dgout_ref, wout, wgu, wdn):
        _load_weights_once(((wout_hbm, wout), (wgu_hbm, wgu), (wdn_hbm, wdn)))

        @pl.when(pl.program_id(0) == 0)
        def _():
            dgffn_ref[...] = jnp.zeros_like(dgffn_ref)
            dgout_ref[...] = jnp.zeros_like(dgout_ref)

        dx2 = dx2_ref[...]
        dx2b = dx2.astype(BF16)
        dx2b_ref[...] = dx2b
        dxn2 = jnp.zeros((tm, D_MODEL), F32)
        for n in range(N_FF_CHUNKS):
            lo, hi = n * FF_CHUNK, (n + 1) * FF_CHUNK
            gate = gu_ref[:, lo:hi]
            up = gu_ref[:, D_FF + lo:D_FF + hi]
            sg = _sigmoid(gate)
            si = gate * sg
            a_ref[:, lo:hi] = (si * up).astype(BF16)
            d_a = _dot_nt(dx2b, wdn[lo:hi, :])
            dgb = ((d_a * up) * (sg * (1.0 + gate * (1.0 - sg)))).astype(BF16)
            dub = (d_a * si).astype(BF16)
            dgu_ref[:, lo:hi] = dgb
            dgu_ref[:, D_FF + lo:D_FF + hi] = dub
            dxn2 = dxn2 + (_dot_nt(dgb, wgu[:, lo:hi]) + _dot_nt(dub, wgu[:, D_FF + lo:D_FF + hi]))

        x1 = x1_ref[...]
        gffn = gffn_ref[...]
        r1 = lax.rsqrt(_mean(x1 * x1) + EPS)
        xh1 = x1 * r1
        xn2_ref[...] = (xh1 * gffn).astype(BF16)
        dgffn_ref[...] += jnp.sum(dxn2 * xh1, axis=0, keepdims=True)
        dyg = dxn2 * gffn
        dx1 = dx2 + r1 * (dyg - xh1 * _mean(dyg * xh1))
        dx1_ref[...] = dx1
        dx1b = dx1.astype(BF16)
        dx1b_ref[...] = dx1b

        ov = o_ref[...]
        sv = s_ref[...]
        go = gout_ref[:, :D_ATTN]
        gs = gout_ref[:, D_ATTN:]
        ro = lax.rsqrt(_mean(ov * ov) + EPS)
        rs = lax.rsqrt(_mean(sv * sv) + EPS)
        oh = ov * ro
        sh = sv * rs
        mrg_ref[:, :D_ATTN] = (oh * go).astype(BF16)
        mrg_ref[:, D_ATTN:] = (sh * gs).astype(BF16)
        dmo = _dot_nt(dx1b, wout[:D_ATTN, :])
        dms = _dot_nt(dx1b, wout[D_ATTN:, :])
        dgout_ref[:, :D_ATTN] += jnp.sum(dmo * oh, axis=0, keepdims=True)
        dgout_ref[:, D_ATTN:] += jnp.sum(dms * sh, axis=0, keepdims=True)
        dmog = dmo * go
        d_o = ro * (dmog - oh * _mean(dmog * oh))
        do_ref[...] = d_o.astype(BF16)
        dl_ref[...] = _dot_exact(d_o * ov, _group_indicator()).T[:N_HEADS, :]
        dmsg = dms * gs
        dsgu_ref[...] = rs * (dmsg - sh * _mean(dmsg * sh))

    return _row_tile_call(
        body, "bwd_b", nt, (dx2, x1, gu, o, sgu, gout, gffn, w_out, w_gu, w_dn),
        in_specs=[_row_spec(tm, D_MODEL), _row_spec(tm, D_MODEL), _row_spec(tm, 2 * D_FF), _row_spec(tm, D_ATTN),
                  _row_spec(tm, D_SGU), _const_spec((1, D_MODEL)), _const_spec((1, D_MODEL)), ANY, ANY, ANY],
        out_specs=[_row_spec(tm, D_MODEL), _row_spec(tm, D_MODEL), _row_spec(tm, D_FF), _row_spec(tm, 2 * D_FF),
                   _row_spec(tm, D_MODEL), _row_spec(tm, D_MODEL), _row_spec(tm, D_MODEL), _row_spec(tm, D_ATTN),
                   pl.BlockSpec((N_HEADS, tm), lambda i: (0, i)), _row_spec(tm, D_SGU),
                   _const_spec((1, D_MODEL)), _const_spec((1, D_MODEL))],
        out_shape=[SDS((t, D_MODEL), F32), SDS((t, D_MODEL), BF16), SDS((t, D_FF), BF16), SDS((t, 2 * D_FF), BF16),
                   SDS((t, D_MODEL), BF16), SDS((t, D_MODEL), BF16), SDS((t, D_MODEL), BF16),
                   SDS((t, D_ATTN), BF16), SDS((N_HEADS, t), F32), SDS((t, D_SGU), F32),
                   SDS((1, D_MODEL), F32), SDS((1, D_MODEL), F32)],
        scratch_shapes=[pltpu.VMEM((D_MODEL, D_MODEL), BF16), pltpu.VMEM((D_MODEL, 2 * D_FF), BF16),
                        pltpu.VMEM((D_FF, D_MODEL), BF16)],
        exchange=scatter, scatter=True)


def _bwd_a(dx1, x, z, fl, dsgu, dq, dk, dv, dc, gmix, w_in, lng, lnb, wsm, wsm_t, bsf, mask, *, tm, scatter=()):
    t = x.shape[0]
    nt = t // tm
    nch = tm // SGU_CHUNK

    def body(dx1_ref, x_ref, z_ref, fl_ref, dsgu_ref, dq_ref, dk_ref, dv_ref, dc_ref, gmix_ref, w_ref,
             lng_ref, lnb_ref, wsm_ref, wsmt_ref, bsf_ref, mask_ref,
             dx_ref, xn_ref, dh_ref, dgmix_ref, dbf_ref, dlng_ref, dlnb_ref, dws_ref, dbs_ref,
             carry_ref, dzvn_ref, dzu_ref, dbacc_ref):
        step = pl.program_id(0)

        @pl.when(step == 0)
        def _():
            carry_ref[...] = jnp.zeros_like(carry_ref)
            dbacc_ref[...] = jnp.zeros_like(dbacc_ref)
            for ref in (dgmix_ref, dbf_ref, dlng_ref, dlnb_ref, dws_ref):
                ref[...] = jnp.zeros_like(ref)

        z = z_ref[...]
        erf = lax.erf(z * INV_SQRT2)
        cdf = 0.5 * (1.0 + erf)
        zg = z * cdf
        zu = zg[:, :D_SGU]
        zv = zg[:, D_SGU:]
        xc = zv - _mean(zv)
        rln = lax.rsqrt(_mean(xc * xc) + EPS)
        zh = xc * rln
        lng = lng_ref[...]
        zvn = (zh * lng + lnb_ref[...]).astype(BF16)
        dsgu = dsgu_ref[...]
        lane_grp = lax.broadcasted_iota(jnp.int32, (SGU_CHUNK, D_SGU), 1) >> 6
        for ch in range(nch):
            rows = slice(ch * SGU_CHUNK, (ch + 1) * SGU_CHUNK)
            zc = zvn[rows, :]
            ds_c = dsgu[rows, :]
            mixed = _sgu_mix(wsm_ref, zc, lane_grp) + bsf_ref[...]
            dzu_ref[rows, :] = ds_c * mixed
            dmix = ds_c * zu[rows, :]
            dbacc_ref[...] += dmix
            dmb = dmix.astype(BF16)
            dzvn_ref[rows, :] = _sgu_mix(wsmt_ref, dmb, lane_grp)
            for g in range(N_GROUPS):
                dws_ref[g] += _dot_nt(jnp.where(lane_grp == g, dmb, jnp.zeros_like(dmb)), zc)
        dzvn = dzvn_ref[...]
        dlng_ref[...] += jnp.sum(dzvn * zh, axis=0, keepdims=True)
        dlnb_ref[...] += jnp.sum(dzvn, axis=0, keepdims=True)
        dzh = dzvn * lng
        dzv = rln * ((dzh - _mean(dzh)) - zh * _mean(dzh * zh))
        pdf = jnp.exp(-0.5 * (z * z)) * INV_SQRT_2PI
        dgelu = cdf + z * pdf
        z_at = 3 * D_ATTN + LANES
        dh_ref[:, z_at:z_at + D_SGU] = (dzu_ref[...] * dgelu[:, :D_SGU]).astype(BF16)
        dh_ref[:, z_at + D_SGU:] = (dzv * dgelu[:, D_SGU:]).astype(BF16)

        dc = jnp.concatenate([dc_ref[...], jnp.zeros((LANES - N_HEADS, tm), F32)], axis=0).T
        row = lax.broadcasted_iota(jnp.int32, (tm, tm), 0)
        col = lax.broadcasted_iota(jnp.int32, (tm, tm), 1)
        dlogf = _dot_exact((col >= row).astype(F32), dc) + carry_ref[...]
        carry_ref[...] = dlogf[0:1, :]
        dfl = dlogf * _sigmoid(-fl_ref[...])
        dbf_ref[...] += jnp.sum(dfl, axis=0, keepdims=True)
        dh_ref[:, 3 * D_ATTN:z_at] = dfl.astype(BF16)

        for pair in range(HEAD_PAIRS):
            dh_ref[:, pair * LANES:(pair + 1) * LANES] = (dq_ref[pair].T * QK_SCALE).astype(BF16)
        dh_ref[:, D_ATTN:2 * D_ATTN] = dk_ref[...]
        dh_ref[:, 2 * D_ATTN:3 * D_ATTN] = dv_ref[...]
        dxn = _dot_nt(dh_ref[...], w_ref[...])

        xt = x_ref[...]
        gmix = gmix_ref[...]
        r = lax.rsqrt(_mean(xt * xt) + EPS)
        xh = xt * r
        xn_ref[...] = (xh * gmix).astype(BF16)
        dgmix_ref[...] += jnp.sum(dxn * xh, axis=0, keepdims=True)
        dyg = dxn * gmix
        dx_ref[...] = dx1_ref[...] + r * (dyg - xh * _mean(dyg * xh))

        @pl.when(step == nt - 1)
        def _():
            for g in range(N_GROUPS):
                dws_ref[g] = dws_ref[g] * mask_ref[...]
            dbs_ref[...] = _dot_exact(dbacc_ref[...], _group_indicator())

    rev = functools.partial(_rev_spec, nt=nt)
    return _row_tile_call(
        body, "bwd_a", nt, (dx1, x, z, fl, dsgu, dq, dk, dv, dc, gmix, w_in, lng, lnb, wsm, wsm_t, bsf, mask),
        exchange=scatter, scatter=True,
        in_specs=[rev(tm, D_MODEL), rev(tm, D_MODEL), rev(tm, 2 * D_SGU), rev(tm, LANES), rev(tm, D_SGU),
                  pl.BlockSpec((HEAD_PAIRS, LANES, tm), lambda i: (0, 0, nt - 1 - i)), rev(tm, D_ATTN), rev(tm, D_ATTN),
                  pl.BlockSpec((N_HEADS, tm), lambda i: (0, nt - 1 - i)),
                  _const_spec((1, D_MODEL)), _const_spec((D_MODEL, D_IN_PADDED)),
                  _const_spec((1, D_SGU)), _const_spec((1, D_SGU)),
                  _const_spec((N_GROUPS, SGU_CHUNK, SGU_CHUNK)), _const_spec((N_GROUPS, SGU_CHUNK, SGU_CHUNK)),
                  _const_spec((SGU_CHUNK, D_SGU)), _const_spec((SGU_CHUNK, SGU_CHUNK))],
        out_specs=[rev(tm, D_MODEL), rev(tm, D_MODEL), rev(tm, D_IN_PADDED),
                   _const_spec((1, D_MODEL)), _const_spec((1, LANES)), _const_spec((1, D_SGU)), _const_spec((1, D_SGU)),
                   _const_spec((N_GROUPS, SGU_CHUNK, SGU_CHUNK)), _const_spec((SGU_CHUNK, LANES))],
        out_shape=[SDS((t, D_MODEL), F32), SDS((t, D_MODEL), BF16), SDS((t, D_IN_PADDED), BF16), SDS((1, D_MODEL), F32), SDS((1, LANES), F32), SDS((1, D_SGU), F32),
                   SDS((1, D_SGU), F32), SDS((N_GROUPS, SGU_CHUNK, SGU_CHUNK), F32), SDS((SGU_CHUNK, LANES), F32)],
        scratch_shapes=[pltpu.VMEM((1, LANES), F32), pltpu.VMEM((tm, D_SGU), F32), pltpu.VMEM((tm, D_SGU), F32),
                        pltpu.VMEM((SGU_CHUNK, D_SGU), F32)])


def _pick(n, cap):
    if n <= cap:
        return n
    best = LANES
    for cand in range(LANES, cap + 1, LANES):
        if n % cand == 0:
            best = cand
    return best


def _tn_matmul(a, b, *, bt):
    t, k1 = a.shape
    n = b.shape[1]
    bk = _pick(k1, 1408)
    bn = _pick(n, 1408)
    nsteps = t // bt

    def body(a_ref, b_ref, o_ref):
        @pl.when(pl.program_id(2) == 0)
        def _():
            o_ref[...] = jnp.zeros_like(o_ref)

        o_ref[...] += _dot_tn(a_ref[...], b_ref[...])

    return pl.pallas_call(
        body, name=f"wgrad_{k1}x{n}", grid=(k1 // bk, n // bn, nsteps),
        in_specs=[pl.BlockSpec((bt, bk), lambda i, j, s: (s, i)), pl.BlockSpec((bt, bn), lambda i, j, s: (s, j))],
        out_specs=pl.BlockSpec((bk, bn), lambda i, j, s: (i, j)),
        out_shape=SDS((k1, n), F32),
        compiler_params=_params("arbitrary", "arbitrary", "arbitrary"),
    )(a, b)


def _adamw(parts, w, m, v, *, name):
    rows, cols = w.shape
    br = _pick_rows(rows, cols)
    c1 = 1.0 - ADAM_B1 ** ADAM_STEP
    c2 = 1.0 - ADAM_B2 ** ADAM_STEP

    def body(p_ref, w_ref, m_ref, v_ref, g_ref, d_ref, nm_ref, nv_ref):
        g = p_ref[0].astype(F32)
        for j in range(1, N_DEV):
            g = g + p_ref[j].astype(F32)
        g_ref[...] = g
        nm = ADAM_B1 * m_ref[...] + (1.0 - ADAM_B1) * g
        nv = ADAM_B2 * v_ref[...] + (1.0 - ADAM_B2) * (g * g)
        nm_ref[...] = nm
        nv_ref[...] = nv
        d_ref[...] = -ADAM_LR * ((nm / c1) / (jnp.sqrt(nv / c2) + ADAM_EPS) + ADAM_WD * w_ref[...])

    spec = pl.BlockSpec((br, cols), lambda i: (i, 0))
    return pl.pallas_call(
        body, name=name, grid=(rows // br,),
        in_specs=[pl.BlockSpec((N_DEV, br, cols), lambda i: (0, i, 0)), spec, spec, spec],
        out_specs=[spec] * 4, out_shape=[SDS((rows, cols), F32)] * 4,
        compiler_params=_params("arbitrary"),
    )(parts, w, m, v)


def _pick_rows(rows, cols):
    target = max(8, (256 * 1024) // cols)
    best = 8
    for cand in range(8, min(rows, target) + 1, 8):
        if rows % cand == 0:
            best = cand
    return best


def _peer(k):
    x, y, c = lax.axis_index("x"), lax.axis_index("y"), lax.axis_index("c")
    px = 1 - x if k & 4 else x
    py = 1 - y if k & 2 else y
    pc = 1 - c if k & 1 else c
    return (px, py, pc), 4 * px + 2 * py + pc


def _exchange_scratch(n):
    return [pltpu.SemaphoreType.DMA((N_DEV - 1, n)), pltpu.SemaphoreType.DMA((N_DEV - 1, n)),
            pltpu.SemaphoreType.DMA((n,))]


def _exchange_copies(ins, outs, sems, scatter, landing):
    send_sems, recv_sems, local_sems = sems
    me = 4 * lax.axis_index("x") + 2 * lax.axis_index("y") + lax.axis_index("c")
    copies = [pltpu.make_async_copy(ins[a].at[me] if scatter[a] else ins[a], outs[a].at[me], local_sems.at[a])
              for a in range(len(ins))]
    for k in range(1, N_DEV):
        peer, pidx = _peer(k)
        for a in range(len(ins)):
            copies.append(pltpu.make_async_remote_copy(
                src_ref=ins[a].at[pidx] if scatter[a] else ins[a], dst_ref=outs[a].at[pidx if landing else me],
                send_sem=send_sems.at[k - 1, a], recv_sem=recv_sems.at[k - 1, a], device_id=peer, device_id_type=MESH))
    return copies


def _exchange_start(ins, outs, sems, scatter):
    for cp in _exchange_copies(ins, outs, sems, scatter, landing=False):
        cp.start()


def _exchange_wait(ins, outs, sems, scatter):
    for cp in _exchange_copies(ins, outs, sems, scatter, landing=True):
        cp.wait()


def _exchange_shapes(arrs, scatter):
    return [SDS(a.shape if sc else (N_DEV,) + a.shape, a.dtype) for a, sc in zip(arrs, scatter)]


def _exchange(arrs, scatter, *, name):
    n = len(arrs)

    def body(*refs):
        ins, outs, sems = refs[:n], refs[n:2 * n], refs[2 * n:]
        _exchange_start(ins, outs, sems, scatter)
        _exchange_wait(ins, outs, sems, scatter)

    return pl.pallas_call(
        body, name=name, in_specs=[ANY] * n, out_specs=[ANY] * n, out_shape=_exchange_shapes(arrs, scatter),
        scratch_shapes=_exchange_scratch(n),
    )(*arrs)


def _fused_exchange(body, n_in, n_out, scatter, nsteps):
    n = len(scatter)

    def wrapped(*refs):
        ins, ex_in = refs[:n_in], refs[n_in:n_in + n]
        outs, ex_out = refs[n_in + n:n_in + n + n_out], refs[n_in + n + n_out:n_in + 2 * n + n_out]
        scratch, sems = refs[n_in + 2 * n + n_out:-3], refs[-3:]

        @pl.when(pl.program_id(0) == 0)
        def _():
            _exchange_start(ex_in, ex_out, sems, scatter)

        body(*ins, *outs, *scratch)

        @pl.when(pl.program_id(0) == nsteps - 1)
        def _():
            _exchange_wait(ex_in, ex_out, sems, scatter)

    return wrapped


def _step(x, tgt, small, shards):
    t = x.shape[0]
    tm, tq, tw = _tiles(t)
    r = jnp.arange(SGU_CHUNK, dtype=jnp.int32) // SGU_BLOCK
    mask = (r[None, :] <= r[:, None]).astype(F32)
    layer_shards = lambda l: [shards[n][l] for n in BIG]

    layers = []
    saved = []
    gathered = _exchange(layer_shards(0)[:1], [False], name="gather_weights")
    for l in range(DEPTH):
        w_in = _assemble(gathered[0], BIG[0])
        w_pad = jnp.concatenate([w_in[:, :3 * D_ATTN + N_HEADS], jnp.zeros((D_MODEL, LANES - N_HEADS), BF16),
                                 w_in[:, 3 * D_ATTN + N_HEADS:]], axis=1)
        bf = jnp.pad(small["b_f"][l], (0, LANES - N_HEADS))[None, :]
        wsm = (small["w_s"][l] * mask[None]).astype(BF16)
        wsm_t = jnp.swapaxes(wsm, 1, 2)
        bsf = jnp.repeat(small["b_s"][l].T, GROUP_DIM, axis=1)
        lw = dict(w_in=w_pad, bf=bf, wsm=wsm, wsm_t=wsm_t, bsf=bsf,
                  gmix=small["mix_norm_g"][l][None, :], lng=small["sgu_ln_g"][l][None, :],
                  lnb=small["sgu_ln_b"][l][None, :], gout=small["out_norm_g"][l][None, :],
                  gffn=small["ffn_norm_g"][l][None, :])
        layers.append(lw)
        q, k, v, fl, z, sgu, *late = _fwd_a(x, lw["gmix"], w_pad, bf, lw["lng"], lw["lnb"], wsm, bsf, tm=tm,
                                                 gather=layer_shards(0)[1:] if l == 0 else ())
        w_out, w_gu, w_dn = (_assemble(g, n) for g, n in zip(late if l == 0 else gathered[1:], BIG[1:]))
        lw.update(w_out=w_out, w_gu=w_gu, w_dn=w_dn)
        o, lse = _attn_fwd(q, k, v, tq=tq)
        x1, x2, gu, *gathered = _fwd_b(x, o, sgu, lw["gout"], lw["gffn"], w_out, w_gu, w_dn, tm=tm,
                                       gather=layer_shards(l + 1) if l + 1 < DEPTH else ())
        saved.append(dict(x=x, q=q, k=k, v=v, fl=fl, z=z, sgu=sgu, o=o, lse=lse[:, :2, :], x1=x1, gu=gu))
        x = x2

    dx, loss, dgfin = _loss_bwd(x, tgt, small["final_norm_g"][None, :], tm=tm)
    grads = {n: [None] * DEPTH for n in SMALL if n != "final_norm_g"}
    parts = [None] * DEPTH
    pending = ()
    for l in reversed(range(DEPTH)):
        lw, sv = layers[l], saved[l]
        (dx1, dx2b, a, dgu, xn2, mrg, dx1b, do, delta, dsgu, dgffn, dgout, *landed) = _bwd_b(
            dx, sv["x1"], sv["gu"], sv["o"], sv["sgu"], lw["gout"], lw["gffn"], lw["w_out"], lw["w_gu"], lw["w_dn"],
            tm=tm, scatter=pending)
        if pending:
            parts[l + 1] = landed
        big_grads = {"w_down": _tn_matmul(a, dx2b, bt=tw), "w_gate_up": _tn_matmul(xn2, dgu, bt=tw),
                     "w_out": _tn_matmul(mrg, dx1b, bt=tw)}
        dqt, dk, dv, dck, dcq = _attn_bwd(sv["q"], sv["k"], sv["v"], do, sv["lse"],
                                          delta.reshape(HEAD_PAIRS, 2, t), tq=tq)
        early = [_split(big_grads[n], n).astype(BF16) for n in BIG[1:]] if l == 0 else ()
        (dx, xn, dh, dgmix, dbf, dlng, dlnb, dws, dbs, *landed) = _bwd_a(
            dx1, sv["x"], sv["z"], sv["fl"], dsgu, dqt, dk, dv, (dck + dcq).reshape(N_HEADS, t), lw["gmix"],
            lw["w_in"], lw["lng"], lw["lnb"], lw["wsm"], lw["wsm_t"], lw["bsf"], mask, tm=tm,
            scatter=early)
        g_pad = _tn_matmul(xn, dh, bt=tw)
        big_grads["w_in"] = jnp.concatenate([g_pad[:, :3 * D_ATTN + N_HEADS], g_pad[:, 3 * D_ATTN + LANES:]], axis=1)
        pending = [_split(big_grads[n], n).astype(BF16) for n in (BIG[:1] if l == 0 else BIG)]
        grads["mix_norm_g"][l] = dgmix[0]
        grads["b_f"][l] = dbf[0, :N_HEADS]
        grads["sgu_ln_g"][l] = dlng[0]
        grads["sgu_ln_b"][l] = dlnb[0]
        grads["w_s"][l] = dws
        grads["b_s"][l] = dbs[:, :N_GROUPS].T
        grads["out_norm_g"][l] = dgout[0]
        grads["ffn_norm_g"][l] = dgffn[0]
    grads = {n: jnp.stack(g) for n, g in grads.items()}
    grads["final_norm_g"] = dgfin[0]
    first, small_parts, final_parts = _exchange(
        pending + [_pack(grads, LAYER_SMALL).astype(BF16), _pack(grads, FINAL)], [True, False, False],
        name="scatter_grads")
    parts[0] = [first] + landed
    big_parts = {}
    for a, n in enumerate(BIG):
        stacked = jnp.stack([parts[l][a] for l in range(DEPTH)], axis=1)
        big_parts[n] = stacked.reshape(N_DEV, -1, stacked.shape[-1])
    return loss[0, 0], dx, big_parts, small_parts, final_parts


SMALL = ("mix_norm_g", "b_f", "sgu_ln_g", "sgu_ln_b", "w_s", "b_s", "out_norm_g", "ffn_norm_g", "final_norm_g")
LAYER_SMALL, FINAL = SMALL[:-1], SMALL[-1:]
BIG = ("w_in", "w_out", "w_gate_up", "w_down")
WEIGHTS = ("mix_norm_g", "w_in", "b_f", "sgu_ln_g", "sgu_ln_b", "w_s", "b_s", "out_norm_g", "w_out", "ffn_norm_g",
           "w_gate_up", "w_down", "final_norm_g")
SHARD_AXIS = {"w_in": 1, "w_out": 0, "w_gate_up": 1, "w_down": 0}


def _assemble(gathered, name):
    if SHARD_AXIS[name] == 0:
        return gathered.reshape(-1, gathered.shape[-1])
    return gathered.transpose(1, 0, 2).reshape(gathered.shape[1], -1)


def _split(full, name):
    rows, cols = full.shape
    if SHARD_AXIS[name] == 0:
        return full.reshape(N_DEV, rows // N_DEV, cols)
    return full.reshape(rows, N_DEV, cols // N_DEV).transpose(1, 0, 2)


def _pack(tree, names):
    flat = jnp.concatenate([tree[n].reshape(-1) for n in names])
    pad = (-flat.shape[0]) % (16 * LANES)
    return jnp.pad(flat, (0, pad)).reshape(-1, LANES)


def _unpack(packed, like, names):
    flat = packed.reshape(-1)
    out, at = {}, 0
    for n in names:
        size = like[n].size
        out[n] = flat[at:at + size].reshape(like[n].shape)
        at += size
    return out


def kernel(x, mix_norm_g, w_in, b_f, sgu_ln_g, sgu_ln_b, w_s, b_s, out_norm_g, w_out, ffn_norm_g, w_gate_up, w_down, final_norm_g, loss_target, m_mix_norm_g, m_w_in, m_b_f, m_sgu_ln_g, m_sgu_ln_b, m_w_s, m_b_s, m_out_norm_g, m_w_out, m_ffn_norm_g, m_w_gate_up, m_w_down, m_final_norm_g, v_mix_norm_g, v_w_in, v_b_f, v_sgu_ln_g, v_sgu_ln_b, v_w_s, v_b_s, v_out_norm_g, v_w_out, v_ffn_norm_g, v_w_gate_up, v_w_down, v_final_norm_g):
    w = dict(mix_norm_g=mix_norm_g, w_in=w_in, b_f=b_f, sgu_ln_g=sgu_ln_g, sgu_ln_b=sgu_ln_b, w_s=w_s, b_s=b_s,
             out_norm_g=out_norm_g, w_out=w_out, ffn_norm_g=ffn_norm_g, w_gate_up=w_gate_up, w_down=w_down,
             final_norm_g=final_norm_g)
    m = dict(mix_norm_g=m_mix_norm_g, w_in=m_w_in, b_f=m_b_f, sgu_ln_g=m_sgu_ln_g, sgu_ln_b=m_sgu_ln_b, w_s=m_w_s,
             b_s=m_b_s, out_norm_g=m_out_norm_g, w_out=m_w_out, ffn_norm_g=m_ffn_norm_g, w_gate_up=m_w_gate_up,
             w_down=m_w_down, final_norm_g=m_final_norm_g)
    v = dict(mix_norm_g=v_mix_norm_g, w_in=v_w_in, b_f=v_b_f, sgu_ln_g=v_sgu_ln_g, sgu_ln_b=v_sgu_ln_b, w_s=v_w_s,
             b_s=v_b_s, out_norm_g=v_out_norm_g, w_out=v_w_out, ffn_norm_g=v_ffn_norm_g, w_gate_up=v_w_gate_up,
             w_down=v_w_down, final_norm_g=v_final_norm_g)

    loss, dx, big_parts, small_parts, final_parts = _step(x[0], loss_target[0], {n: w[n] for n in SMALL},
                                             {n: w[n].astype(BF16) for n in BIG})
    loss = lax.psum(loss, ("x", "y", "c"))

    g_out, d_out, m_out, v_out = {}, {}, {}, {}
    for n in BIG:
        shape = w[n].shape
        two_d = lambda a: a.reshape(-1, shape[-1])
        res = _adamw(big_parts[n], two_d(w[n]), two_d(m[n]), two_d(v[n]), name=f"adamw_{n}")
        g_out[n], d_out[n], m_out[n], v_out[n] = (r.reshape(shape) for r in res)
    for names, parts, name in ((LAYER_SMALL, small_parts, "adamw_small"), (FINAL, final_parts, "adamw_final")):
        res = _adamw(parts, _pack(w, names), _pack(m, names), _pack(v, names), name=name)
        for dst, packed in zip((g_out, d_out, m_out, v_out), res):
            dst.update(_unpack(packed, w, names))

    return (loss, dx[None], *[g_out[n] for n in WEIGHTS], *[d_out[n] for n in WEIGHTS],
            *[m_out[n] for n in WEIGHTS], *[v_out[n] for n in WEIGHTS])
```
